```python
import math
import jax
import jax.numpy as jnp
from jax import lax
import numpy as np

D_MODEL = 1024
BATCH = 8
SEQ = 2048
DEPTH = 2

GRID_W = 64
CTX_LEN = 256
EPS = 1e-6
N_MOD = 6
N_BRANCH = 3
SSM_WIDTH = D_MODEL // 2
SSM_GROUP = 16
SSM_GROUPS = SSM_WIDTH // SSM_GROUP
SSM_STATE = 64
DT_MIN = 1e-3
DT_MAX = 1e-1
ATT_HEADS = 8
ATT_KV_HEADS = 2
ATT_GROUPS = ATT_HEADS // ATT_KV_HEADS
ATT_HEAD_DIM = 64
ATT_WIDTH = ATT_HEADS * ATT_HEAD_DIM
KV_WIDTH = ATT_KV_HEADS * ATT_HEAD_DIM
ROPE_FREQS = ATT_HEAD_DIM // 4
ROPE_BASE = 10000.0
Q_BLOCK = 128
ML_HEADS = 4
ML_HEAD_DIM = 128
ML_WIDTH = ML_HEADS * ML_HEAD_DIM
ML_GATES = 2 * 2 * ML_HEADS
ML_CHUNK = 128
PEER_HEADS = 8
PEER_KEYS = 128
PEER_EXPERTS = PEER_KEYS * PEER_KEYS
PEER_QDIM = 256
PEER_TOPK = 16
PEER_TOKEN_BLOCK = 128
IN_SPLITS = (SSM_WIDTH, ATT_WIDTH, KV_WIDTH, KV_WIDTH, ML_WIDTH, ML_WIDTH, ML_WIDTH, ML_WIDTH, ML_GATES, N_BRANCH * D_MODEL)
IN_WIDTH = SSM_WIDTH + ATT_WIDTH + 2 * KV_WIDTH + 4 * ML_WIDTH + ML_GATES + N_BRANCH * D_MODEL

kernel_name = 'hybrid_s5_gqa_mlstm_peer_dit_block'


def rmsnorm(x, g):
    xf = x.astype(jnp.float32)
    y = xf * lax.rsqrt(jnp.mean(xf * xf, axis=-1, keepdims=True) + EPS)
    return (y * g.astype(jnp.float32)).astype(x.dtype)


def flip_segments(z, n_ctx):
    zc, zl = jnp.split(z, [n_ctx], axis=1)
    return jnp.concatenate([jnp.flip(zc, axis=1), jnp.flip(zl, axis=1)], axis=1)


def axial_rope_tables(seq_len):
    rows = seq_len // GRID_W
    row = jnp.repeat(jnp.arange(rows, dtype=jnp.float32), GRID_W)
    col = jnp.tile(jnp.arange(GRID_W, dtype=jnp.float32), rows)
    inv = ROPE_BASE ** (-jnp.arange(ROPE_FREQS, dtype=jnp.float32) / ROPE_FREQS)
    ang = jnp.stack([row[:, None] * inv, col[:, None] * inv], axis=1)
    return jnp.cos(ang), jnp.sin(ang)


def apply_axial_rope(x, cos, sin):
    n_head_axes = x.ndim - 3
    cs_shape = (cos.shape[0],) + (1,) * n_head_axes + (2, ROPE_FREQS)
    c = cos.reshape(cs_shape).astype(x.dtype)
    s = sin.reshape(cs_shape).astype(x.dtype)
    xr = x.reshape(x.shape[:-1] + (2, 2, ROPE_FREQS))
    x1 = xr[..., 0, :]
    x2 = xr[..., 1, :]
    return jnp.stack([x1 * c - x2 * s, x1 * s + x2 * c], axis=-2).reshape(x.shape)


def complex_affine_combine(left, right):
    al_re, al_im, bl_re, bl_im = left
    ar_re, ar_im, br_re, br_im = right
    return (ar_re * al_re - ar_im * al_im,
            ar_re * al_im + ar_im * al_re,
            ar_re * bl_re - ar_im * bl_im + br_re,
            ar_re * bl_im + ar_im * bl_re + br_im)


def s5_scan(ug, lam_re, lam_im, log_step, b_re, b_im):
    f32 = jnp.float32
    lam_re = lam_re.astype(f32)
    lam_im = lam_im.astype(f32)
    b_re = b_re.astype(f32)
    b_im = b_im.astype(f32)
    step = jnp.exp(log_step.astype(f32))[:, None]
    mag = jnp.exp(lam_re * step)
    a_re = mag * jnp.cos(lam_im * step)
    a_im = mag * jnp.sin(lam_im * step)
    den = lam_re * lam_re + lam_im * lam_im
    z_re = ((a_re - 1.0) * lam_re + a_im * lam_im) / den
    z_im = (a_im * lam_re - (a_re - 1.0) * lam_im) / den
    bb_re = z_re[..., None] * b_re - z_im[..., None] * b_im
    bb_im = z_re[..., None] * b_im + z_im[..., None] * b_re
    u = ug.astype(f32)
    bu_re = jnp.einsum('bsgc,gnc->bsgn', u, bb_re)
    bu_im = jnp.einsum('bsgc,gnc->bsgn', u, bb_im)
    s_len = u.shape[1]
    a_re_s = jnp.broadcast_to(a_re, (1, s_len) + a_re.shape)
    a_im_s = jnp.broadcast_to(a_im, (1, s_len) + a_im.shape)
    _, _, s_re, s_im = lax.associative_scan(complex_affine_combine, (a_re_s, a_im_s, bu_re, bu_im), axis=1)
    return s_re, s_im


def s5_branch(u, n_ctx, lam_re, lam_im, log_step, b_re, b_im, c_re, c_im, d_skip, w_glu, latent_only):
    f32 = jnp.float32
    bsz, s_len, _ = u.shape
    ug = u.astype(f32).reshape(bsz, s_len, SSM_GROUPS, SSM_GROUP)
    y = d_skip.astype(f32) * u.astype(f32)
    for direction in range(2):
        seq = ug if direction == 0 else flip_segments(ug, n_ctx)
        s_re, s_im = s5_scan(seq, lam_re[direction], lam_im[direction], log_step[direction], b_re[direction], b_im[direction])
        yd = (jnp.einsum('bsgn,gcn->bsgc', s_re, c_re[direction].astype(f32))
              - jnp.einsum('bsgn,gcn->bsgc', s_im, c_im[direction].astype(f32))).reshape(bsz, s_len, SSM_WIDTH)
        if direction == 1:
            yd = flip_segments(yd, n_ctx)
        y = y + yd
    if latent_only:
        y = y[:, n_ctx:]
    y = jax.nn.gelu(y).astype(u.dtype)
    return y * jax.nn.sigmoid(y @ w_glu)


def softmax_attend(qb, kk, vv):
    s = jnp.einsum('bqhgd,bkhd->bhgqk', qb, kk).astype(jnp.float32) * (ATT_HEAD_DIM ** -0.5)
    p = jax.nn.softmax(s, axis=-1).astype(vv.dtype)
    return jnp.einsum('bhgqk,bkhd->bqhgd', p, vv)


def attention_branch(q, k, v, n_ctx, cos, sin, q_norm_g, k_norm_g, latent_only):
    bsz, s_len, _ = q.shape
    lat_len = s_len - n_ctx
    q = rmsnorm(q.reshape(bsz, s_len, ATT_KV_HEADS, ATT_GROUPS, ATT_HEAD_DIM), q_norm_g)
    k = rmsnorm(k.reshape(bsz, s_len, ATT_KV_HEADS, ATT_HEAD_DIM), k_norm_g)
    v = v.reshape(bsz, s_len, ATT_KV_HEADS, ATT_HEAD_DIM)
    q_lat = apply_axial_rope(q[:, n_ctx:], cos, sin)
    k_all = jnp.concatenate([k[:, :n_ctx], apply_axial_rope(k[:, n_ctx:], cos, sin)], axis=1)
    n_blk = lat_len // Q_BLOCK
    q_blocks = jnp.moveaxis(q_lat.reshape(bsz, n_blk, Q_BLOCK, ATT_KV_HEADS, ATT_GROUPS, ATT_HEAD_DIM), 1, 0)
    o_lat = lax.map(lambda qb: softmax_attend(qb, k_all, v), q_blocks)
    o_lat = jnp.moveaxis(o_lat, 0, 1).reshape(bsz, lat_len, ATT_WIDTH)
    if latent_only:
        return o_lat
    o_ctx = softmax_attend(q[:, :n_ctx], k[:, :n_ctx], v[:, :n_ctx]).reshape(bsz, n_ctx, ATT_WIDTH)
    return jnp.concatenate([o_ctx, o_lat], axis=1)


def mlstm_chunkwise(q, k, v, log_i, log_f):
    bsz, n_h, s_len, dk = q.shape
    dv = v.shape[-1]
    nc = s_len // ML_CHUNK

    def chunks(z):
        return jnp.moveaxis(z.reshape((bsz, n_h, nc, ML_CHUNK) + z.shape[3:]), 2, 0)

    tri = jnp.tril(jnp.ones((ML_CHUNK, ML_CHUNK), dtype=bool))

    def step(carry, inp):
        c_st, n_st, m_st = carry
        qt, kt, vt, it, ft = inp
        b = jnp.cumsum(ft, axis=-1)
        dmat = jnp.where(tri, b[..., :, None] - b[..., None, :] + it[..., None, :], -jnp.inf)
        inter = b + m_st[..., None]
        m_t = jnp.maximum(inter, jnp.max(dmat, axis=-1))
        w = jnp.exp(dmat - m_t[..., None])
        a_inter = jnp.exp(inter - m_t)
        qk = jnp.einsum('bhtd,bhsd->bhts', qt, kt) * w
        num = a_inter[..., None] * jnp.einsum('bhtd,bhvd->bhtv', qt, c_st) + jnp.einsum('bhts,bhsv->bhtv', qk, vt)
        den = a_inter * jnp.einsum('bhtd,bhd->bht', qt, n_st) + jnp.sum(qk, axis=-1)
        h_out = num / jnp.maximum(jnp.abs(den), jnp.exp(-m_t))[..., None]
        b_last = b[..., -1]
        d_last = b_last[..., None] - b + it
        m_new = jnp.maximum(b_last + m_st, jnp.max(d_last, axis=-1))
        w_last = jnp.exp(d_last - m_new[..., None])
        decay = jnp.exp(b_last + m_st - m_new)
        c_new = decay[..., None, None] * c_st + jnp.einsum('bhs,bhsv,bhsd->bhvd', w_last, vt, kt)
        n_new = decay[..., None] * n_st + jnp.einsum('bhs,bhsd->bhd', w_last, kt)
        return (c_new, n_new, m_new), h_out

    f32 = jnp.float32
    init = (jnp.zeros((bsz, n_h, dv, dk), f32), jnp.zeros((bsz, n_h, dk), f32), jnp.zeros((bsz, n_h), f32))
    _, hs = lax.scan(step, init, (chunks(q), chunks(k), chunks(v), chunks(log_i), chunks(log_f)))
    return jnp.moveaxis(hs, 0, 2).reshape(bsz, n_h, s_len, dv)


def mlstm_branch(q, k, v, o, gate_pre, n_ctx, gate_b, norm_g, latent_only):
    f32 = jnp.float32
    bsz, s_len, _ = q.shape
    gates = gate_pre.reshape(bsz, s_len, 2, 2, ML_HEADS).astype(f32) + gate_b.astype(f32)

    def to_heads(z):
        return z.reshape(bsz, s_len, ML_HEADS, ML_HEAD_DIM).transpose(0, 2, 1, 3).astype(f32)

    outs = []
    for direction in range(2):
        qd, kd, vd, gd = q, k, v, gates[:, :, direction]
        if direction == 1:
            qd, kd, vd, gd = flip_segments(qd, n_ctx), flip_segments(kd, n_ctx), flip_segments(vd, n_ctx), flip_segments(gd, n_ctx)
        log_i = gd[:, :, 0].transpose(0, 2, 1)
        log_f = jax.nn.log_sigmoid(gd[:, :, 1]).transpose(0, 2, 1)
        hd = mlstm_chunkwise(to_heads(qd), to_heads(kd) * (ML_HEAD_DIM ** -0.5), to_heads(vd), log_i, log_f)
        hd = hd.transpose(0, 2, 1, 3)
        if direction == 1:
            hd = flip_segments(hd, n_ctx)
        outs.append(hd)
    h_sum = outs[0] + outs[1]
    if latent_only:
        h_sum = h_sum[:, n_ctx:]
        o = o[:, n_ctx:]
    hn = rmsnorm(h_sum, norm_g.reshape(ML_HEADS, ML_HEAD_DIM))
    return (hn.reshape(hn.shape[0], hn.shape[1], ML_WIDTH) * jax.nn.sigmoid(o.astype(f32))).astype(q.dtype)


def hybrid_mixer(h, n_ctx, cos, sin, w_in, lam_re, lam_im, log_step, b_re, b_im, c_re, c_im, d_skip, w_glu,
                 q_norm_g, k_norm_g, ml_gate_b, ml_norm_g, w_br_ssm, w_br_attn, w_br_mlstm, w_out, latent_only):
    proj = h @ w_in
    offsets = np.cumsum(IN_SPLITS)[:-1].tolist()
    u_s, q_a, k_a, v_a, q_m, k_m, v_m, o_m, g_m, gate_logits = jnp.split(proj, offsets, axis=-1)
    y_s = s5_branch(u_s, n_ctx, lam_re, lam_im, log_step, b_re, b_im, c_re, c_im, d_skip, w_glu, latent_only)
    y_a = attention_branch(q_a, k_a, v_a, n_ctx, cos, sin, q_norm_g, k_norm_g, latent_only)
    y_m = mlstm_branch(q_m, k_m, v_m, o_m, g_m, n_ctx, ml_gate_b, ml_norm_g, latent_only)
    if latent_only:
        gate_logits = gate_logits[:, n_ctx:]
    gates = jax.nn.sigmoid(gate_logits.astype(jnp.float32)).astype(h.dtype)
    gates = gates.reshape(gate_logits.shape[:-1] + (N_BRANCH, D_MODEL))
    merged = (gates[..., 0, :] * (y_s @ w_br_ssm) + gates[..., 1, :] * (y_a @ w_br_attn)
              + gates[..., 2, :] * (y_m @ w_br_mlstm))
    return merged @ w_out


def peer_ffn(h, w_q, sub_k1, sub_k2, u_tab, v_tab):
    bsz, s_len, d = h.shape
    tok = h.reshape(bsz * s_len, d)
    half = PEER_QDIM // 2
    q = (tok @ w_q).reshape(-1, PEER_HEADS, 2, half)
    s1 = jnp.einsum('nhd,kd->nhk', q[:, :, 0], sub_k1).astype(jnp.float32)
    s2 = jnp.einsum('nhd,kd->nhk', q[:, :, 1], sub_k2).astype(jnp.float32)
    v1, i1 = lax.top_k(s1, PEER_TOPK)
    v2, i2 = lax.top_k(s2, PEER_TOPK)
    n_cand = PEER_TOPK * PEER_TOPK
    cand_s = (v1[..., :, None] + v2[..., None, :]).reshape(-1, PEER_HEADS, n_cand)
    cand_e = (i1[..., :, None] * PEER_KEYS + i2[..., None, :]).reshape(-1, PEER_HEADS, n_cand)
    top_s, top_j = lax.top_k(cand_s, PEER_TOPK)
    experts = jnp.take_along_axis(cand_e, top_j, axis=-1)
    gate = jax.nn.softmax(top_s, axis=-1)
    n_blk = tok.shape[0] // PEER_TOKEN_BLOCK

    def block(args):
        xb, eb, gb = args
        act = jax.nn.gelu(jnp.einsum('thkd,td->thk', u_tab[eb], xb).astype(jnp.float32))
        coef = (gb * act).astype(xb.dtype)
        return jnp.einsum('thk,thkd->td', coef, v_tab[eb])

    out = lax.map(block, (tok.reshape(n_blk, PEER_TOKEN_BLOCK, d),
                          experts.reshape(n_blk, PEER_TOKEN_BLOCK, PEER_HEADS, PEER_TOPK),
                          gate.reshape(n_blk, PEER_TOKEN_BLOCK, PEER_HEADS, PEER_TOPK)))
    return out.reshape(bsz, s_len, d)


def setup_inputs(seed: int = 0) -> dict:
    key = jax.random.key(seed)
    ks = list(jax.random.split(key, 40))
    f32 = jnp.float32

    def nrm(shape, scale):
        return jax.random.normal(ks.pop(), shape, f32) * scale

    x = nrm((BATCH, SEQ, D_MODEL), 1.0)
    c = nrm((BATCH, D_MODEL), 1.0)
    ctx = nrm((BATCH, CTX_LEN, D_MODEL), 1.0)
    c_ctx = nrm((D_MODEL,), 1.0)
    w_mod = nrm((DEPTH, D_MODEL, N_MOD * D_MODEL), 0.5 * D_MODEL ** -0.5)
    b_mod = nrm((DEPTH, N_MOD * D_MODEL), 0.01)
    norm1_g = 1.0 + nrm((DEPTH, D_MODEL), 0.02)
    norm2_g = 1.0 + nrm((DEPTH, D_MODEL), 0.02)
    w_in = nrm((DEPTH, D_MODEL, IN_WIDTH), D_MODEL ** -0.5)
    ssm_lam_re = -0.5 + nrm((DEPTH, 2, SSM_GROUPS, SSM_STATE), 0.01)
    ssm_lam_im = math.pi * jnp.arange(SSM_STATE, dtype=f32) + nrm((DEPTH, 2, SSM_GROUPS, SSM_STATE), 0.01)
    ssm_log_step = jax.random.uniform(ks.pop(), (DEPTH, 2, SSM_GROUPS), f32, math.log(DT_MIN), math.log(DT_MAX))
    ssm_b_re = nrm((DEPTH, 2, SSM_GROUPS, SSM_STATE, SSM_GROUP), (2 * SSM_GROUP) ** -0.5)
    ssm_b_im = nrm((DEPTH, 2, SSM_GROUPS, SSM_STATE, SSM_GROUP), (2 * SSM_GROUP) ** -0.5)
    ssm_c_re = nrm((DEPTH, 2, SSM_GROUPS, SSM_GROUP, SSM_STATE), SSM_STATE ** -0.5)
    ssm_c_im = nrm((DEPTH, 2, SSM_GROUPS, SSM_GROUP, SSM_STATE), SSM_STATE ** -0.5)
    ssm_d = nrm((DEPTH, SSM_WIDTH), 1.0)
    ssm_w_glu = nrm((DEPTH, SSM_WIDTH, SSM_WIDTH), SSM_WIDTH ** -0.5)
    attn_q_norm_g = 1.0 + nrm((DEPTH, ATT_HEAD_DIM), 0.02)
    attn_k_norm_g = 1.0 + nrm((DEPTH, ATT_HEAD_DIM), 0.02)
    gate_i = nrm((DEPTH, 2, ML_HEADS), 0.1)
    gate_f = jnp.linspace(3.0, 6.0, ML_HEADS, dtype=f32) + nrm((DEPTH, 2, ML_HEADS), 0.1)
    mlstm_gate_b = jnp.stack([gate_i, gate_f], axis=2)
    mlstm_norm_g = 1.0 + nrm((DEPTH, ML_WIDTH), 0.02)
    w_branch_ssm = nrm((DEPTH, SSM_WIDTH, D_MODEL), SSM_WIDTH ** -0.5)
    w_branch_attn = nrm((DEPTH, ATT_WIDTH, D_MODEL), ATT_WIDTH ** -0.5)
    w_branch_mlstm = nrm((DEPTH, ML_WIDTH, D_MODEL), ML_WIDTH ** -0.5)
    w_out = nrm((DEPTH, D_MODEL, D_MODEL), D_MODEL ** -0.5)
    peer_w_q = nrm((DEPTH, D_MODEL, PEER_HEADS * PEER_QDIM), D_MODEL ** -0.5)
    peer_sub_k1 = nrm((DEPTH, PEER_KEYS, PEER_QDIM // 2), (PEER_QDIM // 2) ** -0.5)
    peer_sub_k2 = nrm((DEPTH, PEER_KEYS, PEER_QDIM // 2), (PEER_QDIM // 2) ** -0.5)
    peer_u = nrm((DEPTH, PEER_EXPERTS, D_MODEL), D_MODEL ** -0.5)
    peer_v = nrm((DEPTH, PEER_EXPERTS, D_MODEL), 1.0)
    final_norm_g = 1.0 + nrm((D_MODEL,), 0.02)
    return {'x': x, 'c': c, 'ctx': ctx, 'c_ctx': c_ctx, 'w_mod': w_mod, 'b_mod': b_mod,
            'norm1_g': norm1_g, 'norm2_g': norm2_g, 'w_in': w_in,
            'ssm_lam_re': ssm_lam_re, 'ssm_lam_im': ssm_lam_im, 'ssm_log_step': ssm_log_step,
            'ssm_b_re': ssm_b_re, 'ssm_b_im': ssm_b_im, 'ssm_c_re': ssm_c_re, 'ssm_c_im': ssm_c_im,
            'ssm_d': ssm_d, 'ssm_w_glu': ssm_w_glu, 'attn_q_norm_g': attn_q_norm_g, 'attn_k_norm_g': attn_k_norm_g,
            'mlstm_gate_b': mlstm_gate_b, 'mlstm_norm_g': mlstm_norm_g,
            'w_branch_ssm': w_branch_ssm, 'w_branch_attn': w_branch_attn, 'w_branch_mlstm': w_branch_mlstm,
            'w_out': w_out, 'peer_w_q': peer_w_q, 'peer_sub_k1': peer_sub_k1, 'peer_sub_k2': peer_sub_k2,
            'peer_u': peer_u, 'peer_v': peer_v, 'final_norm_g': final_norm_g}


def reference(x, c, ctx, c_ctx, w_mod, b_mod, norm1_g, norm2_g, w_in,
              ssm_lam_re, ssm_lam_im, ssm_log_step, ssm_b_re, ssm_b_im, ssm_c_re, ssm_c_im,
              ssm_d, ssm_w_glu, attn_q_norm_g, attn_k_norm_g, mlstm_gate_b, mlstm_norm_g,
              w_branch_ssm, w_branch_attn, w_branch_mlstm, w_out,
              peer_w_q, peer_sub_k1, peer_sub_k2, peer_u, peer_v, final_norm_g):
    n_ctx = ctx.shape[1]
    seq_len = x.shape[1]
    cos, sin = axial_rope_tables(seq_len)
    stream = jnp.concatenate([ctx, x], axis=1)
    silu_c = jax.nn.silu(c)
    silu_ctx = jax.nn.silu(c_ctx)
    for layer in range(DEPTH):
        last = layer == DEPTH - 1
        mod_l = jnp.split(silu_c @ w_mod[layer] + b_mod[layer], N_MOD, axis=-1)
        mod_c = jnp.split(silu_ctx @ w_mod[layer] + b_mod[layer], N_MOD, axis=-1)
        sh1_l, sc1_l, g1_l, sh2_l, sc2_l, g2_l = [m[:, None, :] for m in mod_l]
        sh1_c, sc1_c, g1_c, sh2_c, sc2_c, g2_c = mod_c
        hn = rmsnorm(stream, norm1_g[layer])
        h = jnp.concatenate([hn[:, :n_ctx] * (1 + sc1_c) + sh1_c, hn[:, n_ctx:] * (1 + sc1_l) + sh1_l], axis=1)
        mix = hybrid_mixer(h, n_ctx, cos, sin, w_in[layer], ssm_lam_re[layer], ssm_lam_im[layer], ssm_log_step[layer],
                           ssm_b_re[layer], ssm_b_im[layer], ssm_c_re[layer], ssm_c_im[layer], ssm_d[layer], ssm_w_glu[layer],
                           attn_q_norm_g[layer], attn_k_norm_g[layer], mlstm_gate_b[layer], mlstm_norm_g[layer],
                           w_branch_ssm[layer], w_branch_attn[layer], w_branch_mlstm[layer], w_out[layer], last)
        if last:
            lat = stream[:, n_ctx:] + g1_l * mix
            h2 = rmsnorm(lat, norm2_g[layer]) * (1 + sc2_l) + sh2_l
            stream = lat + g2_l * peer_ffn(h2, peer_w_q[layer], peer_sub_k1[layer], peer_sub_k2[layer], peer_u[layer], peer_v[layer])
        else:
            stream = stream + jnp.concatenate([g1_c * mix[:, :n_ctx], g1_l * mix[:, n_ctx:]], axis=1)
            hn2 = rmsnorm(stream, norm2_g[layer])
            h2 = jnp.concatenate([hn2[:, :n_ctx] * (1 + sc2_c) + sh2_c, hn2[:, n_ctx:] * (1 + sc2_l) + sh2_l], axis=1)
            f = peer_ffn(h2, peer_w_q[layer], peer_sub_k1[layer], peer_sub_k2[layer], peer_u[layer], peer_v[layer])
            stream = stream + jnp.concatenate([g2_c * f[:, :n_ctx], g2_l * f[:, n_ctx:]], axis=1)
    return rmsnorm(stream, final_norm_g)
```

```python
import functools
import math

import jax
import jax.numpy as jnp
from jax import lax
from jax.experimental import pallas as pl
from jax.experimental.pallas import tpu as pltpu

F32 = jnp.float32
BF16 = jnp.bfloat16

D_MODEL = 1024
GRID_W = 64
EPS = 1e-6
N_MOD = 6
N_BRANCH = 3
SSM_WIDTH = 512
SSM_GROUP = 16
SSM_GROUPS = 32
SSM_STATE = 64
ATT_HEADS = 8
ATT_KV_HEADS = 2
ATT_HEAD_DIM = 64
ATT_WIDTH = 512
KV_WIDTH = 128
ROPE_FREQS = 16
ROPE_BASE = 10000.0
ML_HEADS = 4
ML_HEAD_DIM = 128
ML_WIDTH = 512
ML_GATES = 16
ML_CHUNK = 128
PEER_HEADS = 8
PEER_KEYS = 128
PEER_EXPERTS = PEER_KEYS * PEER_KEYS
PEER_QDIM = 256
PEER_TOPK = 16

LANES = 128
VMEM_LIMIT_BYTES = 56 * 1024 * 1024

PROJ_BLOCK = 512
COL_U, COL_QA, COL_QM, COL_KM, COL_VM, COL_OM = (i * PROJ_BLOCK for i in range(6))
COL_GATE = 6 * PROJ_BLOCK
COL_KA = COL_GATE + N_BRANCH * D_MODEL
COL_VA = COL_KA + KV_WIDTH
COL_GM = COL_VA + KV_WIDTH
PROJ_WIDTH = COL_KA + PROJ_BLOCK

S5_BLOCKS = 4
S5_BLOCK_IN = SSM_WIDTH // S5_BLOCKS
S5_BLOCK_STATE = SSM_GROUPS * SSM_STATE // S5_BLOCKS
S5_CHUNK = 64

NEG_INF = float("-inf")


def _cparams(sem):
    return pltpu.CompilerParams(dimension_semantics=sem, vmem_limit_bytes=VMEM_LIMIT_BYTES)


def _split2(x):
    hi = x.astype(BF16)
    lo = (x - hi.astype(F32)).astype(BF16)
    return hi, lo


def _split3(x):
    hi = x.astype(BF16)
    r = x - hi.astype(F32)
    mid = r.astype(BF16)
    lo = (r - mid.astype(F32)).astype(BF16)
    return hi, mid, lo


def _dot(a, b):
    return jnp.dot(a, b, preferred_element_type=F32)


def _dot_nt(a, b):
    return lax.dot_general(a, b, (((1,), (1,)), ((), ())), preferred_element_type=F32)


def _dot_exact_rhs(x, m_bf16):
    hi, mid, lo = _split3(x)
    return _dot(hi, m_bf16) + _dot(mid, m_bf16) + _dot(lo, m_bf16)


def _dot_exact_lhs(m_bf16, x):
    hi, mid, lo = _split3(x)
    return _dot(m_bf16, hi) + _dot(m_bf16, mid) + _dot(m_bf16, lo)


def _sigmoid(x):
    return 1.0 / (1.0 + jnp.exp(-x))


def _gelu(x):
    return jax.nn.gelu(x, approximate=True)


def _mod_kernel(v_ref, w_ref, b_ref, o_ref):
    v = v_ref[...]
    sv = v * _sigmoid(v)
    w = w_ref[0]
    hi, mid, lo = _split3(sv)
    whi, wlo = _split2(w)
    acc = _dot(hi, whi) + _dot(mid, whi) + _dot(hi, wlo) + _dot(lo, whi) + _dot(mid, wlo)
    o_ref[0] = acc + b_ref[0]


def _modulation(c, c_ctx, w_mod, b_mod):
    depth = w_mod.shape[0]
    n_out = w_mod.shape[2]
    rows = 16
    v = jnp.zeros((rows, D_MODEL), F32).at[: c.shape[0]].set(c).at[8].set(c_ctx)
    tn = 1536
    return pl.pallas_call(
        _mod_kernel,
        grid=(depth, n_out // tn),
        in_specs=[pl.BlockSpec((rows, D_MODEL), lambda l, j: (0, 0)),
                  pl.BlockSpec((1, D_MODEL, tn), lambda l, j: (l, 0, j)),
                  pl.BlockSpec((1, 1, tn), lambda l, j: (l, 0, j))],
        out_specs=pl.BlockSpec((1, rows, tn), lambda l, j: (l, 0, j)),
        out_shape=jax.ShapeDtypeStruct((depth, rows, n_out), F32),
        compiler_params=_cparams(("arbitrary", "arbitrary")),
        name="modulation",
    )(v, w_mod, b_mod.reshape(depth, 1, n_out))


CTX_MOD_ROW = 8


def _mod_rows(mod_ref, which, b, is_ctx_col):
    lo = which * D_MODEL
    m_l = mod_ref[pl.ds(b, 1), lo:lo + D_MODEL]
    if is_ctx_col is None:
        return m_l
    m_c = mod_ref[CTX_MOD_ROW:CTX_MOD_ROW + 1, lo:lo + D_MODEL]
    return jnp.where(is_ctx_col, m_c, m_l)


def _rms(x, g):
    ms = jnp.mean(x * x, axis=-1, keepdims=True)
    return x * lax.rsqrt(ms + EPS) * g


def _inproj_kernel(x_ref, g_ref, mod_ref, w_ref, o_ref, h_scr, *, n_ctx, chunk):
    b = pl.program_id(0)
    j = pl.program_id(1)
    s_tot = x_ref.shape[1]

    @pl.when(j == 0)
    def _():
        for r0 in range(0, s_tot, chunk):
            x = x_ref[0, r0:r0 + chunk, :]
            xn = _rms(x, g_ref[...])
            row = r0 + lax.broadcasted_iota(jnp.int32, (chunk, 1), 0)
            is_ctx = row < n_ctx
            sh = _mod_rows(mod_ref, 0, b, is_ctx)
            sc = _mod_rows(mod_ref, 1, b, is_ctx)
            h_scr[r0:r0 + chunk, :] = (xn * (1.0 + sc) + sh).astype(BF16)

    o_ref[...] = _dot(h_scr[...], w_ref[...])


def _inproj(stream3, g, mod_l, w_bf16, n_ctx):
    bsz, s_tot, d = stream3.shape
    n_out = w_bf16.shape[1]
    tn = PROJ_BLOCK
    return pl.pallas_call(
        functools.partial(_inproj_kernel, n_ctx=n_ctx, chunk=256),
        grid=(bsz, n_out // tn),
        in_specs=[pl.BlockSpec((1, s_tot, d), lambda b, j: (b, 0, 0)),
                  pl.BlockSpec((1, d), lambda b, j: (0, 0)),
                  pl.BlockSpec(mod_l.shape, lambda b, j: (0, 0)),
                  pl.BlockSpec((d, tn), lambda b, j: (0, j))],
        out_specs=pl.BlockSpec((s_tot, tn), lambda b, j: (b, j)),
        out_shape=jax.ShapeDtypeStruct((bsz * s_tot, n_out), F32),
        scratch_shapes=[pltpu.VMEM((s_tot, d), BF16)],
        compiler_params=_cparams(("arbitrary", "arbitrary")),
        name="inproj",
    )(stream3, g.reshape(1, d), mod_l, w_bf16)


def _s5_param_kernel(lre_ref, lim_ref, ls_ref, bre_ref, bim_ref, are_ref, aim_ref, bbre_ref, bbim_ref):
    lre = lre_ref[...]
    lim = lim_ref[...]
    step = jnp.exp(ls_ref[...])
    mag = jnp.exp(lre * step)
    a_re = mag * jnp.cos(lim * step)
    a_im = mag * jnp.sin(lim * step)
    den = lre * lre + lim * lim
    z_re = ((a_re - 1.0) * lre + a_im * lim) / den
    z_im = (a_im * lre - (a_re - 1.0) * lim) / den
    b_re = bre_ref[...]
    b_im = bim_ref[...]
    are_ref[...] = a_re
    aim_ref[...] = a_im
    bbre_ref[...] = z_re * b_re - z_im * b_im
    bbim_ref[...] = z_re * b_im + z_im * b_re


def _s5_params(lam_re, lam_im, log_step, b_re, b_im, c_re, c_im):
    nd, g, n = lam_re.shape
    c = b_re.shape[-1]
    rows = nd * g
    wide = n * c

    def expand(z):
        return jnp.broadcast_to(z.reshape(rows, n, 1), (rows, n, c)).reshape(rows, wide)

    ls = jnp.broadcast_to(log_step.reshape(rows, 1), (rows, wide))
    spec = pl.BlockSpec((rows, wide), lambda: (0, 0))
    shp = jax.ShapeDtypeStruct((rows, wide), F32)
    a_re, a_im, bb_re, bb_im = pl.pallas_call(
        _s5_param_kernel,
        in_specs=[spec] * 5, out_specs=[spec] * 4, out_shape=[shp] * 4,
        name="s5_params",
    )(expand(lam_re), expand(lam_im), ls, b_re.reshape(rows, wide), b_im.reshape(rows, wide))

    gpb = g // S5_BLOCKS
    eye = jnp.eye(gpb, dtype=F32)

    def diag_in(bb):
        bb = bb.reshape(nd, S5_BLOCKS, gpb, n, c)
        return jnp.einsum("dkgnc,gh->dkgchn", bb, eye).reshape(nd, S5_BLOCKS, gpb * c, gpb * n).astype(BF16)

    def diag_out(cc):
        cc = cc.reshape(nd, S5_BLOCKS, gpb, c, n)
        return jnp.einsum("dkgcn,gh->dkgnhc", cc, eye).reshape(nd, S5_BLOCKS, gpb * n, gpb * c).astype(BF16)

    def decay(a):
        a = a.reshape(nd, S5_BLOCKS, gpb, n, c)[..., 0].reshape(nd, S5_BLOCKS, 1, gpb * n)
        return jnp.broadcast_to(a, (nd, S5_BLOCKS, 8, gpb * n))

    return decay(a_re), decay(a_im), diag_in(bb_re), diag_in(bb_im), diag_out(c_re), diag_out(c_im)


def _s5_kernel(uf_ref, ub_ref, are_ref, aim_ref, bre_ref, bim_ref, cre_ref, cim_ref,
               yf_ref, yb_ref, bur_scr, bui_scr, st_scr, *, steps, bsz):
    i = pl.program_id(0)

    @pl.when(i == 0)
    def _():
        st_scr[...] = jnp.zeros_like(st_scr)

    for d in range(2):
        u_ref = (uf_ref, ub_ref)[d]
        y_ref = (yf_ref, yb_ref)[d]
        for k in range(S5_BLOCKS):
            u = u_ref[:, k * S5_BLOCK_IN:(k + 1) * S5_BLOCK_IN].astype(BF16)
            bur_scr[...] = _dot(u, bre_ref[d, k])
            bui_scr[...] = _dot(u, bim_ref[d, k])
            ar = are_ref[d, k]
            ai = aim_ref[d, k]

            def step(t, carry, d=d, ar=ar, ai=ai):
                sr, si = carry
                tt = t if d == 0 else steps - 1 - t
                r0 = pl.multiple_of(tt * bsz, bsz)
                nr = ar * sr - ai * si + bur_scr[pl.ds(r0, bsz), :]
                ni = ar * si + ai * sr + bui_scr[pl.ds(r0, bsz), :]
                bur_scr[pl.ds(r0, bsz), :] = nr
                bui_scr[pl.ds(r0, bsz), :] = ni
                return nr, ni

            sr, si = lax.fori_loop(0, steps, step, (st_scr[d, k, 0], st_scr[d, k, 1]), unroll=8)
            st_scr[d, k, 0] = sr
            st_scr[d, k, 1] = si
            y = _dot(bur_scr[...].astype(BF16), cre_ref[d, k]) - _dot(bui_scr[...].astype(BF16), cim_ref[d, k])
            y_ref[:, k * S5_BLOCK_IN:(k + 1) * S5_BLOCK_IN] = y


def _s5_scan(u_tm, params, bsz, s_tot, n_ctx):
    a_re, a_im, bb_re, bb_im, cc_re, cc_im = params
    assert bsz == 8
    rows = S5_CHUNK * bsz
    n_chunks = s_tot // S5_CHUNK
    ctx_chunks = n_ctx // S5_CHUNK

    def bwd_chunk(i):
        return jnp.where(i < ctx_chunks, ctx_chunks - 1 - i, n_chunks - 1 + ctx_chunks - i)

    full = lambda a: pl.BlockSpec(a.shape, lambda i: (0,) * a.ndim)
    shp = jax.ShapeDtypeStruct((s_tot * bsz, SSM_WIDTH), F32)
    return pl.pallas_call(
        functools.partial(_s5_kernel, steps=S5_CHUNK, bsz=bsz),
        grid=(n_chunks,),
        in_specs=[pl.BlockSpec((rows, SSM_WIDTH), lambda i: (i, 0)),
                  pl.BlockSpec((rows, SSM_WIDTH), lambda i: (bwd_chunk(i), 0)),
                  full(a_re), full(a_im), full(bb_re), full(bb_im), full(cc_re), full(cc_im)],
        out_specs=[pl.BlockSpec((rows, SSM_WIDTH), lambda i: (i, 0)),
                   pl.BlockSpec((rows, SSM_WIDTH), lambda i: (bwd_chunk(i), 0))],
        out_shape=[shp, shp],
        scratch_shapes=[pltpu.VMEM((rows, S5_BLOCK_STATE), F32), pltpu.VMEM((rows, S5_BLOCK_STATE), F32),
                        pltpu.VMEM((2, S5_BLOCKS, 2, bsz, S5_BLOCK_STATE), F32)],
        compiler_params=_cparams(("arbitrary",)),
        name="s5_scan",
    )(u_tm, u_tm, a_re, a_im, bb_re, bb_im, cc_re, cc_im)


def _s5_post_kernel(u_ref, yf_ref, yb_ref, d_ref, w_ref, o_ref):
    y = d_ref[...] * u_ref[...] + yf_ref[...] + yb_ref[...]
    g = _gelu(y).astype(BF16)
    gate = _sigmoid(_dot(g, w_ref[...]))
    o_ref[...] = (g.astype(F32) * gate).astype(BF16)


def _s5_post(u_tm, yf, yb, d_skip, w_glu_bf16):
    n, w = u_tm.shape
    tm = 1024
    row = pl.BlockSpec((tm, w), lambda i: (i, 0))
    return pl.pallas_call(
        _s5_post_kernel,
        grid=(n // tm,),
        in_specs=[row, row, row, pl.BlockSpec((1, w), lambda i: (0, 0)), pl.BlockSpec((w, w), lambda i: (0, 0))],
        out_specs=row,
        out_shape=jax.ShapeDtypeStruct((n, w), BF16),
        compiler_params=_cparams(("arbitrary",)),
        name="s5_post",
    )(u_tm, yf, yb, d_skip.reshape(1, w), w_glu_bf16)


def _rope_tables(lat_len):
    rows = lat_len // GRID_W
    row = jnp.repeat(jnp.arange(rows, dtype=F32), GRID_W)
    col = jnp.tile(jnp.arange(GRID_W, dtype=F32), rows)
    inv = ROPE_BASE ** (-jnp.arange(ROPE_FREQS, dtype=F32) / ROPE_FREQS)
    ang_r = row[:, None] * inv
    ang_c = col[:, None] * inv
    cos = jnp.concatenate([jnp.cos(ang_r), jnp.cos(ang_r), jnp.cos(ang_c), jnp.cos(ang_c)], axis=1)
    sin = jnp.concatenate([-jnp.sin(ang_r), jnp.sin(ang_r), -jnp.sin(ang_c), jnp.sin(ang_c)], axis=1)
    return jnp.tile(cos, (1, 2)), jnp.tile(sin, (1, 2))


def _head_rms(x, ones_bd, g):
    hi, lo = _split2(x * x)
    ms = (_dot(hi, ones_bd) + _dot(lo, ones_bd)) * (1.0 / ATT_HEAD_DIM)
    return x * lax.rsqrt(ms + EPS) * g


def _rope(x, cos, sin_signed):
    lane = lax.broadcasted_iota(jnp.int32, x.shape, 1)
    first_half = (lane % (2 * ROPE_FREQS)) < ROPE_FREQS
    partner = jnp.where(first_half, pltpu.roll(x, LANES - ROPE_FREQS, 1), pltpu.roll(x, ROPE_FREQS, 1))
    return x * cos + partner * sin_signed


def _qkprep_kernel(q_ref, kv_ref, cos_ref, sin_ref, qg_ref, kg_ref, qo_ref, ko_ref, vo_ref, *, n_ctx, chunk):
    s_tot = q_ref.shape[0]
    r_i = lax.broadcasted_iota(jnp.int32, (LANES, LANES), 0) // ATT_HEAD_DIM
    c_i = lax.broadcasted_iota(jnp.int32, (LANES, LANES), 1) // ATT_HEAD_DIM
    ones_bd = jnp.where(r_i == c_i, 1.0, 0.0).astype(BF16)
    lane = lax.broadcasted_iota(jnp.int32, (chunk, LANES), 1)
    low = lane < ATT_HEAD_DIM
    for r0 in range(0, s_tot, chunk):
        roped = r0 >= n_ctx
        if roped:
            cos = cos_ref[r0 - n_ctx:r0 - n_ctx + chunk, :]
            sin = sin_ref[r0 - n_ctx:r0 - n_ctx + chunk, :]
        for s in range(ATT_WIDTH // LANES):
            x = _head_rms(q_ref[r0:r0 + chunk, s * LANES:(s + 1) * LANES], ones_bd, qg_ref[...])
            if roped:
                x = _rope(x, cos, sin)
            qo_ref[r0:r0 + chunk, s * LANES:(s + 1) * LANES] = (x * (ATT_HEAD_DIM ** -0.5)).astype(BF16)
        k = _head_rms(kv_ref[r0:r0 + chunk, 0:LANES], ones_bd, kg_ref[...])
        if roped:
            k = _rope(k, cos, sin)
        v = kv_ref[r0:r0 + chunk, LANES:2 * LANES]
        k_sw = pltpu.roll(k, ATT_HEAD_DIM, 1)
        v_sw = pltpu.roll(v, ATT_HEAD_DIM, 1)
        zero = jnp.zeros_like(k)
        ks = (jnp.where(low, k, zero), jnp.where(low, zero, k_sw), jnp.where(low, k_sw, zero), jnp.where(low, zero, k))
        vs = (jnp.where(low, v, zero), jnp.where(low, zero, v_sw), jnp.where(low, v_sw, zero), jnp.where(low, zero, v))
        for n in range(4):
            ko_ref[0, n, r0:r0 + chunk, :] = ks[n].astype(BF16)
            vo_ref[0, n, r0:r0 + chunk, :] = vs[n].astype(BF16)


def _qkprep(proj, cos, sin, q_g, k_g, bsz, s_tot, n_ctx):
    qg = jnp.tile(q_g, 2).reshape(1, LANES)
    kg = jnp.tile(k_g, 2).reshape(1, LANES)
    kv_shape = jax.ShapeDtypeStruct((bsz, 4, s_tot, LANES), BF16)
    kv_spec = pl.BlockSpec((1, 4, s_tot, LANES), lambda b: (b, 0, 0, 0))
    return pl.pallas_call(
        functools.partial(_qkprep_kernel, n_ctx=n_ctx, chunk=256),
        grid=(bsz,),
        in_specs=[pl.BlockSpec((s_tot, PROJ_BLOCK), lambda b: (b, COL_QA // PROJ_BLOCK)),
                  pl.BlockSpec((s_tot, PROJ_BLOCK), lambda b: (b, COL_KA // PROJ_BLOCK)),
                  pl.BlockSpec(cos.shape, lambda b: (0, 0)), pl.BlockSpec(sin.shape, lambda b: (0, 0)),
                  pl.BlockSpec((1, LANES), lambda b: (0, 0)), pl.BlockSpec((1, LANES), lambda b: (0, 0))],
        out_specs=[pl.BlockSpec((s_tot, ATT_WIDTH), lambda b: (b, 0)), kv_spec, kv_spec],
        out_shape=[jax.ShapeDtypeStruct((bsz * s_tot, ATT_WIDTH), BF16), kv_shape, kv_shape],
        compiler_params=_cparams(("arbitrary",)),
        name="qk_prep",
    )(proj, proj, cos, sin, qg, kg)


def _attend(q_ref, k_ref, v_ref, o_ref, n_keys):
    tq = q_ref.shape[0]
    for hk in range(ATT_KV_HEADS):
        qs = jnp.concatenate([q_ref[:, (2 * hk) * LANES:(2 * hk + 1) * LANES],
                              q_ref[:, (2 * hk + 1) * LANES:(2 * hk + 2) * LANES]], axis=0)
        acc = jnp.zeros((2 * tq, LANES), F32)
        for p in range(2):
            s = _dot_nt(qs, k_ref[0, 2 * hk + p, 0:n_keys, :])
            m = jnp.max(s, axis=-1, keepdims=True)
            e = jnp.exp(s - m)
            l = jnp.sum(e, axis=-1, keepdims=True)
            acc = acc + _dot(e.astype(BF16), v_ref[0, 2 * hk + p, 0:n_keys, :]) / l
        o_ref[:, (2 * hk) * LANES:(2 * hk + 1) * LANES] = acc[0:tq].astype(BF16)
        o_ref[:, (2 * hk + 1) * LANES:(2 * hk + 2) * LANES] = acc[tq:2 * tq].astype(BF16)


def _attn_kernel(q_ref, k_ref, v_ref, o_ref, *, n_ctx, first_block):
    qi = pl.program_id(1) + first_block
    s_tot = k_ref.shape[2]
    if first_block == 0:
        @pl.when(qi == 0)
        def _():
            _attend(q_ref, k_ref, v_ref, o_ref, n_ctx)

        @pl.when(qi > 0)
        def _():
            _attend(q_ref, k_ref, v_ref, o_ref, s_tot)
    else:
        _attend(q_ref, k_ref, v_ref, o_ref, s_tot)


def _attention(qn, kn, vn, bsz, s_tot, n_ctx, latent_only):
    tq = n_ctx
    blocks = s_tot // tq
    first = 1 if latent_only else 0
    kv_spec = pl.BlockSpec((1, 4, s_tot, LANES), lambda b, i: (b, 0, 0, 0))
    return pl.pallas_call(
        functools.partial(_attn_kernel, n_ctx=n_ctx, first_block=first),
        grid=(bsz, blocks - first),
        in_specs=[pl.BlockSpec((tq, ATT_WIDTH), lambda b, i: (b * blocks + i + first, 0)), kv_spec, kv_spec],
        out_specs=pl.BlockSpec((tq, ATT_WIDTH), lambda b, i: (b * blocks + i + first, 0)),
        out_shape=jax.ShapeDtypeStruct((bsz * s_tot, ATT_WIDTH), BF16),
        compiler_params=_cparams(("arbitrary", "arbitrary")),
        name="attention",
    )(qn, kn, vn)


def _log_sigmoid(x):
    return jnp.minimum(x, 0.0) - jnp.log(1.0 + jnp.exp(-jnp.abs(x)))


def _mlstm_kernel(qf_ref, kf_ref, vf_ref, gf_ref, qb_ref, kb_ref, vb_ref, gb_ref, bias_ref,
                  hf_ref, hb_ref, c_scr, n_scr, m_scr):
    i = pl.program_id(1)
    t = ML_CHUNK

    @pl.when(i == 0)
    def _():
        c_scr[...] = jnp.zeros_like(c_scr)
        n_scr[...] = jnp.zeros_like(n_scr)
        m_scr[...] = jnp.zeros_like(m_scr)

    r_i = lax.broadcasted_iota(jnp.int32, (t, t), 0)
    c_i = lax.broadcasted_iota(jnp.int32, (t, t), 1)
    lower = r_i >= c_i
    upper = r_i <= c_i
    lower_m = jnp.where(lower, 1.0, 0.0).astype(BF16)
    upper_m = jnp.where(upper, 1.0, 0.0).astype(BF16)

    for d in range(2):
        q_ref, k_ref, v_ref, g_ref, h_ref = ((qf_ref, kf_ref, vf_ref, gf_ref, hf_ref),
                                             (qb_ref, kb_ref, vb_ref, gb_ref, hb_ref))[d]
        g = g_ref[...] + bias_ref[...]
        g_t = g.T
        lf = _log_sigmoid(g)
        lf_t = _log_sigmoid(g_t)
        causal, causal_m, anti_m = (lower, lower_m, upper_m) if d == 0 else (upper, upper_m, lower_m)
        b_cols = _dot_exact_lhs(causal_m, lf)
        b_rows = _dot_exact_rhs(lf_t, anti_m)
        last = t - 1 if d == 0 else 0
        for h in range(ML_HEADS):
            ci = d * 2 * ML_HEADS + h
            cf = ci + ML_HEADS
            i_col = g[:, ci:ci + 1]
            i_row = g_t[ci:ci + 1, :]
            b_col = b_cols[:, cf:cf + 1]
            b_row = b_rows[cf:cf + 1, :]
            b_last = b_cols[last:last + 1, cf:cf + 1]
            m_prev = m_scr[d, h]
            c_prev = c_scr[d, h]
            n_prev = n_scr[d, h]

            q = q_ref[:, h * ML_HEAD_DIM:(h + 1) * ML_HEAD_DIM]
            k = k_ref[:, h * ML_HEAD_DIM:(h + 1) * ML_HEAD_DIM] * (ML_HEAD_DIM ** -0.5)
            v = v_ref[:, h * ML_HEAD_DIM:(h + 1) * ML_HEAD_DIM]
            q16, k16, v16 = q.astype(BF16), k.astype(BF16), v.astype(BF16)

            dmat = jnp.where(causal, b_col - b_row + i_row, NEG_INF)
            inter = b_col + m_prev
            m_t = jnp.maximum(inter, jnp.max(dmat, axis=-1, keepdims=True))
            w = jnp.exp(dmat - m_t)
            a_inter = jnp.exp(inter - m_t)
            qk = _dot_nt(q16, k16) * w
            num = a_inter * _dot(q16, c_prev.astype(BF16)) + _dot(qk.astype(BF16), v16)
            den = a_inter * jnp.sum(q * n_prev, axis=-1, keepdims=True) + jnp.sum(qk, axis=-1, keepdims=True)
            h_ref[:, h * ML_HEAD_DIM:(h + 1) * ML_HEAD_DIM] = num / jnp.maximum(jnp.abs(den), jnp.exp(-m_t))

            d_last_col = b_last - b_col + i_col
            m_new = jnp.maximum(b_last + m_prev, jnp.max(d_last_col, axis=0, keepdims=True))
            w_last = jnp.exp(d_last_col - m_new)
            decay = jnp.exp(b_last + m_prev - m_new)
            kw = k * w_last
            c_scr[d, h] = decay * c_prev + _dot(kw.T.astype(BF16), v16)
            n_scr[d, h] = decay * n_prev + jnp.sum(kw, axis=0, keepdims=True)
            m_scr[d, h] = m_new


def _mlstm(proj, gate_b, bsz, s_tot, n_ctx):
    t = ML_CHUNK
    n_chunks = s_tot // t
    ctx_chunks = n_ctx // t

    def fwd(b, i):
        return b * n_chunks + i

    def bwd(b, i):
        return b * n_chunks + jnp.where(i < ctx_chunks, ctx_chunks - 1 - i, n_chunks - 1 + ctx_chunks - i)

    def col(c0, width):
        return c0 // width

    def specs(rowfn):
        return [pl.BlockSpec((t, ML_WIDTH), lambda b, i: (rowfn(b, i), col(COL_QM, ML_WIDTH))),
                pl.BlockSpec((t, ML_WIDTH), lambda b, i: (rowfn(b, i), col(COL_KM, ML_WIDTH))),
                pl.BlockSpec((t, ML_WIDTH), lambda b, i: (rowfn(b, i), col(COL_VM, ML_WIDTH))),
                pl.BlockSpec((t, LANES), lambda b, i: (rowfn(b, i), col(COL_GM, LANES)))]

    bias = jnp.zeros((1, LANES), F32).at[0, :ML_GATES].set(gate_b.reshape(ML_GATES))
    shp = jax.ShapeDtypeStruct((bsz * s_tot, ML_WIDTH), F32)
    return pl.pallas_call(
        _mlstm_kernel,
        grid=(bsz, n_chunks),
        in_specs=specs(fwd) + specs(bwd) + [pl.BlockSpec((1, LANES), lambda b, i: (0, 0))],
        out_specs=[pl.BlockSpec((t, ML_WIDTH), lambda b, i: (fwd(b, i), 0)),
                   pl.BlockSpec((t, ML_WIDTH), lambda b, i: (bwd(b, i), 0))],
        out_shape=[shp, shp],
        scratch_shapes=[pltpu.VMEM((2, ML_HEADS, ML_HEAD_DIM, ML_HEAD_DIM), F32),
                        pltpu.VMEM((2, ML_HEADS, 1, ML_HEAD_DIM), F32),
                        pltpu.VMEM((2, ML_HEADS, 1, 1), F32)],
        compiler_params=_cparams(("arbitrary", "arbitrary")),
        name="mlstm",
    )(*([proj] * 8), bias)


def _merge_kernel(ys_ref, ya_ref, hf_ref, hb_ref, om_ref, gl_ref, x_ref, mod_ref, mg_ref, n2_ref,
                  wbs_ref, wba_ref, wbm_ref, wo_ref, xo_ref, h2_ref, *, blocks_per_batch, first_block):
    n = pl.program_id(0)
    b = n // (blocks_per_batch - first_block)
    if not first_block:
        b = jnp.where(n % blocks_per_batch == 0, CTX_MOD_ROW, b)
    is_ctx = None

    hs = hf_ref[...] + hb_ref[...]
    parts = []
    for h in range(ML_HEADS):
        sl = slice(h * ML_HEAD_DIM, (h + 1) * ML_HEAD_DIM)
        parts.append(_rms(hs[:, sl], mg_ref[:, sl]))
    hn = jnp.concatenate(parts, axis=1) * _sigmoid(om_ref[...])

    gl = gl_ref[...]
    merged = (_sigmoid(gl[:, 0:D_MODEL]) * _dot(ys_ref[...], wbs_ref[...])
              + _sigmoid(gl[:, D_MODEL:2 * D_MODEL]) * _dot(ya_ref[...], wba_ref[...])
              + _sigmoid(gl[:, 2 * D_MODEL:3 * D_MODEL]) * _dot(hn.astype(BF16), wbm_ref[...]))
    mix = _dot(merged.astype(BF16), wo_ref[...])
    g1 = _mod_rows(mod_ref, 2, b, is_ctx)
    x = x_ref[...] + g1 * mix
    xo_ref[...] = x
    sh2 = _mod_rows(mod_ref, 3, b, is_ctx)
    sc2 = _mod_rows(mod_ref, 4, b, is_ctx)
    h2_ref[...] = (_rms(x, n2_ref[...]) * (1.0 + sc2) + sh2).astype(BF16)


def _merge(ys, ya, hf, hb, proj, stream, mod_l, ml_norm_g, norm2_g, wbs, wba, wbm, wo,
           bsz, s_tot, n_ctx, latent_only):
    tm = n_ctx
    bpb = s_tot // tm
    first = 1 if latent_only else 0
    per = bpb - first

    def rows(n):
        return (n // per) * bpb + n % per + first

    n_blocks = bsz * per
    full = lambda a: pl.BlockSpec(a.shape, lambda n: (0,) * a.ndim)
    wide = lambda w: pl.BlockSpec((tm, w), lambda n: (rows(n), 0))
    mg = ml_norm_g.reshape(1, ML_WIDTH)
    n2 = norm2_g.reshape(1, D_MODEL)
    return pl.pallas_call(
        functools.partial(_merge_kernel, blocks_per_batch=bpb, first_block=first),
        grid=(n_blocks,),
        in_specs=[wide(SSM_WIDTH), wide(ATT_WIDTH), wide(ML_WIDTH), wide(ML_WIDTH),
                  pl.BlockSpec((tm, ML_WIDTH), lambda n: (rows(n), COL_OM // ML_WIDTH)),
                  pl.BlockSpec((tm, N_BRANCH * D_MODEL), lambda n: (rows(n), COL_GATE // (N_BRANCH * D_MODEL))),
                  wide(D_MODEL), full(mod_l), full(mg), full(n2), full(wbs), full(wba), full(wbm), full(wo)],
        out_specs=[pl.BlockSpec((tm, D_MODEL), lambda n: (n, 0)), pl.BlockSpec((tm, D_MODEL), lambda n: (n, 0))],
        out_shape=[jax.ShapeDtypeStruct((n_blocks * tm, D_MODEL), F32),
                   jax.ShapeDtypeStruct((n_blocks * tm, D_MODEL), BF16)],
        compiler_params=_cparams(("arbitrary",)),
        name="merge",
    )(ys, ya, hf, hb, proj, proj, stream, mod_l, mg, n2, wbs, wba, wbm, wo)


def _peer_score_kernel(x_ref, wq_ref, k1_ref, k2_ref, s1_ref, s2_ref):
    half = PEER_QDIM // 2
    q_t = _dot_nt(wq_ref[...], x_ref[...])
    for key_ref, s_ref, lo in ((k1_ref, s1_ref, 0), (k2_ref, s2_ref, half)):
        khi, klo = _split2(key_ref[...])
        qhi, qlo = _split2(q_t[lo:lo + half, :])
        s_ref[0] = _dot(khi, qhi) + _dot(khi, qlo) + _dot(klo, qhi)


def _peer_scores(h2, wq_t, k1, k2):
    n = h2.shape[0]
    tm = 1024
    shp = jax.ShapeDtypeStruct((PEER_HEADS, PEER_KEYS, n), F32)
    out = pl.BlockSpec((1, PEER_KEYS, tm), lambda i, h: (h, 0, i))
    key = pl.BlockSpec((PEER_KEYS, PEER_QDIM // 2), lambda i, h: (0, 0))
    return pl.pallas_call(
        _peer_score_kernel,
        grid=(n // tm, PEER_HEADS),
        in_specs=[pl.BlockSpec((tm, D_MODEL), lambda i, h: (i, 0)),
                  pl.BlockSpec((PEER_QDIM, D_MODEL), lambda i, h: (h, 0)), key, key],
        out_specs=[out, out],
        out_shape=[shp, shp],
        compiler_params=_cparams(("arbitrary", "arbitrary")),
        name="peer_scores",
    )(h2, wq_t, k1, k2)


def _top_values(s, k):
    work = s
    rank = jnp.full(s.shape, float(PEER_KEYS), F32)
    vals = []
    for r in range(k):
        m = jnp.max(work, axis=0, keepdims=True)
        hit = work == m
        rank = jnp.where(hit, float(r), rank)
        work = jnp.where(hit, NEG_INF, work)
        vals.append(m)
    return vals, rank


def _peer_select_kernel(s1_ref, s2_ref, r2_ref, g2_ref, cnt_ref, e1_ref, v1_scr, v2_scr):
    k = PEER_TOPK
    half = k // 2
    s1 = s1_ref[0]
    s2 = s2_ref[0]
    v1, rank1 = _top_values(s1, k)
    v2, rank2 = _top_values(s2, k)
    for a in range(k):
        v1_scr[a:a + 1, :] = v1[a]
        v2_scr[a:a + 1, :] = v2[a]
    row = lax.broadcasted_iota(jnp.int32, (half, s1.shape[1]), 0)
    pieces = [v1[0] + v2_scr[...]]
    for a in range(1, half):
        pieces.append(jnp.where(row < k // (a + 1), v1[a] + v2_scr[0:half, :], NEG_INF))
    pieces.append(v1_scr[half:k, :] + v2[0])
    top = v1[0] + v2[0]
    z = jnp.zeros_like(top)
    thr = top
    for _ in range(k):
        thr = functools.reduce(jnp.maximum, [jnp.max(p, axis=0, keepdims=True) for p in pieces])
        pieces = [jnp.where(p == thr, NEG_INF, p) for p in pieces]
        z = z + jnp.exp(thr - top)
    cnt = jnp.zeros_like(s1)
    for a in range(k):
        if a == 0:
            sums = v1[0] + v2_scr[...]
        elif a < half:
            sums = jnp.where(row < k // (a + 1), v1[a] + v2_scr[0:half, :], NEG_INF)
        else:
            sums = v1[a] + v2[0]
        n_sel = jnp.sum(jnp.where(sums >= thr, 1.0, 0.0), axis=0, keepdims=True)
        cnt = jnp.where(rank1 == float(a), n_sel, cnt)
    r2_ref[0] = rank2
    g2_ref[0] = jnp.exp(s2 - v2[0])
    cnt_ref[0] = cnt
    e1_ref[0] = jnp.exp(s1 - v1[0]) / z


def _peer_select(s1, s2):
    heads, keys, n = s1.shape
    tl = 512
    spec = pl.BlockSpec((1, keys, tl), lambda h, i: (h, 0, i))
    shp = jax.ShapeDtypeStruct((heads, keys, n), F32)
    return pl.pallas_call(
        _peer_select_kernel,
        grid=(heads, n // tl),
        in_specs=[spec, spec], out_specs=[spec] * 4, out_shape=[shp] * 4,
        scratch_shapes=[pltpu.VMEM((PEER_TOPK, tl), F32), pltpu.VMEM((PEER_TOPK, tl), F32)],
        compiler_params=_cparams(("arbitrary", "arbitrary")),
        name="peer_select",
    )(s1, s2)


def _peer_dense_kernel(x_ref, u_ref, vt_ref, r2_ref, g2_ref, cnt_ref, e1_ref, s_ref, mod_ref, fg_ref,
                       o_ref, acc_scr, *, te, blocks_per_batch, n_ctx, final):
    i = pl.program_id(0)
    j = pl.program_id(1)
    tm = x_ref.shape[0]

    @pl.when(j == 0)
    def _():
        acc_scr[...] = jnp.zeros_like(acc_scr)

    act = _gelu(_dot_nt(u_ref[...], x_ref[...]))
    sub = te // PEER_KEYS
    coef = []
    for s in range(sub):
        i1 = j * sub + s
        w = jnp.zeros((PEER_KEYS, tm), F32)
        for h in range(PEER_HEADS):
            c = cnt_ref[h, pl.ds(i1, 1), :]
            e = e1_ref[h, pl.ds(i1, 1), :]
            w = w + jnp.where(r2_ref[h] < c, g2_ref[h], 0.0) * e
        coef.append((w * act[s * PEER_KEYS:(s + 1) * PEER_KEYS, :]).astype(BF16))
    acc_scr[...] += _dot(vt_ref[...], jnp.concatenate(coef, axis=0))

    @pl.when(j == pl.num_programs(1) - 1)
    def _():
        b = i // blocks_per_batch
        if n_ctx:
            row = (i % blocks_per_batch) * tm + lax.broadcasted_iota(jnp.int32, (tm, 1), 0)
            is_ctx = row < n_ctx
        else:
            is_ctx = None
        g2 = _mod_rows(mod_ref, 5, b, is_ctx)
        x = s_ref[...] + g2 * acc_scr[...].T
        if final:
            x = _rms(x, fg_ref[...])
        o_ref[...] = x


def _peer_dense(h2, u_bf16, vt_bf16, sel, stream, mod_l, final_g, rows_per_batch, n_ctx, final):
    n = h2.shape[0]
    r2, g2, cnt, e1 = sel
    tm = 768 if rows_per_batch % 768 == 0 else 512
    te = 512
    bpb = rows_per_batch // tm
    tok = pl.BlockSpec((PEER_HEADS, PEER_KEYS, tm), lambda i, j: (0, 0, i))
    full = lambda a: pl.BlockSpec(a.shape, lambda i, j: (0,) * a.ndim)
    fg = final_g.reshape(1, D_MODEL)
    return pl.pallas_call(
        functools.partial(_peer_dense_kernel, te=te, blocks_per_batch=bpb, n_ctx=n_ctx, final=final),
        grid=(n // tm, PEER_EXPERTS // te),
        in_specs=[pl.BlockSpec((tm, D_MODEL), lambda i, j: (i, 0)),
                  pl.BlockSpec((te, D_MODEL), lambda i, j: (j, 0)),
                  pl.BlockSpec((D_MODEL, te), lambda i, j: (0, j)),
                  tok, tok, tok, tok,
                  pl.BlockSpec((tm, D_MODEL), lambda i, j: (i, 0)), full(mod_l), full(fg)],
        out_specs=pl.BlockSpec((tm, D_MODEL), lambda i, j: (i, 0)),
        out_shape=jax.ShapeDtypeStruct((n, D_MODEL), F32),
        scratch_shapes=[pltpu.VMEM((D_MODEL, tm), F32)],
        compiler_params=_cparams(("arbitrary", "arbitrary")),
        name="peer_dense",
    )(h2, u_bf16, vt_bf16, r2, g2, cnt, e1, stream, mod_l, fg)


def _reorder_w_in(w):
    offs = [0]
    for width in (SSM_WIDTH, ATT_WIDTH, KV_WIDTH, KV_WIDTH, ML_WIDTH, ML_WIDTH, ML_WIDTH, ML_WIDTH, ML_GATES,
                  N_BRANCH * D_MODEL):
        offs.append(offs[-1] + width)
    u_s, q_a, k_a, v_a, q_m, k_m, v_m, o_m, g_m, gate = (w[:, offs[n]:offs[n + 1]] for n in range(10))
    pad = jnp.zeros((w.shape[0], PROJ_WIDTH - COL_GM - ML_GATES), w.dtype)
    return jnp.concatenate([u_s, q_a, q_m, k_m, v_m, o_m, gate, k_a, v_a, g_m, pad], axis=1).astype(BF16)


def kernel(x, c, ctx, c_ctx, w_mod, b_mod, norm1_g, norm2_g, w_in, ssm_lam_re, ssm_lam_im, ssm_log_step, ssm_b_re, ssm_b_im, ssm_c_re, ssm_c_im, ssm_d, ssm_w_glu, attn_q_norm_g, attn_k_norm_g, mlstm_gate_b, mlstm_norm_g, w_branch_ssm, w_branch_attn, w_branch_mlstm, w_out, peer_w_q, peer_sub_k1, peer_sub_k2, peer_u, peer_v, final_norm_g):
    bsz, lat_len, d = x.shape
    n_ctx = ctx.shape[1]
    s_tot = n_ctx + lat_len
    depth = w_in.shape[0]
    assert d == D_MODEL and n_ctx == 256 and lat_len % n_ctx == 0 and bsz == 8

    cos, sin = _rope_tables(lat_len)
    mod = _modulation(c, c_ctx, w_mod, b_mod)
    stream = jnp.concatenate([ctx, x], axis=1)

    for layer in range(depth):
        last = layer == depth - 1
        mod_l = mod[layer]
        proj = _inproj(stream.reshape(bsz, s_tot, d), norm1_g[layer], mod_l, _reorder_w_in(w_in[layer]), n_ctx)

        u_tm = proj[:, COL_U:COL_U + SSM_WIDTH].reshape(bsz, s_tot, SSM_WIDTH).transpose(1, 0, 2)
        u_tm = u_tm.reshape(s_tot * bsz, SSM_WIDTH)
        params = _s5_params(ssm_lam_re[layer], ssm_lam_im[layer], ssm_log_step[layer], ssm_b_re[layer],
                            ssm_b_im[layer], ssm_c_re[layer], ssm_c_im[layer])
        yf, yb = _s5_scan(u_tm, params, bsz, s_tot, n_ctx)
        ys_tm = _s5_post(u_tm, yf, yb, ssm_d[layer], ssm_w_glu[layer].astype(BF16))
        ys = ys_tm.reshape(s_tot, bsz, SSM_WIDTH).transpose(1, 0, 2).reshape(bsz * s_tot, SSM_WIDTH)

        qn, kn, vn = _qkprep(proj, cos, sin, attn_q_norm_g[layer], attn_k_norm_g[layer], bsz, s_tot, n_ctx)
        ya = _attention(qn, kn, vn, bsz, s_tot, n_ctx, last)

        hf, hb = _mlstm(proj, mlstm_gate_b[layer], bsz, s_tot, n_ctx)

        stream2, h2 = _merge(ys, ya, hf, hb, proj, stream.reshape(bsz * s_tot, d), mod_l, mlstm_norm_g[layer],
                             norm2_g[layer], w_branch_ssm[layer].astype(BF16), w_branch_attn[layer].astype(BF16),
                             w_branch_mlstm[layer].astype(BF16), w_out[layer].astype(BF16),
                             bsz, s_tot, n_ctx, last)

        s1, s2 = _peer_scores(h2, peer_w_q[layer].T.astype(BF16), peer_sub_k1[layer], peer_sub_k2[layer])
        sel = _peer_select(s1, s2)
        rows_per_batch = lat_len if last else s_tot
        stream = _peer_dense(h2, peer_u[layer].astype(BF16), peer_v[layer].T.astype(BF16), sel, stream2, mod_l,
                             final_norm_g, rows_per_batch, 0 if last else n_ctx, last)

    return stream.reshape(bsz, lat_len, d)
```

```python
import functools
import math

import jax
import jax.numpy as jnp
from jax import lax
from jax.experimental import pallas as pl
from jax.experimental.pallas import tpu as pltpu

F32 = jnp.float32
BF16 = jnp.bfloat16

D_MODEL = 1024
GRID_W = 64
EPS = 1e-6
N_MOD = 6
N_BRANCH = 3
SSM_WIDTH = 512
SSM_GROUP = 16
SSM_GROUPS = 32
SSM_STATE = 64
ATT_HEADS = 8
ATT_KV_HEADS = 2
ATT_HEAD_DIM = 64
ATT_WIDTH = 512
KV_WIDTH = 128
ROPE_FREQS = 16
ROPE_BASE = 10000.0
ML_HEADS = 4
ML_HEAD_DIM = 128
ML_WIDTH = 512
ML_GATES = 16
ML_CHUNK = 128
PEER_HEADS = 8
PEER_KEYS = 128
PEER_EXPERTS = PEER_KEYS * PEER_KEYS
PEER_QDIM = 256
PEER_TOPK = 16
PEER_CHUNK = 256

LANES = 128
F32_SUBLANES = 8
BF16_SUBLANES = 16
VMEM_LIMIT_BYTES = 56 * 1024 * 1024

PROJ_BLOCK = 512
COL_U, COL_QA, COL_QM, COL_KM, COL_VM, COL_OM = (i * PROJ_BLOCK for i in range(6))
COL_GATE = 6 * PROJ_BLOCK
COL_KA = COL_GATE + N_BRANCH * D_MODEL
COL_VA = COL_KA + KV_WIDTH
COL_GM = COL_VA + KV_WIDTH
PROJ_WIDTH = COL_KA + PROJ_BLOCK

S5_BLOCKS = 4
S5_BLOCK_IN = SSM_WIDTH // S5_BLOCKS
S5_BLOCK_STATE = SSM_GROUPS * SSM_STATE // S5_BLOCKS
S5_CHUNK = 64

NEG_INF = float("-inf")


def _cparams(sem, flags=None):
    return pltpu.CompilerParams(dimension_semantics=sem, vmem_limit_bytes=VMEM_LIMIT_BYTES, flags=flags)


def _split2(x):
    hi = x.astype(BF16)
    lo = (x - hi.astype(F32)).astype(BF16)
    return hi, lo


def _split3(x):
    hi = x.astype(BF16)
    r = x - hi.astype(F32)
    mid = r.astype(BF16)
    lo = (r - mid.astype(F32)).astype(BF16)
    return hi, mid, lo


def _dot(a, b):
    return jnp.dot(a, b, preferred_element_type=F32)


def _dot_nt(a, b):
    return lax.dot_general(a, b, (((1,), (1,)), ((), ())), preferred_element_type=F32)


def _dot_exact_rhs(x, m_bf16):
    hi, mid, lo = _split3(x)
    return _dot(hi, m_bf16) + _dot(mid, m_bf16) + _dot(lo, m_bf16)


def _dot_exact_lhs(m_bf16, x):
    hi, mid, lo = _split3(x)
    return _dot(m_bf16, hi) + _dot(m_bf16, mid) + _dot(m_bf16, lo)


def _sigmoid(x):
    return 1.0 / (1.0 + jnp.exp(-x))


def _gelu(x):
    return jax.nn.gelu(x, approximate=True)


def _mod_kernel(v_ref, w_ref, b_ref, o_ref):
    v = v_ref[...]
    sv = v * _sigmoid(v)
    w = w_ref[0]
    hi, mid, lo = _split3(sv)
    whi, wlo = _split2(w)
    acc = _dot(hi, whi) + _dot(mid, whi) + _dot(hi, wlo) + _dot(lo, whi) + _dot(mid, wlo)
    o_ref[0] = acc + b_ref[0]


def _modulation(c, c_ctx, w_mod, b_mod):
    depth = w_mod.shape[0]
    n_out = w_mod.shape[2]
    rows = 16
    v = jnp.zeros((rows, D_MODEL), F32).at[: c.shape[0]].set(c).at[8].set(c_ctx)
    tn = 1536
    return pl.pallas_call(
        _mod_kernel,
        grid=(depth, n_out // tn),
        in_specs=[pl.BlockSpec((rows, D_MODEL), lambda l, j: (0, 0)),
                  pl.BlockSpec((1, D_MODEL, tn), lambda l, j: (l, 0, j)),
                  pl.BlockSpec((1, 1, tn), lambda l, j: (l, 0, j))],
        out_specs=pl.BlockSpec((1, rows, tn), lambda l, j: (l, 0, j)),
        out_shape=jax.ShapeDtypeStruct((depth, rows, n_out), F32),
        compiler_params=_cparams(("arbitrary", "arbitrary")),
        name="modulation",
    )(v, w_mod, b_mod.reshape(depth, 1, n_out))


CTX_MOD_ROW = 8


def _mod_rows(mod_ref, which, b, is_ctx_col):
    lo = which * D_MODEL
    m_l = mod_ref[pl.ds(b, 1), lo:lo + D_MODEL]
    if is_ctx_col is None:
        return m_l
    m_c = mod_ref[CTX_MOD_ROW:CTX_MOD_ROW + 1, lo:lo + D_MODEL]
    return jnp.where(is_ctx_col, m_c, m_l)


def _rms(x, g):
    ms = jnp.mean(x * x, axis=-1, keepdims=True)
    return x * lax.rsqrt(ms + EPS) * g


def _inproj_kernel(x_ref, g_ref, mod_ref, w_ref, o_ref, h_scr, *, n_ctx, chunk):
    b = pl.program_id(0)
    j = pl.program_id(1)
    s_tot = x_ref.shape[1]

    @pl.when(j == 0)
    def _():
        for r0 in range(0, s_tot, chunk):
            x = x_ref[0, r0:r0 + chunk, :]
            xn = _rms(x, g_ref[...])
            row = r0 + lax.broadcasted_iota(jnp.int32, (chunk, 1), 0)
            is_ctx = row < n_ctx
            sh = _mod_rows(mod_ref, 0, b, is_ctx)
            sc = _mod_rows(mod_ref, 1, b, is_ctx)
            h_scr[r0:r0 + chunk, :] = (xn * (1.0 + sc) + sh).astype(BF16)

    o_ref[...] = _dot(h_scr[...], w_ref[...])


def _inproj(stream3, g, mod_l, w_bf16, n_ctx):
    bsz, s_tot, d = stream3.shape
    n_out = w_bf16.shape[1]
    tn = PROJ_BLOCK
    return pl.pallas_call(
        functools.partial(_inproj_kernel, n_ctx=n_ctx, chunk=256),
        grid=(bsz, n_out // tn),
        in_specs=[pl.BlockSpec((1, s_tot, d), lambda b, j: (b, 0, 0)),
                  pl.BlockSpec((1, d), lambda b, j: (0, 0)),
                  pl.BlockSpec(mod_l.shape, lambda b, j: (0, 0)),
                  pl.BlockSpec((d, tn), lambda b, j: (0, j))],
        out_specs=pl.BlockSpec((s_tot, tn), lambda b, j: (b, j)),
        out_shape=jax.ShapeDtypeStruct((bsz * s_tot, n_out), F32),
        scratch_shapes=[pltpu.VMEM((s_tot, d), BF16)],
        compiler_params=_cparams(("arbitrary", "arbitrary")),
        name="inproj",
    )(stream3, g.reshape(1, d), mod_l, w_bf16)


def _s5_param_kernel(lre_ref, lim_ref, ls_ref, bre_ref, bim_ref, are_ref, aim_ref, bbre_ref, bbim_ref):
    lre = lre_ref[...]
    lim = lim_ref[...]
    step = jnp.exp(ls_ref[...])
    mag = jnp.exp(lre * step)
    a_re = mag * jnp.cos(lim * step)
    a_im = mag * jnp.sin(lim * step)
    den = lre * lre + lim * lim
    z_re = ((a_re - 1.0) * lre + a_im * lim) / den
    z_im = (a_im * lre - (a_re - 1.0) * lim) / den
    b_re = bre_ref[...]
    b_im = bim_ref[...]
    are_ref[...] = a_re
    aim_ref[...] = a_im
    bbre_ref[...] = z_re * b_re - z_im * b_im
    bbim_ref[...] = z_re * b_im + z_im * b_re


def _s5_params(lam_re, lam_im, log_step, b_re, b_im, c_re, c_im):
    nd, g, n = lam_re.shape
    c = b_re.shape[-1]
    rows = nd * g
    wide = n * c

    def expand(z):
        return jnp.broadcast_to(z.reshape(rows, n, 1), (rows, n, c)).reshape(rows, wide)

    ls = jnp.broadcast_to(log_step.reshape(rows, 1), (rows, wide))
    spec = pl.BlockSpec((rows, wide), lambda: (0, 0))
    shp = jax.ShapeDtypeStruct((rows, wide), F32)
    a_re, a_im, bb_re, bb_im = pl.pallas_call(
        _s5_param_kernel,
        in_specs=[spec] * 5, out_specs=[spec] * 4, out_shape=[shp] * 4,
        name="s5_params",
    )(expand(lam_re), expand(lam_im), ls, b_re.reshape(rows, wide), b_im.reshape(rows, wide))

    gpb = g // S5_BLOCKS
    eye = jnp.eye(gpb, dtype=F32)

    def diag_in(bb):
        bb = bb.reshape(nd, S5_BLOCKS, gpb, n, c)
        return jnp.einsum("dkgnc,gh->dkgchn", bb, eye).reshape(nd, S5_BLOCKS, gpb * c, gpb * n).astype(BF16)

    def diag_out(cc):
        cc = cc.reshape(nd, S5_BLOCKS, gpb, c, n)
        return jnp.einsum("dkgcn,gh->dkgnhc", cc, eye).reshape(nd, S5_BLOCKS, gpb * n, gpb * c).astype(BF16)

    def decay(a):
        a = a.reshape(nd, S5_BLOCKS, gpb, n, c)[..., 0].reshape(nd, S5_BLOCKS, 1, gpb * n)
        return jnp.broadcast_to(a, (nd, S5_BLOCKS, 8, gpb * n))

    return decay(a_re), decay(a_im), diag_in(bb_re), diag_in(bb_im), diag_out(c_re), diag_out(c_im)


def _s5_kernel(uf_ref, ub_ref, are_ref, aim_ref, bre_ref, bim_ref, cre_ref, cim_ref,
               yf_ref, yb_ref, bur_scr, bui_scr, st_scr, *, steps, bsz):
    i = pl.program_id(0)

    @pl.when(i == 0)
    def _():
        st_scr[...] = jnp.zeros_like(st_scr)

    for d in range(2):
        u_ref = (uf_ref, ub_ref)[d]
        y_ref = (yf_ref, yb_ref)[d]
        for k in range(S5_BLOCKS):
            u = u_ref[:, k * S5_BLOCK_IN:(k + 1) * S5_BLOCK_IN].astype(BF16)
            bur_scr[...] = _dot(u, bre_ref[d, k])
            bui_scr[...] = _dot(u, bim_ref[d, k])
            ar = are_ref[d, k]
            ai = aim_ref[d, k]

            def step(t, carry, d=d, ar=ar, ai=ai):
                sr, si = carry
                tt = t if d == 0 else steps - 1 - t
                r0 = pl.multiple_of(tt * bsz, bsz)
                nr = ar * sr - ai * si + bur_scr[pl.ds(r0, bsz), :]
                ni = ar * si + ai * sr + bui_scr[pl.ds(r0, bsz), :]
                bur_scr[pl.ds(r0, bsz), :] = nr
                bui_scr[pl.ds(r0, bsz), :] = ni
                return nr, ni

            sr, si = lax.fori_loop(0, steps, step, (st_scr[d, k, 0], st_scr[d, k, 1]), unroll=8)
            st_scr[d, k, 0] = sr
            st_scr[d, k, 1] = si
            y = _dot(bur_scr[...].astype(BF16), cre_ref[d, k]) - _dot(bui_scr[...].astype(BF16), cim_ref[d, k])
            y_ref[:, k * S5_BLOCK_IN:(k + 1) * S5_BLOCK_IN] = y


def _s5_scan(u_tm, params, bsz, s_tot, n_ctx):
    a_re, a_im, bb_re, bb_im, cc_re, cc_im = params
    assert bsz == 8
    rows = S5_CHUNK * bsz
    n_chunks = s_tot // S5_CHUNK
    ctx_chunks = n_ctx // S5_CHUNK

    def bwd_chunk(i):
        return jnp.where(i < ctx_chunks, ctx_chunks - 1 - i, n_chunks - 1 + ctx_chunks - i)

    full = lambda a: pl.BlockSpec(a.shape, lambda i: (0,) * a.ndim)
    shp = jax.ShapeDtypeStruct((s_tot * bsz, SSM_WIDTH), F32)
    return pl.pallas_call(
        functools.partial(_s5_kernel, steps=S5_CHUNK, bsz=bsz),
        grid=(n_chunks,),
        in_specs=[pl.BlockSpec((rows, SSM_WIDTH), lambda i: (i, 0)),
                  pl.BlockSpec((rows, SSM_WIDTH), lambda i: (bwd_chunk(i), 0)),
                  full(a_re), full(a_im), full(bb_re), full(bb_im), full(cc_re), full(cc_im)],
        out_specs=[pl.BlockSpec((rows, SSM_WIDTH), lambda i: (i, 0)),
                   pl.BlockSpec((rows, SSM_WIDTH), lambda i: (bwd_chunk(i), 0))],
        out_shape=[shp, shp],
        scratch_shapes=[pltpu.VMEM((rows, S5_BLOCK_STATE), F32), pltpu.VMEM((rows, S5_BLOCK_STATE), F32),
                        pltpu.VMEM((2, S5_BLOCKS, 2, bsz, S5_BLOCK_STATE), F32)],
        compiler_params=_cparams(("arbitrary",)),
        name="s5_scan",
    )(u_tm, u_tm, a_re, a_im, bb_re, bb_im, cc_re, cc_im)


def _s5_post_kernel(u_ref, yf_ref, yb_ref, d_ref, w_ref, o_ref):
    y = d_ref[...] * u_ref[...] + yf_ref[...] + yb_ref[...]
    g = _gelu(y).astype(BF16)
    gate = _sigmoid(_dot(g, w_ref[...]))
    o_ref[...] = (g.astype(F32) * gate).astype(BF16)


def _s5_post(u_tm, yf, yb, d_skip, w_glu_bf16):
    n, w = u_tm.shape
    tm = 1024
    row = pl.BlockSpec((tm, w), lambda i: (i, 0))
    return pl.pallas_call(
        _s5_post_kernel,
        grid=(n // tm,),
        in_specs=[row, row, row, pl.BlockSpec((1, w), lambda i: (0, 0)), pl.BlockSpec((w, w), lambda i: (0, 0))],
        out_specs=row,
        out_shape=jax.ShapeDtypeStruct((n, w), BF16),
        compiler_params=_cparams(("arbitrary",)),
        name="s5_post",
    )(u_tm, yf, yb, d_skip.reshape(1, w), w_glu_bf16)


def _rope_tables(lat_len):
    rows = lat_len // GRID_W
    row = jnp.repeat(jnp.arange(rows, dtype=F32), GRID_W)
    col = jnp.tile(jnp.arange(GRID_W, dtype=F32), rows)
    inv = ROPE_BASE ** (-jnp.arange(ROPE_FREQS, dtype=F32) / ROPE_FREQS)
    ang_r = row[:, None] * inv
    ang_c = col[:, None] * inv
    cos = jnp.concatenate([jnp.cos(ang_r), jnp.cos(ang_r), jnp.cos(ang_c), jnp.cos(ang_c)], axis=1)
    sin = jnp.concatenate([-jnp.sin(ang_r), jnp.sin(ang_r), -jnp.sin(ang_c), jnp.sin(ang_c)], axis=1)
    return jnp.tile(cos, (1, 2)), jnp.tile(sin, (1, 2))


def _head_rms(x, ones_bd, g):
    hi, lo = _split2(x * x)
    ms = (_dot(hi, ones_bd) + _dot(lo, ones_bd)) * (1.0 / ATT_HEAD_DIM)
    return x * lax.rsqrt(ms + EPS) * g


def _rope(x, cos, sin_signed):
    lane = lax.broadcasted_iota(jnp.int32, x.shape, 1)
    first_half = (lane % (2 * ROPE_FREQS)) < ROPE_FREQS
    partner = jnp.where(first_half, pltpu.roll(x, LANES - ROPE_FREQS, 1), pltpu.roll(x, ROPE_FREQS, 1))
    return x * cos + partner * sin_signed


def _qkprep_kernel(q_ref, kv_ref, cos_ref, sin_ref, qg_ref, kg_ref, qo_ref, ko_ref, vo_ref, *, n_ctx, chunk):
    s_tot = q_ref.shape[0]
    r_i = lax.broadcasted_iota(jnp.int32, (LANES, LANES), 0) // ATT_HEAD_DIM
    c_i = lax.broadcasted_iota(jnp.int32, (LANES, LANES), 1) // ATT_HEAD_DIM
    ones_bd = jnp.where(r_i == c_i, 1.0, 0.0).astype(BF16)
    lane = lax.broadcasted_iota(jnp.int32, (chunk, LANES), 1)
    low = lane < ATT_HEAD_DIM
    for r0 in range(0, s_tot, chunk):
        roped = r0 >= n_ctx
        if roped:
            cos = cos_ref[r0 - n_ctx:r0 - n_ctx + chunk, :]
            sin = sin_ref[r0 - n_ctx:r0 - n_ctx + chunk, :]
        for s in range(ATT_WIDTH // LANES):
            x = _head_rms(q_ref[r0:r0 + chunk, s * LANES:(s + 1) * LANES], ones_bd, qg_ref[...])
            if roped:
                x = _rope(x, cos, sin)
            qo_ref[r0:r0 + chunk, s * LANES:(s + 1) * LANES] = (x * (ATT_HEAD_DIM ** -0.5)).astype(BF16)
        k = _head_rms(kv_ref[r0:r0 + chunk, 0:LANES], ones_bd, kg_ref[...])
        if roped:
            k = _rope(k, cos, sin)
        v = kv_ref[r0:r0 + chunk, LANES:2 * LANES]
        k_sw = pltpu.roll(k, ATT_HEAD_DIM, 1)
        v_sw = pltpu.roll(v, ATT_HEAD_DIM, 1)
        zero = jnp.zeros_like(k)
        ks = (jnp.where(low, k, zero), jnp.where(low, zero, k_sw), jnp.where(low, k_sw, zero), jnp.where(low, zero, k))
        vs = (jnp.where(low, v, zero), jnp.where(low, zero, v_sw), jnp.where(low, v_sw, zero), jnp.where(low, zero, v))
        for n in range(4):
            ko_ref[0, n, r0:r0 + chunk, :] = ks[n].astype(BF16)
            vo_ref[0, n, r0:r0 + chunk, :] = vs[n].astype(BF16)


def _qkprep(proj, cos, sin, q_g, k_g, bsz, s_tot, n_ctx):
    qg = jnp.tile(q_g, 2).reshape(1, LANES)
    kg = jnp.tile(k_g, 2).reshape(1, LANES)
    kv_shape = jax.ShapeDtypeStruct((bsz, 4, s_tot, LANES), BF16)
    kv_spec = pl.BlockSpec((1, 4, s_tot, LANES), lambda b: (b, 0, 0, 0))
    return pl.pallas_call(
        functools.partial(_qkprep_kernel, n_ctx=n_ctx, chunk=256),
        grid=(bsz,),
        in_specs=[pl.BlockSpec((s_tot, PROJ_BLOCK), lambda b: (b, COL_QA // PROJ_BLOCK)),
                  pl.BlockSpec((s_tot, PROJ_BLOCK), lambda b: (b, COL_KA // PROJ_BLOCK)),
                  pl.BlockSpec(cos.shape, lambda b: (0, 0)), pl.BlockSpec(sin.shape, lambda b: (0, 0)),
                  pl.BlockSpec((1, LANES), lambda b: (0, 0)), pl.BlockSpec((1, LANES), lambda b: (0, 0))],
        out_specs=[pl.BlockSpec((s_tot, ATT_WIDTH), lambda b: (b, 0)), kv_spec, kv_spec],
        out_shape=[jax.ShapeDtypeStruct((bsz * s_tot, ATT_WIDTH), BF16), kv_shape, kv_shape],
        compiler_params=_cparams(("arbitrary",)),
        name="qk_prep",
    )(proj, proj, cos, sin, qg, kg)


def _attend(q_ref, k_ref, v_ref, o_ref, n_keys):
    tq = q_ref.shape[0]
    for hk in range(ATT_KV_HEADS):
        qs = jnp.concatenate([q_ref[:, (2 * hk) * LANES:(2 * hk + 1) * LANES],
                              q_ref[:, (2 * hk + 1) * LANES:(2 * hk + 2) * LANES]], axis=0)
        acc = jnp.zeros((2 * tq, LANES), F32)
        for p in range(2):
            s = _dot_nt(qs, k_ref[0, 2 * hk + p, 0:n_keys, :])
            m = jnp.max(s, axis=-1, keepdims=True)
            e = jnp.exp(s - m)
            l = jnp.sum(e, axis=-1, keepdims=True)
            acc = acc + _dot(e.astype(BF16), v_ref[0, 2 * hk + p, 0:n_keys, :]) / l
        o_ref[:, (2 * hk) * LANES:(2 * hk + 1) * LANES] = acc[0:tq].astype(BF16)
        o_ref[:, (2 * hk + 1) * LANES:(2 * hk + 2) * LANES] = acc[tq:2 * tq].astype(BF16)


def _attn_kernel(q_ref, k_ref, v_ref, o_ref, *, n_ctx, first_block):
    qi = pl.program_id(1) + first_block
    s_tot = k_ref.shape[2]
    if first_block == 0:
        @pl.when(qi == 0)
        def _():
            _attend(q_ref, k_ref, v_ref, o_ref, n_ctx)

        @pl.when(qi > 0)
        def _():
            _attend(q_ref, k_ref, v_ref, o_ref, s_tot)
    else:
        _attend(q_ref, k_ref, v_ref, o_ref, s_tot)


def _attention(qn, kn, vn, bsz, s_tot, n_ctx, latent_only):
    tq = n_ctx
    blocks = s_tot // tq
    first = 1 if latent_only else 0
    kv_spec = pl.BlockSpec((1, 4, s_tot, LANES), lambda b, i: (b, 0, 0, 0))
    return pl.pallas_call(
        functools.partial(_attn_kernel, n_ctx=n_ctx, first_block=first),
        grid=(bsz, blocks - first),
        in_specs=[pl.BlockSpec((tq, ATT_WIDTH), lambda b, i: (b * blocks + i + first, 0)), kv_spec, kv_spec],
        out_specs=pl.BlockSpec((tq, ATT_WIDTH), lambda b, i: (b * (blocks - first) + i, 0)),
        out_shape=jax.ShapeDtypeStruct((bsz * (blocks - first) * tq, ATT_WIDTH), BF16),
        compiler_params=_cparams(("arbitrary", "arbitrary")),
        name="attention",
    )(qn, kn, vn)


def _log_sigmoid(x):
    return jnp.minimum(x, 0.0) - jnp.log(1.0 + jnp.exp(-jnp.abs(x)))


def _mlstm_kernel(qf_ref, kf_ref, vf_ref, gf_ref, qb_ref, kb_ref, vb_ref, gb_ref, bias_ref,
                  hf_ref, hb_ref, c_scr, n_scr, m_scr):
    i = pl.program_id(1)
    t = ML_CHUNK

    @pl.when(i == 0)
    def _():
        c_scr[...] = jnp.zeros_like(c_scr)
        n_scr[...] = jnp.zeros_like(n_scr)
        m_scr[...] = jnp.zeros_like(m_scr)

    r_i = lax.broadcasted_iota(jnp.int32, (t, t), 0)
    c_i = lax.broadcasted_iota(jnp.int32, (t, t), 1)
    lower = r_i >= c_i
    upper = r_i <= c_i
    lower_m = jnp.where(lower, 1.0, 0.0).astype(BF16)
    upper_m = jnp.where(upper, 1.0, 0.0).astype(BF16)

    for d in range(2):
        q_ref, k_ref, v_ref, g_ref, h_ref = ((qf_ref, kf_ref, vf_ref, gf_ref, hf_ref),
                                             (qb_ref, kb_ref, vb_ref, gb_ref, hb_ref))[d]
        g = g_ref[...] + bias_ref[...]
        g_t = g.T
        lf = _log_sigmoid(g)
        lf_t = _log_sigmoid(g_t)
        causal, causal_m, anti_m = (lower, lower_m, upper_m) if d == 0 else (upper, upper_m, lower_m)
        b_cols = _dot_exact_lhs(causal_m, lf)
        b_rows = _dot_exact_rhs(lf_t, anti_m)
        last = t - 1 if d == 0 else 0
        for h in range(ML_HEADS):
            ci = d * 2 * ML_HEADS + h
            cf = ci + ML_HEADS
            i_col = g[:, ci:ci + 1]
            i_row = g_t[ci:ci + 1, :]
            b_col = b_cols[:, cf:cf + 1]
            b_row = b_rows[cf:cf + 1, :]
            b_last = b_cols[last:last + 1, cf:cf + 1]
            m_prev = m_scr[d, h]
            c_prev = c_scr[d, h]
            n_prev = n_scr[d, h]

            q = q_ref[:, h * ML_HEAD_DIM:(h + 1) * ML_HEAD_DIM]
            k = k_ref[:, h * ML_HEAD_DIM:(h + 1) * ML_HEAD_DIM] * (ML_HEAD_DIM ** -0.5)
            v = v_ref[:, h * ML_HEAD_DIM:(h + 1) * ML_HEAD_DIM]
            q16, k16, v16 = q.astype(BF16), k.astype(BF16), v.astype(BF16)

            dmat = jnp.where(causal, b_col - b_row + i_row, NEG_INF)
            inter = b_col + m_prev
            m_t = jnp.maximum(inter, jnp.max(dmat, axis=-1, keepdims=True))
            w = jnp.exp(dmat - m_t)
            a_inter = jnp.exp(inter - m_t)
            qk = _dot_nt(q16, k16) * w
            num = a_inter * _dot(q16, c_prev.astype(BF16)) + _dot(qk.astype(BF16), v16)
            den = a_inter * jnp.sum(q * n_prev, axis=-1, keepdims=True) + jnp.sum(qk, axis=-1, keepdims=True)
            h_ref[:, h * ML_HEAD_DIM:(h + 1) * ML_HEAD_DIM] = num / jnp.maximum(jnp.abs(den), jnp.exp(-m_t))

            d_last_col = b_last - b_col + i_col
            m_new = jnp.maximum(b_last + m_prev, jnp.max(d_last_col, axis=0, keepdims=True))
            w_last = jnp.exp(d_last_col - m_new)
            decay = jnp.exp(b_last + m_prev - m_new)
            kw = k * w_last
            c_scr[d, h] = decay * c_prev + _dot(kw.T.astype(BF16), v16)
            n_scr[d, h] = decay * n_prev + jnp.sum(kw, axis=0, keepdims=True)
            m_scr[d, h] = m_new


def _mlstm(proj, gate_b, bsz, s_tot, n_ctx):
    t = ML_CHUNK
    n_chunks = s_tot // t
    ctx_chunks = n_ctx // t

    def fwd(b, i):
        return b * n_chunks + i

    def bwd(b, i):
        return b * n_chunks + jnp.where(i < ctx_chunks, ctx_chunks - 1 - i, n_chunks - 1 + ctx_chunks - i)

    def col(c0, width):
        return c0 // width

    def specs(rowfn):
        return [pl.BlockSpec((t, ML_WIDTH), lambda b, i: (rowfn(b, i), col(COL_QM, ML_WIDTH))),
                pl.BlockSpec((t, ML_WIDTH), lambda b, i: (rowfn(b, i), col(COL_KM, ML_WIDTH))),
                pl.BlockSpec((t, ML_WIDTH), lambda b, i: (rowfn(b, i), col(COL_VM, ML_WIDTH))),
                pl.BlockSpec((t, LANES), lambda b, i: (rowfn(b, i), col(COL_GM, LANES)))]

    bias = jnp.zeros((1, LANES), F32).at[0, :ML_GATES].set(gate_b.reshape(ML_GATES))
    shp = jax.ShapeDtypeStruct((bsz * s_tot, ML_WIDTH), F32)
    return pl.pallas_call(
        _mlstm_kernel,
        grid=(bsz, n_chunks),
        in_specs=specs(fwd) + specs(bwd) + [pl.BlockSpec((1, LANES), lambda b, i: (0, 0))],
        out_specs=[pl.BlockSpec((t, ML_WIDTH), lambda b, i: (fwd(b, i), 0)),
                   pl.BlockSpec((t, ML_WIDTH), lambda b, i: (bwd(b, i), 0))],
        out_shape=[shp, shp],
        scratch_shapes=[pltpu.VMEM((2, ML_HEADS, ML_HEAD_DIM, ML_HEAD_DIM), F32),
                        pltpu.VMEM((2, ML_HEADS, 1, ML_HEAD_DIM), F32),
                        pltpu.VMEM((2, ML_HEADS, 1, 1), F32)],
        compiler_params=_cparams(("arbitrary", "arbitrary")),
        name="mlstm",
    )(*([proj] * 8), bias)


def _merge_kernel(ys_ref, ya_ref, hf_ref, hb_ref, om_ref, gl_ref, x_ref, mod_ref, mg_ref, n2_ref,
                  wbs_ref, wba_ref, wbm_ref, wo_ref, xo_ref, h2_ref, *, blocks_per_batch, first_block):
    n = pl.program_id(0)
    b = n // (blocks_per_batch - first_block)
    if not first_block:
        b = jnp.where(n % blocks_per_batch == 0, CTX_MOD_ROW, b)
    is_ctx = None

    hs = hf_ref[...] + hb_ref[...]
    parts = []
    for h in range(ML_HEADS):
        sl = slice(h * ML_HEAD_DIM, (h + 1) * ML_HEAD_DIM)
        parts.append(_rms(hs[:, sl], mg_ref[:, sl]))
    hn = jnp.concatenate(parts, axis=1) * _sigmoid(om_ref[...])

    gl = gl_ref[...]
    merged = (_sigmoid(gl[:, 0:D_MODEL]) * _dot(ys_ref[...], wbs_ref[...])
              + _sigmoid(gl[:, D_MODEL:2 * D_MODEL]) * _dot(ya_ref[...], wba_ref[...])
              + _sigmoid(gl[:, 2 * D_MODEL:3 * D_MODEL]) * _dot(hn.astype(BF16), wbm_ref[...]))
    mix = _dot(merged.astype(BF16), wo_ref[...])
    g1 = _mod_rows(mod_ref, 2, b, is_ctx)
    x = x_ref[...] + g1 * mix
    xo_ref[...] = x
    sh2 = _mod_rows(mod_ref, 3, b, is_ctx)
    sc2 = _mod_rows(mod_ref, 4, b, is_ctx)
    h2_ref[...] = (_rms(x, n2_ref[...]) * (1.0 + sc2) + sh2).astype(BF16)


def _merge(ys, ya, hf, hb, proj, stream, mod_l, ml_norm_g, norm2_g, wbs, wba, wbm, wo,
           bsz, s_tot, n_ctx, latent_only):
    tm = n_ctx
    bpb = s_tot // tm
    first = 1 if latent_only else 0
    per = bpb - first

    def rows(n):
        return (n // per) * bpb + n % per + first

    n_blocks = bsz * per
    full = lambda a: pl.BlockSpec(a.shape, lambda n: (0,) * a.ndim)
    wide = lambda w: pl.BlockSpec((tm, w), lambda n: (rows(n), 0))
    mg = ml_norm_g.reshape(1, ML_WIDTH)
    n2 = norm2_g.reshape(1, D_MODEL)
    return pl.pallas_call(
        functools.partial(_merge_kernel, blocks_per_batch=bpb, first_block=first),
        grid=(n_blocks,),
        in_specs=[wide(SSM_WIDTH), pl.BlockSpec((tm, ATT_WIDTH), lambda n: (n, 0)), wide(ML_WIDTH), wide(ML_WIDTH),
                  pl.BlockSpec((tm, ML_WIDTH), lambda n: (rows(n), COL_OM // ML_WIDTH)),
                  pl.BlockSpec((tm, N_BRANCH * D_MODEL), lambda n: (rows(n), COL_GATE // (N_BRANCH * D_MODEL))),
                  wide(D_MODEL), full(mod_l), full(mg), full(n2), full(wbs), full(wba), full(wbm), full(wo)],
        out_specs=[pl.BlockSpec((tm, D_MODEL), lambda n: (n, 0)), pl.BlockSpec((tm, D_MODEL), lambda n: (n, 0))],
        out_shape=[jax.ShapeDtypeStruct((n_blocks * tm, D_MODEL), F32),
                   jax.ShapeDtypeStruct((n_blocks * tm, D_MODEL), BF16)],
        compiler_params=_cparams(("arbitrary",)),
        name="merge",
    )(ys, ya, hf, hb, proj, proj, stream, mod_l, mg, n2, wbs, wba, wbm, wo)


def _peer_score_kernel(x_ref, wq_ref, k1_ref, k2_ref, s1_ref, s2_ref):
    half = PEER_QDIM // 2
    q_t = _dot_nt(wq_ref[...], x_ref[...])
    for key_ref, s_ref, lo in ((k1_ref, s1_ref, 0), (k2_ref, s2_ref, half)):
        khi, klo = _split2(key_ref[...])
        qhi, qlo = _split2(q_t[lo:lo + half, :])
        s_ref[0] = _dot(khi, qhi) + _dot(khi, qlo) + _dot(klo, qhi)


def _peer_scores(h2, wq_t, k1, k2):
    n = h2.shape[0]
    tm = 1024
    shp = jax.ShapeDtypeStruct((PEER_HEADS, PEER_KEYS, n), F32)
    out = pl.BlockSpec((1, PEER_KEYS, tm), lambda i, h: (h, 0, i))
    key = pl.BlockSpec((PEER_KEYS, PEER_QDIM // 2), lambda i, h: (0, 0))
    return pl.pallas_call(
        _peer_score_kernel,
        grid=(n // tm, PEER_HEADS),
        in_specs=[pl.BlockSpec((tm, D_MODEL), lambda i, h: (i, 0)),
                  pl.BlockSpec((PEER_QDIM, D_MODEL), lambda i, h: (h, 0)), key, key],
        out_specs=[out, out],
        out_shape=[shp, shp],
        compiler_params=_cparams(("arbitrary", "arbitrary")),
        name="peer_scores",
    )(h2, wq_t, k1, k2)


def _top_values(s, k):
    work = s
    rank = jnp.full(s.shape, float(PEER_KEYS), F32)
    vals = []
    for r in range(k):
        m = jnp.max(work, axis=0, keepdims=True)
        hit = work == m
        rank = jnp.where(hit, float(r), rank)
        work = jnp.where(hit, NEG_INF, work)
        vals.append(m)
    return vals, rank


def _peer_select_kernel(s1_ref, s2_ref, r2_ref, g2_ref, cnt_ref, e1_ref, v1_scr, v2_scr):
    k = PEER_TOPK
    half = k // 2
    s1 = s1_ref[0]
    s2 = s2_ref[0]
    v1, rank1 = _top_values(s1, k)
    v2, rank2 = _top_values(s2, k)
    for a in range(k):
        v1_scr[a:a + 1, :] = v1[a]
        v2_scr[a:a + 1, :] = v2[a]
    row = lax.broadcasted_iota(jnp.int32, (half, s1.shape[1]), 0)
    pieces = [v1[0] + v2_scr[...]]
    for a in range(1, half):
        pieces.append(jnp.where(row < k // (a + 1), v1[a] + v2_scr[0:half, :], NEG_INF))
    pieces.append(v1_scr[half:k, :] + v2[0])
    top = v1[0] + v2[0]
    z = jnp.zeros_like(top)
    thr = top
    for _ in range(k):
        thr = functools.reduce(jnp.maximum, [jnp.max(p, axis=0, keepdims=True) for p in pieces])
        pieces = [jnp.where(p == thr, NEG_INF, p) for p in pieces]
        z = z + jnp.exp(thr - top)
    cnt = jnp.zeros_like(s1)
    for a in range(k):
        if a == 0:
            sums = v1[0] + v2_scr[...]
        elif a < half:
            sums = jnp.where(row < k // (a + 1), v1[a] + v2_scr[0:half, :], NEG_INF)
        else:
            sums = v1[a] + v2[0]
        n_sel = jnp.sum(jnp.where(sums >= thr, 1.0, 0.0), axis=0, keepdims=True)
        cnt = jnp.where(rank1 == float(a), n_sel, cnt)
    r2_ref[0] = rank2.astype(BF16)
    g2_ref[0] = jnp.exp(s2 - v2[0]).astype(BF16)
    cnt_ref[0] = cnt
    e1_ref[0] = jnp.exp(s1 - v1[0]) / z


def _peer_select(s1, s2):
    heads, keys, n = s1.shape
    tl = 512
    spec = pl.BlockSpec((1, keys, tl), lambda h, i: (h, 0, i))
    shp = jax.ShapeDtypeStruct((heads, keys, n), F32)
    shp16 = jax.ShapeDtypeStruct((heads, keys, n), BF16)
    return pl.pallas_call(
        _peer_select_kernel,
        grid=(heads, n // tl),
        in_specs=[spec, spec], out_specs=[spec] * 4, out_shape=[shp16, shp16, shp, shp],
        scratch_shapes=[pltpu.VMEM((PEER_TOPK, tl), F32), pltpu.VMEM((PEER_TOPK, tl), F32)],
        compiler_params=_cparams(("arbitrary", "arbitrary")),
        name="peer_select",
    )(s1, s2)


def _peer_dense_kernel(x_ref, u_ref, vt_ref, r2_ref, g2_ref, cnt_ref, e1_ref, s_ref, mod_ref, fg_ref,
                       o_ref, acc_scr, a_scr, coef_even, coef_odd, x_scr, r2_scr, g2_scr,
                       *, te, nj, blocks_per_batch, n_ctx, final):
    n = pl.program_id(0)
    tm = x_ref.shape[0]

    @pl.when(n == 0)
    def _():
        acc_scr[...] = jnp.zeros_like(acc_scr)
        coef_odd[...] = jnp.zeros_like(coef_odd)

    @pl.when(n % nj == 0)
    def _():
        x_scr[...] = x_ref[...]
        r2_scr[...] = r2_ref[...]
        g2_scr[...] = g2_ref[...]

    keys_per_step = te // PEER_KEYS
    assert keys_per_step == F32_SUBLANES or (2 * keys_per_step == F32_SUBLANES and nj % 2 == 0)

    def step(read_ref, write_ref, parity):
        row0 = parity * keys_per_step % F32_SUBLANES

        def row_tile(ref, h, s, lanes):
            return ref[h, row0 + s:row0 + s + 1, lanes].astype(BF16)

        zero = jnp.zeros((PEER_KEYS, LANES), BF16)
        n_chunks = te // PEER_CHUNK
        out_rows = D_MODEL // n_chunks
        for k in range(n_chunks):
            a_scr[k * PEER_CHUNK:(k + 1) * PEER_CHUNK, :] = _dot_nt(
                u_ref[k * PEER_CHUNK:(k + 1) * PEER_CHUNK, :], x_scr[...]).astype(BF16)
            acc_scr[k * out_rows:(k + 1) * out_rows, :] += _dot(vt_ref[k * out_rows:(k + 1) * out_rows, :], read_ref[...])
            for s in range(k * PEER_CHUNK // PEER_KEYS, (k + 1) * PEER_CHUNK // PEER_KEYS):
                rows = slice(s * PEER_KEYS, (s + 1) * PEER_KEYS)
                for l in range(tm // LANES):
                    lanes = slice(l * LANES, (l + 1) * LANES)
                    w = zero
                    for h in range(PEER_HEADS):
                        sel = jnp.where(r2_scr[h, :, lanes] < row_tile(cnt_ref, h, s, lanes), g2_scr[h, :, lanes], zero)
                        w = w + sel * row_tile(e1_ref, h, s, lanes)
                    write_ref[rows, lanes] = w * _gelu(a_scr[rows, lanes])

    @pl.when(n % 2 == 0)
    def _():
        step(coef_odd, coef_even, 0)

    @pl.when(n % 2 == 1)
    def _():
        step(coef_even, coef_odd, 1)

    prev = n - 1

    @pl.when(jnp.logical_and(n > 0, prev % nj == nj - 1))
    def _():
        ip = prev // nj
        b = ip // blocks_per_batch
        if n_ctx:
            row = (ip % blocks_per_batch) * tm + lax.broadcasted_iota(jnp.int32, (tm, 1), 0)
            is_ctx = row < n_ctx
        else:
            is_ctx = None
        g2 = _mod_rows(mod_ref, 5, b, is_ctx)
        x = s_ref[...] + g2 * acc_scr[...].T
        if final:
            x = _rms(x, fg_ref[...])
        o_ref[...] = x
        acc_scr[...] = jnp.zeros_like(acc_scr)


def _peer_dense(h2, u_bf16, vt_bf16, sel, stream, mod_l, final_g, rows_per_batch, n_ctx, final):
    n = h2.shape[0]
    r2, g2, cnt, e1 = sel
    tm = 768 if rows_per_batch % 768 == 0 else 512
    te = 1024
    bpb = rows_per_batch // tm
    nj = PEER_EXPERTS // te
    n_steps = (n // tm) * nj

    def cur(s):
        c = jnp.minimum(s, n_steps - 1)
        return c // nj, c % nj

    def prev(s):
        p = jnp.maximum(s - 1, 0)
        return p // nj, p % nj

    tok = pl.BlockSpec((PEER_HEADS, PEER_KEYS, tm), lambda s: (0, 0, cur(s)[0]))
    sub = pl.BlockSpec((PEER_HEADS, F32_SUBLANES, tm),
                       lambda s: (0, cur(s)[1] * (te // PEER_KEYS) // F32_SUBLANES, cur(s)[0]))
    full = lambda a: pl.BlockSpec(a.shape, lambda s: (0,) * a.ndim)
    fg = final_g.reshape(1, D_MODEL)
    coef = pltpu.VMEM((te, tm), BF16)
    return pl.pallas_call(
        functools.partial(_peer_dense_kernel, te=te, nj=nj, blocks_per_batch=bpb, n_ctx=n_ctx, final=final),
        grid=(n_steps + 1,),
        in_specs=[pl.BlockSpec((tm, D_MODEL), lambda s: (cur(s)[0], 0)),
                  pl.BlockSpec((te, D_MODEL), lambda s: (cur(s)[1], 0)),
                  pl.BlockSpec((D_MODEL, te), lambda s: (0, prev(s)[1])),
                  tok, tok, sub, sub,
                  pl.BlockSpec((tm, D_MODEL), lambda s: (prev(s)[0], 0)), full(mod_l), full(fg)],
        out_specs=pl.BlockSpec((tm, D_MODEL), lambda s: (prev(s)[0], 0)),
        out_shape=jax.ShapeDtypeStruct((n, D_MODEL), F32),
        scratch_shapes=[pltpu.VMEM((D_MODEL, tm), F32), coef, coef, coef, pltpu.VMEM((tm, D_MODEL), BF16),
                        pltpu.VMEM((PEER_HEADS, PEER_KEYS, tm), BF16), pltpu.VMEM((PEER_HEADS, PEER_KEYS, tm), BF16)],
        compiler_params=_cparams(("arbitrary",)),
        name="peer_dense",
    )(h2, u_bf16, vt_bf16, r2, g2, cnt, e1, stream, mod_l, fg)


def _reorder_w_in(w):
    offs = [0]
    for width in (SSM_WIDTH, ATT_WIDTH, KV_WIDTH, KV_WIDTH, ML_WIDTH, ML_WIDTH, ML_WIDTH, ML_WIDTH, ML_GATES,
                  N_BRANCH * D_MODEL):
        offs.append(offs[-1] + width)
    u_s, q_a, k_a, v_a, q_m, k_m, v_m, o_m, g_m, gate = (w[:, offs[n]:offs[n + 1]] for n in range(10))
    pad = jnp.zeros((w.shape[0], PROJ_WIDTH - COL_GM - ML_GATES), w.dtype)
    return jnp.concatenate([u_s, q_a, q_m, k_m, v_m, o_m, gate, k_a, v_a, g_m, pad], axis=1).astype(BF16)


def kernel(x, c, ctx, c_ctx, w_mod, b_mod, norm1_g, norm2_g, w_in, ssm_lam_re, ssm_lam_im, ssm_log_step, ssm_b_re, ssm_b_im, ssm_c_re, ssm_c_im, ssm_d, ssm_w_glu, attn_q_norm_g, attn_k_norm_g, mlstm_gate_b, mlstm_norm_g, w_branch_ssm, w_branch_attn, w_branch_mlstm, w_out, peer_w_q, peer_sub_k1, peer_sub_k2, peer_u, peer_v, final_norm_g):
    bsz, lat_len, d = x.shape
    n_ctx = ctx.shape[1]
    s_tot = n_ctx + lat_len
    depth = w_in.shape[0]
    assert d == D_MODEL and n_ctx == 256 and lat_len % n_ctx == 0 and bsz == 8

    cos, sin = _rope_tables(lat_len)
    mod = _modulation(c, c_ctx, w_mod, b_mod)
    stream = jnp.concatenate([ctx, x], axis=1)

    for layer in range(depth):
        last = layer == depth - 1
        mod_l = mod[layer]
        proj = _inproj(stream.reshape(bsz, s_tot, d), norm1_g[layer], mod_l, _reorder_w_in(w_in[layer]), n_ctx)

        u_tm = proj[:, COL_U:COL_U + SSM_WIDTH].reshape(bsz, s_tot, SSM_WIDTH).transpose(1, 0, 2)
        u_tm = u_tm.reshape(s_tot * bsz, SSM_WIDTH)
        params = _s5_params(ssm_lam_re[layer], ssm_lam_im[layer], ssm_log_step[layer], ssm_b_re[layer],
                            ssm_b_im[layer], ssm_c_re[layer], ssm_c_im[layer])
        yf, yb = _s5_scan(u_tm, params, bsz, s_tot, n_ctx)
        ys_tm = _s5_post(u_tm, yf, yb, ssm_d[layer], ssm_w_glu[layer].astype(BF16))
        ys = ys_tm.reshape(s_tot, bsz, SSM_WIDTH).transpose(1, 0, 2).reshape(bsz * s_tot, SSM_WIDTH)

        qn, kn, vn = _qkprep(proj, cos, sin, attn_q_norm_g[layer], attn_k_norm_g[layer], bsz, s_tot, n_ctx)
        ya = _attention(qn, kn, vn, bsz, s_tot, n_ctx, last)

        hf, hb = _mlstm(proj, mlstm_gate_b[layer], bsz, s_tot, n_ctx)

        stream2, h2 = _merge(ys, ya, hf, hb, proj, stream.reshape(bsz * s_tot, d), mod_l, mlstm_norm_g[layer],
                             norm2_g[layer], w_branch_ssm[layer].astype(BF16), w_branch_attn[layer].astype(BF16),
                             w_branch_mlstm[layer].astype(BF16), w_out[layer].astype(BF16),
                             bsz, s_tot, n_ctx, last)

        s1, s2 = _peer_scores(h2, peer_w_q[layer].T.astype(BF16), peer_sub_k1[layer], peer_sub_k2[layer])
        sel = _peer_select(s1, s2)
        rows_per_batch = lat_len if last else s_tot
        stream = _peer_dense(h2, peer_u[layer].astype(BF16), peer_v[layer].T.astype(BF16), sel, stream2, mod_l,
                             final_norm_g, rows_per_batch, 0 if last else n_ctx, last)

    return stream.reshape(bsz, lat_len, d)
```

```python
import functools
import math

import jax
import jax.numpy as jnp
from jax import lax
from jax.experimental import pallas as pl
from jax.experimental.pallas import tpu as pltpu

F32 = jnp.float32
BF16 = jnp.bfloat16

D_MODEL = 1024
GRID_W = 64
EPS = 1e-6
N_MOD = 6
N_BRANCH = 3
SSM_WIDTH = 512
SSM_GROUP = 16
SSM_GROUPS = 32
SSM_STATE = 64
ATT_HEADS = 8
ATT_KV_HEADS = 2
ATT_HEAD_DIM = 64
ATT_WIDTH = 512
KV_WIDTH = 128
ROPE_FREQS = 16
ROPE_BASE = 10000.0
ML_HEADS = 4
ML_HEAD_DIM = 128
ML_WIDTH = 512
ML_GATES = 16
ML_CHUNK = 128
PEER_HEADS = 8
PEER_KEYS = 128
PEER_EXPERTS = PEER_KEYS * PEER_KEYS
PEER_QDIM = 256
PEER_TOPK = 16
PEER_TOK = 256

LANES = 128
F32_SUBLANES = 8
BF16_SUBLANES = 16
VMEM_LIMIT_BYTES = 56 * 1024 * 1024

PROJ_BLOCK = 512
COL_U, COL_QA, COL_QM, COL_KM, COL_VM, COL_OM = (i * PROJ_BLOCK for i in range(6))
COL_GATE = 6 * PROJ_BLOCK
COL_KA = COL_GATE + N_BRANCH * D_MODEL
COL_VA = COL_KA + KV_WIDTH
COL_GM = COL_VA + KV_WIDTH
PROJ_WIDTH = COL_KA + PROJ_BLOCK

S5_BLOCKS = 4
S5_BLOCK_IN = SSM_WIDTH // S5_BLOCKS
S5_BLOCK_STATE = SSM_GROUPS * SSM_STATE // S5_BLOCKS
S5_CHUNK = 64

NEG_INF = float("-inf")


def _cparams(sem, flags=None):
    return pltpu.CompilerParams(dimension_semantics=sem, vmem_limit_bytes=VMEM_LIMIT_BYTES, flags=flags)


def _split2(x):
    hi = x.astype(BF16)
    lo = (x - hi.astype(F32)).astype(BF16)
    return hi, lo


def _split3(x):
    hi = x.astype(BF16)
    r = x - hi.astype(F32)
    mid = r.astype(BF16)
    lo = (r - mid.astype(F32)).astype(BF16)
    return hi, mid, lo


def _dot(a, b):
    return jnp.dot(a, b, preferred_element_type=F32)


def _dot_nt(a, b):
    return lax.dot_general(a, b, (((1,), (1,)), ((), ())), preferred_element_type=F32)


def _dot_exact_rhs(x, m_bf16):
    hi, mid, lo = _split3(x)
    return _dot(hi, m_bf16) + _dot(mid, m_bf16) + _dot(lo, m_bf16)


def _dot_exact_lhs(m_bf16, x):
    hi, mid, lo = _split3(x)
    return _dot(m_bf16, hi) + _dot(m_bf16, mid) + _dot(m_bf16, lo)


def _sigmoid(x):
    return 1.0 / (1.0 + jnp.exp(-x))


def _gelu(x):
    return jax.nn.gelu(x, approximate=True)


def _mod_kernel(v_ref, w_ref, b_ref, o_ref):
    v = v_ref[...]
    sv = v * _sigmoid(v)
    w = w_ref[0]
    hi, mid, lo = _split3(sv)
    whi, wlo = _split2(w)
    acc = _dot(hi, whi) + _dot(mid, whi) + _dot(hi, wlo) + _dot(lo, whi) + _dot(mid, wlo)
    o_ref[0] = acc + b_ref[0]


def _modulation(c, c_ctx, w_mod, b_mod):
    depth = w_mod.shape[0]
    n_out = w_mod.shape[2]
    rows = 16
    v = jnp.zeros((rows, D_MODEL), F32).at[: c.shape[0]].set(c).at[8].set(c_ctx)
    tn = 1536
    return pl.pallas_call(
        _mod_kernel,
        grid=(depth, n_out // tn),
        in_specs=[pl.BlockSpec((rows, D_MODEL), lambda l, j: (0, 0)),
                  pl.BlockSpec((1, D_MODEL, tn), lambda l, j: (l, 0, j)),
                  pl.BlockSpec((1, 1, tn), lambda l, j: (l, 0, j))],
        out_specs=pl.BlockSpec((1, rows, tn), lambda l, j: (l, 0, j)),
        out_shape=jax.ShapeDtypeStruct((depth, rows, n_out), F32),
        compiler_params=_cparams(("arbitrary", "arbitrary")),
        name="modulation",
    )(v, w_mod, b_mod.reshape(depth, 1, n_out))


CTX_MOD_ROW = 8


def _mod_rows(mod_ref, which, b, is_ctx_col):
    lo = which * D_MODEL
    m_l = mod_ref[pl.ds(b, 1), lo:lo + D_MODEL]
    if is_ctx_col is None:
        return m_l
    m_c = mod_ref[CTX_MOD_ROW:CTX_MOD_ROW + 1, lo:lo + D_MODEL]
    return jnp.where(is_ctx_col, m_c, m_l)


def _rms(x, g):
    ms = jnp.mean(x * x, axis=-1, keepdims=True)
    return x * lax.rsqrt(ms + EPS) * g


def _inproj_kernel(x_ref, g_ref, mod_ref, w_ref, o_ref, h_scr, *, n_ctx, chunk):
    b = pl.program_id(0)
    j = pl.program_id(1)
    s_tot = x_ref.shape[1]

    @pl.when(j == 0)
    def _():
        for r0 in range(0, s_tot, chunk):
            x = x_ref[0, r0:r0 + chunk, :]
            xn = _rms(x, g_ref[...])
            row = r0 + lax.broadcasted_iota(jnp.int32, (chunk, 1), 0)
            is_ctx = row < n_ctx
            sh = _mod_rows(mod_ref, 0, b, is_ctx)
            sc = _mod_rows(mod_ref, 1, b, is_ctx)
            h_scr[r0:r0 + chunk, :] = (xn * (1.0 + sc) + sh).astype(BF16)

    o_ref[...] = _dot(h_scr[...], w_ref[...])


def _inproj(stream3, g, mod_l, w_bf16, n_ctx):
    bsz, s_tot, d = stream3.shape
    n_out = w_bf16.shape[1]
    tn = PROJ_BLOCK
    return pl.pallas_call(
        functools.partial(_inproj_kernel, n_ctx=n_ctx, chunk=256),
        grid=(bsz, n_out // tn),
        in_specs=[pl.BlockSpec((1, s_tot, d), lambda b, j: (b, 0, 0)),
                  pl.BlockSpec((1, d), lambda b, j: (0, 0)),
                  pl.BlockSpec(mod_l.shape, lambda b, j: (0, 0)),
                  pl.BlockSpec((d, tn), lambda b, j: (0, j))],
        out_specs=pl.BlockSpec((s_tot, tn), lambda b, j: (b, j)),
        out_shape=jax.ShapeDtypeStruct((bsz * s_tot, n_out), F32),
        scratch_shapes=[pltpu.VMEM((s_tot, d), BF16)],
        compiler_params=_cparams(("arbitrary", "arbitrary")),
        name="inproj",
    )(stream3, g.reshape(1, d), mod_l, w_bf16)


def _s5_param_kernel(lre_ref, lim_ref, ls_ref, bre_ref, bim_ref, are_ref, aim_ref, bbre_ref, bbim_ref):
    lre = lre_ref[...]
    lim = lim_ref[...]
    step = jnp.exp(ls_ref[...])
    mag = jnp.exp(lre * step)
    a_re = mag * jnp.cos(lim * step)
    a_im = mag * jnp.sin(lim * step)
    den = lre * lre + lim * lim
    z_re = ((a_re - 1.0) * lre + a_im * lim) / den
    z_im = (a_im * lre - (a_re - 1.0) * lim) / den
    b_re = bre_ref[...]
    b_im = bim_ref[...]
    are_ref[...] = a_re
    aim_ref[...] = a_im
    bbre_ref[...] = z_re * b_re - z_im * b_im
    bbim_ref[...] = z_re * b_im + z_im * b_re


def _s5_params(lam_re, lam_im, log_step, b_re, b_im, c_re, c_im):
    nd, g, n = lam_re.shape
    c = b_re.shape[-1]
    rows = nd * g
    wide = n * c

    def expand(z):
        return jnp.broadcast_to(z.reshape(rows, n, 1), (rows, n, c)).reshape(rows, wide)

    ls = jnp.broadcast_to(log_step.reshape(rows, 1), (rows, wide))
    spec = pl.BlockSpec((rows, wide), lambda: (0, 0))
    shp = jax.ShapeDtypeStruct((rows, wide), F32)
    a_re, a_im, bb_re, bb_im = pl.pallas_call(
        _s5_param_kernel,
        in_specs=[spec] * 5, out_specs=[spec] * 4, out_shape=[shp] * 4,
        name="s5_params",
    )(expand(lam_re), expand(lam_im), ls, b_re.reshape(rows, wide), b_im.reshape(rows, wide))

    gpb = g // S5_BLOCKS
    eye = jnp.eye(gpb, dtype=F32)

    def diag_in(bb):
        bb = bb.reshape(nd, S5_BLOCKS, gpb, n, c)
        return jnp.einsum("dkgnc,gh->dkgchn", bb, eye).reshape(nd, S5_BLOCKS, gpb * c, gpb * n).astype(BF16)

    def diag_out(cc):
        cc = cc.reshape(nd, S5_BLOCKS, gpb, c, n)
        return jnp.einsum("dkgcn,gh->dkgnhc", cc, eye).reshape(nd, S5_BLOCKS, gpb * n, gpb * c).astype(BF16)

    def decay(a):
        a = a.reshape(nd, S5_BLOCKS, gpb, n, c)[..., 0].reshape(nd, S5_BLOCKS, 1, gpb * n)
        return jnp.broadcast_to(a, (nd, S5_BLOCKS, 8, gpb * n))

    return decay(a_re), decay(a_im), diag_in(bb_re), diag_in(bb_im), diag_out(c_re), diag_out(c_im)


def _s5_kernel(uf_ref, ub_ref, are_ref, aim_ref, bre_ref, bim_ref, cre_ref, cim_ref,
               yf_ref, yb_ref, bur_scr, bui_scr, st_scr, *, steps, bsz):
    i = pl.program_id(0)

    @pl.when(i == 0)
    def _():
        st_scr[...] = jnp.zeros_like(st_scr)

    for d in range(2):
        u_ref = (uf_ref, ub_ref)[d]
        y_ref = (yf_ref, yb_ref)[d]
        for k in range(S5_BLOCKS):
            u = u_ref[:, k * S5_BLOCK_IN:(k + 1) * S5_BLOCK_IN].astype(BF16)
            bur_scr[...] = _dot(u, bre_ref[d, k])
            bui_scr[...] = _dot(u, bim_ref[d, k])
            ar = are_ref[d, k]
            ai = aim_ref[d, k]

            def step(t, carry, d=d, ar=ar, ai=ai):
                sr, si = carry
                tt = t if d == 0 else steps - 1 - t
                r0 = pl.multiple_of(tt * bsz, bsz)
                nr = ar * sr - ai * si + bur_scr[pl.ds(r0, bsz), :]
                ni = ar * si + ai * sr + bui_scr[pl.ds(r0, bsz), :]
                bur_scr[pl.ds(r0, bsz), :] = nr
                bui_scr[pl.ds(r0, bsz), :] = ni
                return nr, ni

            sr, si = lax.fori_loop(0, steps, step, (st_scr[d, k, 0], st_scr[d, k, 1]), unroll=8)
            st_scr[d, k, 0] = sr
            st_scr[d, k, 1] = si
            y = _dot(bur_scr[...].astype(BF16), cre_ref[d, k]) - _dot(bui_scr[...].astype(BF16), cim_ref[d, k])
            y_ref[:, k * S5_BLOCK_IN:(k + 1) * S5_BLOCK_IN] = y


def _s5_scan(u_tm, params, bsz, s_tot, n_ctx):
    a_re, a_im, bb_re, bb_im, cc_re, cc_im = params
    assert bsz == 8
    rows = S5_CHUNK * bsz
    n_chunks = s_tot // S5_CHUNK
    ctx_chunks = n_ctx // S5_CHUNK

    def bwd_chunk(i):
        return jnp.where(i < ctx_chunks, ctx_chunks - 1 - i, n_chunks - 1 + ctx_chunks - i)

    full = lambda a: pl.BlockSpec(a.shape, lambda i: (0,) * a.ndim)
    shp = jax.ShapeDtypeStruct((s_tot * bsz, SSM_WIDTH), F32)
    return pl.pallas_call(
        functools.partial(_s5_kernel, steps=S5_CHUNK, bsz=bsz),
        grid=(n_chunks,),
        in_specs=[pl.BlockSpec((rows, SSM_WIDTH), lambda i: (i, 0)),
                  pl.BlockSpec((rows, SSM_WIDTH), lambda i: (bwd_chunk(i), 0)),
                  full(a_re), full(a_im), full(bb_re), full(bb_im), full(cc_re), full(cc_im)],
        out_specs=[pl.BlockSpec((rows, SSM_WIDTH), lambda i: (i, 0)),
                   pl.BlockSpec((rows, SSM_WIDTH), lambda i: (bwd_chunk(i), 0))],
        out_shape=[shp, shp],
        scratch_shapes=[pltpu.VMEM((rows, S5_BLOCK_STATE), F32), pltpu.VMEM((rows, S5_BLOCK_STATE), F32),
                        pltpu.VMEM((2, S5_BLOCKS, 2, bsz, S5_BLOCK_STATE), F32)],
        compiler_params=_cparams(("arbitrary",)),
        name="s5_scan",
    )(u_tm, u_tm, a_re, a_im, bb_re, bb_im, cc_re, cc_im)


def _s5_post_kernel(u_ref, yf_ref, yb_ref, d_ref, w_ref, o_ref):
    y = d_ref[...] * u_ref[...] + yf_ref[...] + yb_ref[...]
    g = _gelu(y).astype(BF16)
    gate = _sigmoid(_dot(g, w_ref[...]))
    o_ref[...] = (g.astype(F32) * gate).astype(BF16)


def _s5_post(u_tm, yf, yb, d_skip, w_glu_bf16):
    n, w = u_tm.shape
    tm = 1024
    row = pl.BlockSpec((tm, w), lambda i: (i, 0))
    return pl.pallas_call(
        _s5_post_kernel,
        grid=(n // tm,),
        in_specs=[row, row, row, pl.BlockSpec((1, w), lambda i: (0, 0)), pl.BlockSpec((w, w), lambda i: (0, 0))],
        out_specs=row,
        out_shape=jax.ShapeDtypeStruct((n, w), BF16),
        compiler_params=_cparams(("arbitrary",)),
        name="s5_post",
    )(u_tm, yf, yb, d_skip.reshape(1, w), w_glu_bf16)


def _rope_tables(lat_len):
    rows = lat_len // GRID_W
    row = jnp.repeat(jnp.arange(rows, dtype=F32), GRID_W)
    col = jnp.tile(jnp.arange(GRID_W, dtype=F32), rows)
    inv = ROPE_BASE ** (-jnp.arange(ROPE_FREQS, dtype=F32) / ROPE_FREQS)
    ang_r = row[:, None] * inv
    ang_c = col[:, None] * inv
    cos = jnp.concatenate([jnp.cos(ang_r), jnp.cos(ang_r), jnp.cos(ang_c), jnp.cos(ang_c)], axis=1)
    sin = jnp.concatenate([-jnp.sin(ang_r), jnp.sin(ang_r), -jnp.sin(ang_c), jnp.sin(ang_c)], axis=1)
    return jnp.tile(cos, (1, 2)), jnp.tile(sin, (1, 2))


def _head_rms(x, ones_bd, g):
    hi, lo = _split2(x * x)
    ms = (_dot(hi, ones_bd) + _dot(lo, ones_bd)) * (1.0 / ATT_HEAD_DIM)
    return x * lax.rsqrt(ms + EPS) * g


def _rope(x, cos, sin_signed):
    lane = lax.broadcasted_iota(jnp.int32, x.shape, 1)
    first_half = (lane % (2 * ROPE_FREQS)) < ROPE_FREQS
    partner = jnp.where(first_half, pltpu.roll(x, LANES - ROPE_FREQS, 1), pltpu.roll(x, ROPE_FREQS, 1))
    return x * cos + partner * sin_signed


def _qkprep_kernel(q_ref, kv_ref, cos_ref, sin_ref, qg_ref, kg_ref, qo_ref, ko_ref, vo_ref, *, n_ctx, chunk):
    s_tot = q_ref.shape[0]
    r_i = lax.broadcasted_iota(jnp.int32, (LANES, LANES), 0) // ATT_HEAD_DIM
    c_i = lax.broadcasted_iota(jnp.int32, (LANES, LANES), 1) // ATT_HEAD_DIM
    ones_bd = jnp.where(r_i == c_i, 1.0, 0.0).astype(BF16)
    lane = lax.broadcasted_iota(jnp.int32, (chunk, LANES), 1)
    low = lane < ATT_HEAD_DIM
    for r0 in range(0, s_tot, chunk):
        roped = r0 >= n_ctx
        if roped:
            cos = cos_ref[r0 - n_ctx:r0 - n_ctx + chunk, :]
            sin = sin_ref[r0 - n_ctx:r0 - n_ctx + chunk, :]
        for s in range(ATT_WIDTH // LANES):
            x = _head_rms(q_ref[r0:r0 + chunk, s * LANES:(s + 1) * LANES], ones_bd, qg_ref[...])
            if roped:
                x = _rope(x, cos, sin)
            qo_ref[r0:r0 + chunk, s * LANES:(s + 1) * LANES] = (x * (ATT_HEAD_DIM ** -0.5)).astype(BF16)
        k = _head_rms(kv_ref[r0:r0 + chunk, 0:LANES], ones_bd, kg_ref[...])
        if roped:
            k = _rope(k, cos, sin)
        v = kv_ref[r0:r0 + chunk, LANES:2 * LANES]
        k_sw = pltpu.roll(k, ATT_HEAD_DIM, 1)
        v_sw = pltpu.roll(v, ATT_HEAD_DIM, 1)
        zero = jnp.zeros_like(k)
        ks = (jnp.where(low, k, zero), jnp.where(low, zero, k_sw), jnp.where(low, k_sw, zero), jnp.where(low, zero, k))
        vs = (jnp.where(low, v, zero), jnp.where(low, zero, v_sw), jnp.where(low, v_sw, zero), jnp.where(low, zero, v))
        for n in range(4):
            ko_ref[0, n, r0:r0 + chunk, :] = ks[n].astype(BF16)
            vo_ref[0, n, r0:r0 + chunk, :] = vs[n].astype(BF16)


def _qkprep(proj, cos, sin, q_g, k_g, bsz, s_tot, n_ctx):
    qg = jnp.tile(q_g, 2).reshape(1, LANES)
    kg = jnp.tile(k_g, 2).reshape(1, LANES)
    kv_shape = jax.ShapeDtypeStruct((bsz, 4, s_tot, LANES), BF16)
    kv_spec = pl.BlockSpec((1, 4, s_tot, LANES), lambda b: (b, 0, 0, 0))
    return pl.pallas_call(
        functools.partial(_qkprep_kernel, n_ctx=n_ctx, chunk=256),
        grid=(bsz,),
        in_specs=[pl.BlockSpec((s_tot, PROJ_BLOCK), lambda b: (b, COL_QA // PROJ_BLOCK)),
                  pl.BlockSpec((s_tot, PROJ_BLOCK), lambda b: (b, COL_KA // PROJ_BLOCK)),
                  pl.BlockSpec(cos.shape, lambda b: (0, 0)), pl.BlockSpec(sin.shape, lambda b: (0, 0)),
                  pl.BlockSpec((1, LANES), lambda b: (0, 0)), pl.BlockSpec((1, LANES), lambda b: (0, 0))],
        out_specs=[pl.BlockSpec((s_tot, ATT_WIDTH), lambda b: (b, 0)), kv_spec, kv_spec],
        out_shape=[jax.ShapeDtypeStruct((bsz * s_tot, ATT_WIDTH), BF16), kv_shape, kv_shape],
        compiler_params=_cparams(("arbitrary",)),
        name="qk_prep",
    )(proj, proj, cos, sin, qg, kg)


def _attend(q_ref, k_ref, v_ref, o_ref, n_keys):
    tq = q_ref.shape[0]
    for hk in range(ATT_KV_HEADS):
        qs = jnp.concatenate([q_ref[:, (2 * hk) * LANES:(2 * hk + 1) * LANES],
                              q_ref[:, (2 * hk + 1) * LANES:(2 * hk + 2) * LANES]], axis=0)
        acc = jnp.zeros((2 * tq, LANES), F32)
        for p in range(2):
            s = _dot_nt(qs, k_ref[0, 2 * hk + p, 0:n_keys, :])
            m = jnp.max(s, axis=-1, keepdims=True)
            e = jnp.exp(s - m)
            l = jnp.sum(e, axis=-1, keepdims=True)
            acc = acc + _dot(e.astype(BF16), v_ref[0, 2 * hk + p, 0:n_keys, :]) / l
        o_ref[:, (2 * hk) * LANES:(2 * hk + 1) * LANES] = acc[0:tq].astype(BF16)
        o_ref[:, (2 * hk + 1) * LANES:(2 * hk + 2) * LANES] = acc[tq:2 * tq].astype(BF16)


def _attn_kernel(q_ref, k_ref, v_ref, o_ref, *, n_ctx, first_block):
    qi = pl.program_id(1) + first_block
    s_tot = k_ref.shape[2]
    if first_block == 0:
        @pl.when(qi == 0)
        def _():
            _attend(q_ref, k_ref, v_ref, o_ref, n_ctx)

        @pl.when(qi > 0)
        def _():
            _attend(q_ref, k_ref, v_ref, o_ref, s_tot)
    else:
        _attend(q_ref, k_ref, v_ref, o_ref, s_tot)


def _attention(qn, kn, vn, bsz, s_tot, n_ctx, latent_only):
    tq = n_ctx
    blocks = s_tot // tq
    first = 1 if latent_only else 0
    kv_spec = pl.BlockSpec((1, 4, s_tot, LANES), lambda b, i: (b, 0, 0, 0))
    return pl.pallas_call(
        functools.partial(_attn_kernel, n_ctx=n_ctx, first_block=first),
        grid=(bsz, blocks - first),
        in_specs=[pl.BlockSpec((tq, ATT_WIDTH), lambda b, i: (b * blocks + i + first, 0)), kv_spec, kv_spec],
        out_specs=pl.BlockSpec((tq, ATT_WIDTH), lambda b, i: (b * (blocks - first) + i, 0)),
        out_shape=jax.ShapeDtypeStruct((bsz * (blocks - first) * tq, ATT_WIDTH), BF16),
        compiler_params=_cparams(("arbitrary", "arbitrary")),
        name="attention",
    )(qn, kn, vn)


def _log_sigmoid(x):
    return jnp.minimum(x, 0.0) - jnp.log(1.0 + jnp.exp(-jnp.abs(x)))


def _mlstm_kernel(qf_ref, kf_ref, vf_ref, gf_ref, qb_ref, kb_ref, vb_ref, gb_ref, bias_ref,
                  hf_ref, hb_ref, c_scr, n_scr, m_scr):
    i = pl.program_id(1)
    t = ML_CHUNK

    @pl.when(i == 0)
    def _():
        c_scr[...] = jnp.zeros_like(c_scr)
        n_scr[...] = jnp.zeros_like(n_scr)
        m_scr[...] = jnp.zeros_like(m_scr)

    r_i = lax.broadcasted_iota(jnp.int32, (t, t), 0)
    c_i = lax.broadcasted_iota(jnp.int32, (t, t), 1)
    lower = r_i >= c_i
    upper = r_i <= c_i
    lower_m = jnp.where(lower, 1.0, 0.0).astype(BF16)
    upper_m = jnp.where(upper, 1.0, 0.0).astype(BF16)

    for d in range(2):
        q_ref, k_ref, v_ref, g_ref, h_ref = ((qf_ref, kf_ref, vf_ref, gf_ref, hf_ref),
                                             (qb_ref, kb_ref, vb_ref, gb_ref, hb_ref))[d]
        g = g_ref[...] + bias_ref[...]
        g_t = g.T
        lf = _log_sigmoid(g)
        lf_t = _log_sigmoid(g_t)
        causal, causal_m, anti_m = (lower, lower_m, upper_m) if d == 0 else (upper, upper_m, lower_m)
        b_cols = _dot_exact_lhs(causal_m, lf)
        b_rows = _dot_exact_rhs(lf_t, anti_m)
        last = t - 1 if d == 0 else 0
        for h in range(ML_HEADS):
            ci = d * 2 * ML_HEADS + h
            cf = ci + ML_HEADS
            i_col = g[:, ci:ci + 1]
            i_row = g_t[ci:ci + 1, :]
            b_col = b_cols[:, cf:cf + 1]
            b_row = b_rows[cf:cf + 1, :]
            b_last = b_cols[last:last + 1, cf:cf + 1]
            m_prev = m_scr[d, h]
            c_prev = c_scr[d, h]
            n_prev = n_scr[d, h]

            q = q_ref[:, h * ML_HEAD_DIM:(h + 1) * ML_HEAD_DIM]
            k = k_ref[:, h * ML_HEAD_DIM:(h + 1) * ML_HEAD_DIM] * (ML_HEAD_DIM ** -0.5)
            v = v_ref[:, h * ML_HEAD_DIM:(h + 1) * ML_HEAD_DIM]
            q16, k16, v16 = q.astype(BF16), k.astype(BF16), v.astype(BF16)

            dmat = jnp.where(causal, b_col - b_row + i_row, NEG_INF)
            inter = b_col + m_prev
            m_t = jnp.maximum(inter, jnp.max(dmat, axis=-1, keepdims=True))
            w = jnp.exp(dmat - m_t)
            a_inter = jnp.exp(inter - m_t)
            qk = _dot_nt(q16, k16) * w
            num = a_inter * _dot(q16, c_prev.astype(BF16)) + _dot(qk.astype(BF16), v16)
            den = a_inter * jnp.sum(q * n_prev, axis=-1, keepdims=True) + jnp.sum(qk, axis=-1, keepdims=True)
            h_ref[:, h * ML_HEAD_DIM:(h + 1) * ML_HEAD_DIM] = num / jnp.maximum(jnp.abs(den), jnp.exp(-m_t))

            d_last_col = b_last - b_col + i_col
            m_new = jnp.maximum(b_last + m_prev, jnp.max(d_last_col, axis=0, keepdims=True))
            w_last = jnp.exp(d_last_col - m_new)
            decay = jnp.exp(b_last + m_prev - m_new)
            kw = k * w_last
            c_scr[d, h] = decay * c_prev + _dot(kw.T.astype(BF16), v16)
            n_scr[d, h] = decay * n_prev + jnp.sum(kw, axis=0, keepdims=True)
            m_scr[d, h] = m_new


def _mlstm(proj, gate_b, bsz, s_tot, n_ctx):
    t = ML_CHUNK
    n_chunks = s_tot // t
    ctx_chunks = n_ctx // t

    def fwd(b, i):
        return b * n_chunks + i

    def bwd(b, i):
        return b * n_chunks + jnp.where(i < ctx_chunks, ctx_chunks - 1 - i, n_chunks - 1 + ctx_chunks - i)

    def col(c0, width):
        return c0 // width

    def specs(rowfn):
        return [pl.BlockSpec((t, ML_WIDTH), lambda b, i: (rowfn(b, i), col(COL_QM, ML_WIDTH))),
                pl.BlockSpec((t, ML_WIDTH), lambda b, i: (rowfn(b, i), col(COL_KM, ML_WIDTH))),
                pl.BlockSpec((t, ML_WIDTH), lambda b, i: (rowfn(b, i), col(COL_VM, ML_WIDTH))),
                pl.BlockSpec((t, LANES), lambda b, i: (rowfn(b, i), col(COL_GM, LANES)))]

    bias = jnp.zeros((1, LANES), F32).at[0, :ML_GATES].set(gate_b.reshape(ML_GATES))
    shp = jax.ShapeDtypeStruct((bsz * s_tot, ML_WIDTH), F32)
    return pl.pallas_call(
        _mlstm_kernel,
        grid=(bsz, n_chunks),
        in_specs=specs(fwd) + specs(bwd) + [pl.BlockSpec((1, LANES), lambda b, i: (0, 0))],
        out_specs=[pl.BlockSpec((t, ML_WIDTH), lambda b, i: (fwd(b, i), 0)),
                   pl.BlockSpec((t, ML_WIDTH), lambda b, i: (bwd(b, i), 0))],
        out_shape=[shp, shp],
        scratch_shapes=[pltpu.VMEM((2, ML_HEADS, ML_HEAD_DIM, ML_HEAD_DIM), F32),
                        pltpu.VMEM((2, ML_HEADS, 1, ML_HEAD_DIM), F32),
                        pltpu.VMEM((2, ML_HEADS, 1, 1), F32)],
        compiler_params=_cparams(("arbitrary", "arbitrary")),
        name="mlstm",
    )(*([proj] * 8), bias)


def _merge_kernel(ys_ref, ya_ref, hf_ref, hb_ref, om_ref, gl_ref, x_ref, mod_ref, mg_ref, n2_ref,
                  wbs_ref, wba_ref, wbm_ref, wo_ref, xo_ref, h2_ref, *, blocks_per_batch, first_block):
    n = pl.program_id(0)
    b = n // (blocks_per_batch - first_block)
    if not first_block:
        b = jnp.where(n % blocks_per_batch == 0, CTX_MOD_ROW, b)
    is_ctx = None

    hs = hf_ref[...] + hb_ref[...]
    parts = []
    for h in range(ML_HEADS):
        sl = slice(h * ML_HEAD_DIM, (h + 1) * ML_HEAD_DIM)
        parts.append(_rms(hs[:, sl], mg_ref[:, sl]))
    hn = jnp.concatenate(parts, axis=1) * _sigmoid(om_ref[...])

    gl = gl_ref[...]
    merged = (_sigmoid(gl[:, 0:D_MODEL]) * _dot(ys_ref[...], wbs_ref[...])
              + _sigmoid(gl[:, D_MODEL:2 * D_MODEL]) * _dot(ya_ref[...], wba_ref[...])
              + _sigmoid(gl[:, 2 * D_MODEL:3 * D_MODEL]) * _dot(hn.astype(BF16), wbm_ref[...]))
    mix = _dot(merged.astype(BF16), wo_ref[...])
    g1 = _mod_rows(mod_ref, 2, b, is_ctx)
    x = x_ref[...] + g1 * mix
    xo_ref[...] = x
    sh2 = _mod_rows(mod_ref, 3, b, is_ctx)
    sc2 = _mod_rows(mod_ref, 4, b, is_ctx)
    h2_ref[...] = (_rms(x, n2_ref[...]) * (1.0 + sc2) + sh2).astype(BF16)


def _merge(ys, ya, hf, hb, proj, stream, mod_l, ml_norm_g, norm2_g, wbs, wba, wbm, wo,
           bsz, s_tot, n_ctx, latent_only):
    tm = n_ctx
    bpb = s_tot // tm
    first = 1 if latent_only else 0
    per = bpb - first

    def rows(n):
        return (n // per) * bpb + n % per + first

    n_blocks = bsz * per
    full = lambda a: pl.BlockSpec(a.shape, lambda n: (0,) * a.ndim)
    wide = lambda w: pl.BlockSpec((tm, w), lambda n: (rows(n), 0))
    mg = ml_norm_g.reshape(1, ML_WIDTH)
    n2 = norm2_g.reshape(1, D_MODEL)
    return pl.pallas_call(
        functools.partial(_merge_kernel, blocks_per_batch=bpb, first_block=first),
        grid=(n_blocks,),
        in_specs=[wide(SSM_WIDTH), pl.BlockSpec((tm, ATT_WIDTH), lambda n: (n, 0)), wide(ML_WIDTH), wide(ML_WIDTH),
                  pl.BlockSpec((tm, ML_WIDTH), lambda n: (rows(n), COL_OM // ML_WIDTH)),
                  pl.BlockSpec((tm, N_BRANCH * D_MODEL), lambda n: (rows(n), COL_GATE // (N_BRANCH * D_MODEL))),
                  wide(D_MODEL), full(mod_l), full(mg), full(n2), full(wbs), full(wba), full(wbm), full(wo)],
        out_specs=[pl.BlockSpec((tm, D_MODEL), lambda n: (n, 0)), pl.BlockSpec((tm, D_MODEL), lambda n: (n, 0))],
        out_shape=[jax.ShapeDtypeStruct((n_blocks * tm, D_MODEL), F32),
                   jax.ShapeDtypeStruct((n_blocks * tm, D_MODEL), BF16)],
        compiler_params=_cparams(("arbitrary",)),
        name="merge",
    )(ys, ya, hf, hb, proj, proj, stream, mod_l, mg, n2, wbs, wba, wbm, wo)


def _peer_score_kernel(x_ref, wq_ref, k1_ref, k2_ref, s1_ref, s2_ref):
    half = PEER_QDIM // 2
    q_t = _dot_nt(wq_ref[...], x_ref[...])
    for key_ref, s_ref, lo in ((k1_ref, s1_ref, 0), (k2_ref, s2_ref, half)):
        khi, klo = _split2(key_ref[...])
        qhi, qlo = _split2(q_t[lo:lo + half, :])
        s_ref[0] = _dot(khi, qhi) + _dot(khi, qlo) + _dot(klo, qhi)


def _peer_scores(h2, wq_t, k1, k2):
    n = h2.shape[0]
    tm = 1024
    shp = jax.ShapeDtypeStruct((PEER_HEADS, PEER_KEYS, n), F32)
    out = pl.BlockSpec((1, PEER_KEYS, tm), lambda i, h: (h, 0, i))
    key = pl.BlockSpec((PEER_KEYS, PEER_QDIM // 2), lambda i, h: (0, 0))
    return pl.pallas_call(
        _peer_score_kernel,
        grid=(n // tm, PEER_HEADS),
        in_specs=[pl.BlockSpec((tm, D_MODEL), lambda i, h: (i, 0)),
                  pl.BlockSpec((PEER_QDIM, D_MODEL), lambda i, h: (h, 0)), key, key],
        out_specs=[out, out],
        out_shape=[shp, shp],
        compiler_params=_cparams(("arbitrary", "arbitrary")),
        name="peer_scores",
    )(h2, wq_t, k1, k2)


def _top_values(s, k):
    work = s
    rank = jnp.full(s.shape, float(PEER_KEYS), F32)
    vals = []
    for r in range(k):
        m = jnp.max(work, axis=0, keepdims=True)
        hit = work == m
        rank = jnp.where(hit, float(r), rank)
        work = jnp.where(hit, NEG_INF, work)
        vals.append(m)
    return vals, rank


def _peer_select_kernel(s1_ref, s2_ref, r2_ref, g2_ref, cnt_ref, e1_ref, v1_scr, v2_scr):
    k = PEER_TOPK
    half = k // 2
    s1 = s1_ref[0]
    s2 = s2_ref[0]
    v1, rank1 = _top_values(s1, k)
    v2, rank2 = _top_values(s2, k)
    for a in range(k):
        v1_scr[a:a + 1, :] = v1[a]
        v2_scr[a:a + 1, :] = v2[a]
    row = lax.broadcasted_iota(jnp.int32, (half, s1.shape[1]), 0)
    pieces = [v1[0] + v2_scr[...]]
    for a in range(1, half):
        pieces.append(jnp.where(row < k // (a + 1), v1[a] + v2_scr[0:half, :], NEG_INF))
    pieces.append(v1_scr[half:k, :] + v2[0])
    top = v1[0] + v2[0]
    z = jnp.zeros_like(top)
    thr = top
    for _ in range(k):
        thr = functools.reduce(jnp.maximum, [jnp.max(p, axis=0, keepdims=True) for p in pieces])
        pieces = [jnp.where(p == thr, NEG_INF, p) for p in pieces]
        z = z + jnp.exp(thr - top)
    cnt = jnp.zeros_like(s1)
    for a in range(k):
        if a == 0:
            sums = v1[0] + v2_scr[...]
        elif a < half:
            sums = jnp.where(row < k // (a + 1), v1[a] + v2_scr[0:half, :], NEG_INF)
        else:
            sums = v1[a] + v2[0]
        n_sel = jnp.sum(jnp.where(sums >= thr, 1.0, 0.0), axis=0, keepdims=True)
        cnt = jnp.where(rank1 == float(a), n_sel, cnt)
    r2_ref[0] = rank2.astype(BF16)
    g2_ref[0] = jnp.exp(s2 - v2[0]).astype(BF16)
    cnt_ref[0] = cnt
    e1_ref[0] = jnp.exp(s1 - v1[0]) / z


def _peer_select(s1, s2):
    heads, keys, n = s1.shape
    tl = 512
    spec = pl.BlockSpec((1, keys, tl), lambda h, i: (h, 0, i))
    shp = jax.ShapeDtypeStruct((heads, keys, n), F32)
    shp16 = jax.ShapeDtypeStruct((heads, keys, n), BF16)
    return pl.pallas_call(
        _peer_select_kernel,
        grid=(heads, n // tl),
        in_specs=[spec, spec], out_specs=[spec] * 4, out_shape=[shp16, shp16, shp, shp],
        scratch_shapes=[pltpu.VMEM((PEER_TOPK, tl), F32), pltpu.VMEM((PEER_TOPK, tl), F32)],
        compiler_params=_cparams(("arbitrary", "arbitrary")),
        name="peer_select",
    )(s1, s2)


def _peer_dense_kernel(x_ref, u_ref, vt_ref, r2_ref, g2_ref, cnt_ref, e1_ref, s_ref, mod_ref, fg_ref,
                       o_ref, acc_scr, a_scr, coef_even, coef_odd, x_scr, r2_scr, g2_scr, row_scr,
                       *, te, nj, blocks_per_batch, n_ctx, final):
    n = pl.program_id(0)
    tm = x_ref.shape[0]
    n_tiles = tm // PEER_TOK
    keys_per_step = te // PEER_KEYS
    assert keys_per_step == F32_SUBLANES

    @pl.when(n == 0)
    def _():
        acc_scr[...] = jnp.zeros_like(acc_scr)
        coef_odd[...] = jnp.zeros_like(coef_odd)

    @pl.when(n % nj == 0)
    def _():
        for t in range(n_tiles):
            tok = slice(t * PEER_TOK, (t + 1) * PEER_TOK)
            x_scr[t] = x_ref[tok, :]
            r2_scr[t] = r2_ref[:, :, tok]
            g2_scr[t] = g2_ref[:, :, tok]

    for t in range(n_tiles):
        tok = slice(t * PEER_TOK, (t + 1) * PEER_TOK)
        row_scr[t, 0] = cnt_ref[:, :, tok]
        row_scr[t, 1] = e1_ref[:, :, tok]

    def step(read_ref, write_ref):
        zero = jnp.zeros((PEER_KEYS, LANES), BF16)

        def tile(t, carry):
            a_scr[...] = _dot_nt(u_ref[...], x_scr[t]).astype(BF16)
            acc_scr[t] += _dot(vt_ref[...], read_ref[t])
            for s in range(keys_per_step):
                rows = slice(s * PEER_KEYS, (s + 1) * PEER_KEYS)
                for l in range(PEER_TOK // LANES):
                    lanes = slice(l * LANES, (l + 1) * LANES)
                    w = zero
                    for h in range(PEER_HEADS):
                        cnt = row_scr[t, 0, h, s:s + 1, lanes].astype(BF16)
                        e1 = row_scr[t, 1, h, s:s + 1, lanes].astype(BF16)
                        w = w + jnp.where(r2_scr[t, h, :, lanes] < cnt, g2_scr[t, h, :, lanes], zero) * e1
                    write_ref[t, rows, lanes] = w * _gelu(a_scr[rows, lanes])
            return carry

        lax.fori_loop(0, n_tiles, tile, 0)

    @pl.when(n % 2 == 0)
    def _():
        step(coef_odd, coef_even)

    @pl.when(n % 2 == 1)
    def _():
        step(coef_even, coef_odd)

    prev = n - 1

    @pl.when(jnp.logical_and(n > 0, prev % nj == nj - 1))
    def _():
        ip = prev // nj
        b = ip // blocks_per_batch
        for t in range(n_tiles):
            tok = slice(t * PEER_TOK, (t + 1) * PEER_TOK)
            if n_ctx:
                row = (ip % blocks_per_batch) * tm + t * PEER_TOK + lax.broadcasted_iota(jnp.int32, (PEER_TOK, 1), 0)
                is_ctx = row < n_ctx
            else:
                is_ctx = None
            g2 = _mod_rows(mod_ref, 5, b, is_ctx)
            x = s_ref[tok, :] + g2 * acc_scr[t].T
            if final:
                x = _rms(x, fg_ref[...])
            o_ref[tok, :] = x
        acc_scr[...] = jnp.zeros_like(acc_scr)


def _peer_dense(h2, u_bf16, vt_bf16, sel, stream, mod_l, final_g, rows_per_batch, n_ctx, final):
    n = h2.shape[0]
    r2, g2, cnt, e1 = sel
    tm = 768 if rows_per_batch % 768 == 0 else 512
    te = 1024
    bpb = rows_per_batch // tm
    nj = PEER_EXPERTS // te
    n_steps = (n // tm) * nj

    def cur(s):
        c = jnp.minimum(s, n_steps - 1)
        return c // nj, c % nj

    def prev(s):
        p = jnp.maximum(s - 1, 0)
        return p // nj, p % nj

    tok = pl.BlockSpec((PEER_HEADS, PEER_KEYS, tm), lambda s: (0, 0, cur(s)[0]))
    sub = pl.BlockSpec((PEER_HEADS, F32_SUBLANES, tm),
                       lambda s: (0, cur(s)[1] * (te // PEER_KEYS) // F32_SUBLANES, cur(s)[0]))
    full = lambda a: pl.BlockSpec(a.shape, lambda s: (0,) * a.ndim)
    fg = final_g.reshape(1, D_MODEL)
    n_tiles = tm // PEER_TOK
    coef = pltpu.VMEM((n_tiles, te, PEER_TOK), BF16)
    sel16 = pltpu.VMEM((n_tiles, PEER_HEADS, PEER_KEYS, PEER_TOK), BF16)
    return pl.pallas_call(
        functools.partial(_peer_dense_kernel, te=te, nj=nj, blocks_per_batch=bpb, n_ctx=n_ctx, final=final),
        grid=(n_steps + 1,),
        in_specs=[pl.BlockSpec((tm, D_MODEL), lambda s: (cur(s)[0], 0)),
                  pl.BlockSpec((te, D_MODEL), lambda s: (cur(s)[1], 0)),
                  pl.BlockSpec((D_MODEL, te), lambda s: (0, prev(s)[1])),
                  tok, tok, sub, sub,
                  pl.BlockSpec((tm, D_MODEL), lambda s: (prev(s)[0], 0)), full(mod_l), full(fg)],
        out_specs=pl.BlockSpec((tm, D_MODEL), lambda s: (prev(s)[0], 0)),
        out_shape=jax.ShapeDtypeStruct((n, D_MODEL), F32),
        scratch_shapes=[pltpu.VMEM((n_tiles, D_MODEL, PEER_TOK), F32), pltpu.VMEM((te, PEER_TOK), BF16), coef, coef,
                        pltpu.VMEM((n_tiles, PEER_TOK, D_MODEL), BF16), sel16, sel16,
                        pltpu.VMEM((n_tiles, 2, PEER_HEADS, F32_SUBLANES, PEER_TOK), F32)],
        compiler_params=_cparams(("arbitrary",)),
        name="peer_dense",
    )(h2, u_bf16, vt_bf16, r2, g2, cnt, e1, stream, mod_l, fg)


def _reorder_w_in(w):
    offs = [0]
    for width in (SSM_WIDTH, ATT_WIDTH, KV_WIDTH, KV_WIDTH, ML_WIDTH, ML_WIDTH, ML_WIDTH, ML_WIDTH, ML_GATES,
                  N_BRANCH * D_MODEL):
        offs.append(offs[-1] + width)
    u_s, q_a, k_a, v_a, q_m, k_m, v_m, o_m, g_m, gate = (w[:, offs[n]:offs[n + 1]] for n in range(10))
    pad = jnp.zeros((w.shape[0], PROJ_WIDTH - COL_GM - ML_GATES), w.dtype)
    return jnp.concatenate([u_s, q_a, q_m, k_m, v_m, o_m, gate, k_a, v_a, g_m, pad], axis=1).astype(BF16)


def kernel(x, c, ctx, c_ctx, w_mod, b_mod, norm1_g, norm2_g, w_in, ssm_lam_re, ssm_lam_im, ssm_log_step, ssm_b_re, ssm_b_im, ssm_c_re, ssm_c_im, ssm_d, ssm_w_glu, attn_q_norm_g, attn_k_norm_g, mlstm_gate_b, mlstm_norm_g, w_branch_ssm, w_branch_attn, w_branch_mlstm, w_out, peer_w_q, peer_sub_k1, peer_sub_k2, peer_u, peer_v, final_norm_g):
    bsz, lat_len, d = x.shape
    n_ctx = ctx.shape[1]
    s_tot = n_ctx + lat_len
    depth = w_in.shape[0]
    assert d == D_MODEL and n_ctx == 256 and lat_len % n_ctx == 0 and bsz == 8

    cos, sin = _rope_tables(lat_len)
    mod = _modulation(c, c_ctx, w_mod, b_mod)
    stream = jnp.concatenate([ctx, x], axis=1)

    for layer in range(depth):
        last = layer == depth - 1
        mod_l = mod[layer]
        proj = _inproj(stream.reshape(bsz, s_tot, d), norm1_g[layer], mod_l, _reorder_w_in(w_in[layer]), n_ctx)

        u_tm = proj[:, COL_U:COL_U + SSM_WIDTH].reshape(bsz, s_tot, SSM_WIDTH).transpose(1, 0, 2)
        u_tm = u_tm.reshape(s_tot * bsz, SSM_WIDTH)
        params = _s5_params(ssm_lam_re[layer], ssm_lam_im[layer], ssm_log_step[layer], ssm_b_re[layer],
                            ssm_b_im[layer], ssm_c_re[layer], ssm_c_im[layer])
        yf, yb = _s5_scan(u_tm, params, bsz, s_tot, n_ctx)
        ys_tm = _s5_post(u_tm, yf, yb, ssm_d[layer], ssm_w_glu[layer].astype(BF16))
        ys = ys_tm.reshape(s_tot, bsz, SSM_WIDTH).transpose(1, 0, 2).reshape(bsz * s_tot, SSM_WIDTH)

        qn, kn, vn = _qkprep(proj, cos, sin, attn_q_norm_g[layer], attn_k_norm_g[layer], bsz, s_tot, n_ctx)
        ya = _attention(qn, kn, vn, bsz, s_tot, n_ctx, last)

        hf, hb = _mlstm(proj, mlstm_gate_b[layer], bsz, s_tot, n_ctx)

        stream2, h2 = _merge(ys, ya, hf, hb, proj, stream.reshape(bsz * s_tot, d), mod_l, mlstm_norm_g[layer],
                             norm2_g[layer], w_branch_ssm[layer].astype(BF16), w_branch_attn[layer].astype(BF16),
                             w_branch_mlstm[layer].astype(BF16), w_out[layer].astype(BF16),
                             bsz, s_tot, n_ctx, last)

        s1, s2 = _peer_scores(h2, peer_w_q[layer].T.astype(BF16), peer_sub_k1[layer], peer_sub_k2[layer])
        sel = _peer_select(s1, s2)
        rows_per_batch = lat_len if last else s_tot
        stream = _peer_dense(h2, peer_u[layer].astype(BF16), peer_v[layer].T.astype(BF16), sel, stream2, mod_l,
                             final_norm_g, rows_per_batch, 0 if last else n_ctx, last)

    return stream.reshape(bsz, lat_len, d)
```

```python
import functools
import math

import jax
import jax.numpy as jnp
from jax import lax
from jax.experimental import pallas as pl
from jax.experimental.pallas import tpu as pltpu

F32 = jnp.float32
BF16 = jnp.bfloat16

D_MODEL = 1024
GRID_W = 64
EPS = 1e-6
N_MOD = 6
N_BRANCH = 3
SSM_WIDTH = 512
SSM_GROUP = 16
SSM_GROUPS = 32
SSM_STATE = 64
ATT_HEADS = 8
ATT_KV_HEADS = 2
ATT_HEAD_DIM = 64
ATT_WIDTH = 512
KV_WIDTH = 128
ROPE_FREQS = 16
ROPE_BASE = 10000.0
ML_HEADS = 4
ML_HEAD_DIM = 128
ML_WIDTH = 512
ML_GATES = 16
ML_CHUNK = 128
PEER_HEADS = 8
PEER_KEYS = 128
PEER_EXPERTS = PEER_KEYS * PEER_KEYS
PEER_QDIM = 256
PEER_TOPK = 16
PEER_TOK = 256

LANES = 128
F32_SUBLANES = 8
BF16_SUBLANES = 16
VMEM_LIMIT_BYTES = 56 * 1024 * 1024

PROJ_BLOCK = 512
COL_U, COL_QA, COL_QM, COL_KM, COL_VM, COL_OM = (i * PROJ_BLOCK for i in range(6))
COL_GATE = 6 * PROJ_BLOCK
COL_KA = COL_GATE + N_BRANCH * D_MODEL
COL_VA = COL_KA + KV_WIDTH
COL_GM = COL_VA + KV_WIDTH
PROJ_WIDTH = COL_KA + PROJ_BLOCK

S5_BLOCKS = 4
S5_BLOCK_IN = SSM_WIDTH // S5_BLOCKS
S5_BLOCK_STATE = SSM_GROUPS * SSM_STATE // S5_BLOCKS
S5_CHUNK = 64

NEG_INF = float("-inf")


def _cparams(sem, flags=None):
    return pltpu.CompilerParams(dimension_semantics=sem, vmem_limit_bytes=VMEM_LIMIT_BYTES, flags=flags)


def _split2(x):
    hi = x.astype(BF16)
    lo = (x - hi.astype(F32)).astype(BF16)
    return hi, lo


def _split3(x):
    hi = x.astype(BF16)
    r = x - hi.astype(F32)
    mid = r.astype(BF16)
    lo = (r - mid.astype(F32)).astype(BF16)
    return hi, mid, lo


def _dot(a, b):
    return jnp.dot(a, b, preferred_element_type=F32)


def _dot_nt(a, b):
    return lax.dot_general(a, b, (((1,), (1,)), ((), ())), preferred_element_type=F32)


def _dot_exact_rhs(x, m_bf16):
    hi, mid, lo = _split3(x)
    return _dot(hi, m_bf16) + _dot(mid, m_bf16) + _dot(lo, m_bf16)


def _dot_exact_lhs(m_bf16, x):
    hi, mid, lo = _split3(x)
    return _dot(m_bf16, hi) + _dot(m_bf16, mid) + _dot(m_bf16, lo)


def _sigmoid(x):
    return 1.0 / (1.0 + jnp.exp(-x))


def _gelu(x):
    return jax.nn.gelu(x, approximate=True)


def _mod_kernel(v_ref, w_ref, b_ref, o_ref):
    v = v_ref[...]
    sv = v * _sigmoid(v)
    w = w_ref[0]
    hi, mid, lo = _split3(sv)
    whi, wlo = _split2(w)
    acc = _dot(hi, whi) + _dot(mid, whi) + _dot(hi, wlo) + _dot(lo, whi) + _dot(mid, wlo)
    o_ref[0] = acc + b_ref[0]


def _modulation(c, c_ctx, w_mod, b_mod):
    depth = w_mod.shape[0]
    n_out = w_mod.shape[2]
    rows = 16
    v = jnp.zeros((rows, D_MODEL), F32).at[: c.shape[0]].set(c).at[8].set(c_ctx)
    tn = 1536
    return pl.pallas_call(
        _mod_kernel,
        grid=(depth, n_out // tn),
        in_specs=[pl.BlockSpec((rows, D_MODEL), lambda l, j: (0, 0)),
                  pl.BlockSpec((1, D_MODEL, tn), lambda l, j: (l, 0, j)),
                  pl.BlockSpec((1, 1, tn), lambda l, j: (l, 0, j))],
        out_specs=pl.BlockSpec((1, rows, tn), lambda l, j: (l, 0, j)),
        out_shape=jax.ShapeDtypeStruct((depth, rows, n_out), F32),
        compiler_params=_cparams(("arbitrary", "arbitrary")),
        name="modulation",
    )(v, w_mod, b_mod.reshape(depth, 1, n_out))


CTX_MOD_ROW = 8


def _mod_rows(mod_ref, which, b, is_ctx_col):
    lo = which * D_MODEL
    m_l = mod_ref[pl.ds(b, 1), lo:lo + D_MODEL]
    if is_ctx_col is None:
        return m_l
    m_c = mod_ref[CTX_MOD_ROW:CTX_MOD_ROW + 1, lo:lo + D_MODEL]
    return jnp.where(is_ctx_col, m_c, m_l)


def _rms(x, g):
    ms = jnp.mean(x * x, axis=-1, keepdims=True)
    return x * lax.rsqrt(ms + EPS) * g


def _inproj_kernel(x_ref, g_ref, mod_ref, w_ref, o_ref, h_scr, *, n_ctx, chunk):
    b = pl.program_id(0)
    j = pl.program_id(1)
    s_tot = x_ref.shape[1]

    @pl.when(j == 0)
    def _():
        for r0 in range(0, s_tot, chunk):
            x = x_ref[0, r0:r0 + chunk, :]
            xn = _rms(x, g_ref[...])
            row = r0 + lax.broadcasted_iota(jnp.int32, (chunk, 1), 0)
            is_ctx = row < n_ctx
            sh = _mod_rows(mod_ref, 0, b, is_ctx)
            sc = _mod_rows(mod_ref, 1, b, is_ctx)
            h_scr[r0:r0 + chunk, :] = (xn * (1.0 + sc) + sh).astype(BF16)

    o_ref[...] = _dot(h_scr[...], w_ref[...])


def _inproj(stream3, g, mod_l, w_bf16, n_ctx):
    bsz, s_tot, d = stream3.shape
    n_out = w_bf16.shape[1]
    tn = PROJ_BLOCK
    return pl.pallas_call(
        functools.partial(_inproj_kernel, n_ctx=n_ctx, chunk=256),
        grid=(bsz, n_out // tn),
        in_specs=[pl.BlockSpec((1, s_tot, d), lambda b, j: (b, 0, 0)),
                  pl.BlockSpec((1, d), lambda b, j: (0, 0)),
                  pl.BlockSpec(mod_l.shape, lambda b, j: (0, 0)),
                  pl.BlockSpec((d, tn), lambda b, j: (0, j))],
        out_specs=pl.BlockSpec((s_tot, tn), lambda b, j: (b, j)),
        out_shape=jax.ShapeDtypeStruct((bsz * s_tot, n_out), F32),
        scratch_shapes=[pltpu.VMEM((s_tot, d), BF16)],
        compiler_params=_cparams(("arbitrary", "arbitrary")),
        name="inproj",
    )(stream3, g.reshape(1, d), mod_l, w_bf16)


def _s5_param_kernel(lre_ref, lim_ref, ls_ref, bre_ref, bim_ref, are_ref, aim_ref, bbre_ref, bbim_ref):
    lre = lre_ref[...]
    lim = lim_ref[...]
    step = jnp.exp(ls_ref[...])
    mag = jnp.exp(lre * step)
    a_re = mag * jnp.cos(lim * step)
    a_im = mag * jnp.sin(lim * step)
    den = lre * lre + lim * lim
    z_re = ((a_re - 1.0) * lre + a_im * lim) / den
    z_im = (a_im * lre - (a_re - 1.0) * lim) / den
    b_re = bre_ref[...]
    b_im = bim_ref[...]
    are_ref[...] = a_re
    aim_ref[...] = a_im
    bbre_ref[...] = z_re * b_re - z_im * b_im
    bbim_ref[...] = z_re * b_im + z_im * b_re


def _s5_params(lam_re, lam_im, log_step, b_re, b_im, c_re, c_im):
    nd, g, n = lam_re.shape
    c = b_re.shape[-1]
    rows = nd * g
    wide = n * c

    def expand(z):
        return jnp.broadcast_to(z.reshape(rows, n, 1), (rows, n, c)).reshape(rows, wide)

    ls = jnp.broadcast_to(log_step.reshape(rows, 1), (rows, wide))
    spec = pl.BlockSpec((rows, wide), lambda: (0, 0))
    shp = jax.ShapeDtypeStruct((rows, wide), F32)
    a_re, a_im, bb_re, bb_im = pl.pallas_call(
        _s5_param_kernel,
        in_specs=[spec] * 5, out_specs=[spec] * 4, out_shape=[shp] * 4,
        name="s5_params",
    )(expand(lam_re), expand(lam_im), ls, b_re.reshape(rows, wide), b_im.reshape(rows, wide))

    gpb = g // S5_BLOCKS
    eye = jnp.eye(gpb, dtype=F32)

    def diag_in(bb):
        bb = bb.reshape(nd, S5_BLOCKS, gpb, n, c)
        return jnp.einsum("dkgnc,gh->dkgchn", bb, eye).reshape(nd, S5_BLOCKS, gpb * c, gpb * n).astype(BF16)

    def diag_out(cc):
        cc = cc.reshape(nd, S5_BLOCKS, gpb, c, n)
        return jnp.einsum("dkgcn,gh->dkgnhc", cc, eye).reshape(nd, S5_BLOCKS, gpb * n, gpb * c).astype(BF16)

    def decay(a):
        a = a.reshape(nd, S5_BLOCKS, gpb, n, c)[..., 0].reshape(nd, S5_BLOCKS, 1, gpb * n)
        return jnp.broadcast_to(a, (nd, S5_BLOCKS, 8, gpb * n))

    return decay(a_re), decay(a_im), diag_in(bb_re), diag_in(bb_im), diag_out(c_re), diag_out(c_im)


def _s5_kernel(uf_ref, ub_ref, are_ref, aim_ref, bre_ref, bim_ref, cre_ref, cim_ref,
               yf_ref, yb_ref, bur_scr, bui_scr, st_scr, *, steps, bsz):
    i = pl.program_id(0)

    @pl.when(i == 0)
    def _():
        st_scr[...] = jnp.zeros_like(st_scr)

    for d in range(2):
        u_ref = (uf_ref, ub_ref)[d]
        y_ref = (yf_ref, yb_ref)[d]
        for k in range(S5_BLOCKS):
            u = u_ref[:, k * S5_BLOCK_IN:(k + 1) * S5_BLOCK_IN].astype(BF16)
            bur_scr[...] = _dot(u, bre_ref[d, k])
            bui_scr[...] = _dot(u, bim_ref[d, k])
            ar = are_ref[d, k]
            ai = aim_ref[d, k]

            def step(t, carry, d=d, ar=ar, ai=ai):
                sr, si = carry
                tt = t if d == 0 else steps - 1 - t
                r0 = pl.multiple_of(tt * bsz, bsz)
                nr = ar * sr - ai * si + bur_scr[pl.ds(r0, bsz), :]
                ni = ar * si + ai * sr + bui_scr[pl.ds(r0, bsz), :]
                bur_scr[pl.ds(r0, bsz), :] = nr
                bui_scr[pl.ds(r0, bsz), :] = ni
                return nr, ni

            sr, si = lax.fori_loop(0, steps, step, (st_scr[d, k, 0], st_scr[d, k, 1]), unroll=8)
            st_scr[d, k, 0] = sr
            st_scr[d, k, 1] = si
            y = _dot(bur_scr[...].astype(BF16), cre_ref[d, k]) - _dot(bui_scr[...].astype(BF16), cim_ref[d, k])
            y_ref[:, k * S5_BLOCK_IN:(k + 1) * S5_BLOCK_IN] = y


def _s5_scan(u_tm, params, bsz, s_tot, n_ctx):
    a_re, a_im, bb_re, bb_im, cc_re, cc_im = params
    assert bsz == 8
    rows = S5_CHUNK * bsz
    n_chunks = s_tot // S5_CHUNK
    ctx_chunks = n_ctx // S5_CHUNK

    def bwd_chunk(i):
        return jnp.where(i < ctx_chunks, ctx_chunks - 1 - i, n_chunks - 1 + ctx_chunks - i)

    full = lambda a: pl.BlockSpec(a.shape, lambda i: (0,) * a.ndim)
    shp = jax.ShapeDtypeStruct((s_tot * bsz, SSM_WIDTH), F32)
    return pl.pallas_call(
        functools.partial(_s5_kernel, steps=S5_CHUNK, bsz=bsz),
        grid=(n_chunks,),
        in_specs=[pl.BlockSpec((rows, SSM_WIDTH), lambda i: (i, 0)),
                  pl.BlockSpec((rows, SSM_WIDTH), lambda i: (bwd_chunk(i), 0)),
                  full(a_re), full(a_im), full(bb_re), full(bb_im), full(cc_re), full(cc_im)],
        out_specs=[pl.BlockSpec((rows, SSM_WIDTH), lambda i: (i, 0)),
                   pl.BlockSpec((rows, SSM_WIDTH), lambda i: (bwd_chunk(i), 0))],
        out_shape=[shp, shp],
        scratch_shapes=[pltpu.VMEM((rows, S5_BLOCK_STATE), F32), pltpu.VMEM((rows, S5_BLOCK_STATE), F32),
                        pltpu.VMEM((2, S5_BLOCKS, 2, bsz, S5_BLOCK_STATE), F32)],
        compiler_params=_cparams(("arbitrary",)),
        name="s5_scan",
    )(u_tm, u_tm, a_re, a_im, bb_re, bb_im, cc_re, cc_im)


def _s5_post_kernel(u_ref, yf_ref, yb_ref, d_ref, w_ref, o_ref):
    y = d_ref[...] * u_ref[...] + yf_ref[...] + yb_ref[...]
    g = _gelu(y).astype(BF16)
    gate = _sigmoid(_dot(g, w_ref[...]))
    o_ref[...] = (g.astype(F32) * gate).astype(BF16)


def _s5_post(u_tm, yf, yb, d_skip, w_glu_bf16):
    n, w = u_tm.shape
    tm = 1024
    row = pl.BlockSpec((tm, w), lambda i: (i, 0))
    return pl.pallas_call(
        _s5_post_kernel,
        grid=(n // tm,),
        in_specs=[row, row, row, pl.BlockSpec((1, w), lambda i: (0, 0)), pl.BlockSpec((w, w), lambda i: (0, 0))],
        out_specs=row,
        out_shape=jax.ShapeDtypeStruct((n, w), BF16),
        compiler_params=_cparams(("arbitrary",)),
        name="s5_post",
    )(u_tm, yf, yb, d_skip.reshape(1, w), w_glu_bf16)


def _rope_tables(lat_len):
    rows = lat_len // GRID_W
    row = jnp.repeat(jnp.arange(rows, dtype=F32), GRID_W)
    col = jnp.tile(jnp.arange(GRID_W, dtype=F32), rows)
    inv = ROPE_BASE ** (-jnp.arange(ROPE_FREQS, dtype=F32) / ROPE_FREQS)
    ang_r = row[:, None] * inv
    ang_c = col[:, None] * inv
    cos = jnp.concatenate([jnp.cos(ang_r), jnp.cos(ang_r), jnp.cos(ang_c), jnp.cos(ang_c)], axis=1)
    sin = jnp.concatenate([-jnp.sin(ang_r), jnp.sin(ang_r), -jnp.sin(ang_c), jnp.sin(ang_c)], axis=1)
    return jnp.tile(cos, (1, 2)), jnp.tile(sin, (1, 2))


def _head_rms(x, ones_bd, g):
    hi, lo = _split2(x * x)
    ms = (_dot(hi, ones_bd) + _dot(lo, ones_bd)) * (1.0 / ATT_HEAD_DIM)
    return x * lax.rsqrt(ms + EPS) * g


def _rope(x, cos, sin_signed):
    lane = lax.broadcasted_iota(jnp.int32, x.shape, 1)
    first_half = (lane % (2 * ROPE_FREQS)) < ROPE_FREQS
    partner = jnp.where(first_half, pltpu.roll(x, LANES - ROPE_FREQS, 1), pltpu.roll(x, ROPE_FREQS, 1))
    return x * cos + partner * sin_signed


def _qkprep_kernel(q_ref, kv_ref, cos_ref, sin_ref, qg_ref, kg_ref, qo_ref, ko_ref, vo_ref, *, n_ctx, chunk):
    s_tot = q_ref.shape[0]
    r_i = lax.broadcasted_iota(jnp.int32, (LANES, LANES), 0) // ATT_HEAD_DIM
    c_i = lax.broadcasted_iota(jnp.int32, (LANES, LANES), 1) // ATT_HEAD_DIM
    ones_bd = jnp.where(r_i == c_i, 1.0, 0.0).astype(BF16)
    lane = lax.broadcasted_iota(jnp.int32, (chunk, LANES), 1)
    low = lane < ATT_HEAD_DIM
    for r0 in range(0, s_tot, chunk):
        roped = r0 >= n_ctx
        if roped:
            cos = cos_ref[r0 - n_ctx:r0 - n_ctx + chunk, :]
            sin = sin_ref[r0 - n_ctx:r0 - n_ctx + chunk, :]
        for s in range(ATT_WIDTH // LANES):
            x = _head_rms(q_ref[r0:r0 + chunk, s * LANES:(s + 1) * LANES], ones_bd, qg_ref[...])
            if roped:
                x = _rope(x, cos, sin)
            qo_ref[r0:r0 + chunk, s * LANES:(s + 1) * LANES] = (x * (ATT_HEAD_DIM ** -0.5)).astype(BF16)
        k = _head_rms(kv_ref[r0:r0 + chunk, 0:LANES], ones_bd, kg_ref[...])
        if roped:
            k = _rope(k, cos, sin)
        v = kv_ref[r0:r0 + chunk, LANES:2 * LANES]
        k_sw = pltpu.roll(k, ATT_HEAD_DIM, 1)
        v_sw = pltpu.roll(v, ATT_HEAD_DIM, 1)
        zero = jnp.zeros_like(k)
        ks = (jnp.where(low, k, zero), jnp.where(low, zero, k_sw), jnp.where(low, k_sw, zero), jnp.where(low, zero, k))
        vs = (jnp.where(low, v, zero), jnp.where(low, zero, v_sw), jnp.where(low, v_sw, zero), jnp.where(low, zero, v))
        for n in range(4):
            ko_ref[0, n, r0:r0 + chunk, :] = ks[n].astype(BF16)
            vo_ref[0, n, r0:r0 + chunk, :] = vs[n].astype(BF16)


def _qkprep(proj, cos, sin, q_g, k_g, bsz, s_tot, n_ctx):
    qg = jnp.tile(q_g, 2).reshape(1, LANES)
    kg = jnp.tile(k_g, 2).reshape(1, LANES)
    kv_shape = jax.ShapeDtypeStruct((bsz, 4, s_tot, LANES), BF16)
    kv_spec = pl.BlockSpec((1, 4, s_tot, LANES), lambda b: (b, 0, 0, 0))
    return pl.pallas_call(
        functools.partial(_qkprep_kernel, n_ctx=n_ctx, chunk=256),
        grid=(bsz,),
        in_specs=[pl.BlockSpec((s_tot, PROJ_BLOCK), lambda b: (b, COL_QA // PROJ_BLOCK)),
                  pl.BlockSpec((s_tot, PROJ_BLOCK), lambda b: (b, COL_KA // PROJ_BLOCK)),
                  pl.BlockSpec(cos.shape, lambda b: (0, 0)), pl.BlockSpec(sin.shape, lambda b: (0, 0)),
                  pl.BlockSpec((1, LANES), lambda b: (0, 0)), pl.BlockSpec((1, LANES), lambda b: (0, 0))],
        out_specs=[pl.BlockSpec((s_tot, ATT_WIDTH), lambda b: (b, 0)), kv_spec, kv_spec],
        out_shape=[jax.ShapeDtypeStruct((bsz * s_tot, ATT_WIDTH), BF16), kv_shape, kv_shape],
        compiler_params=_cparams(("arbitrary",)),
        name="qk_prep",
    )(proj, proj, cos, sin, qg, kg)


def _attend(q_ref, k_ref, v_ref, o_ref, n_keys):
    tq = q_ref.shape[0]
    for hk in range(ATT_KV_HEADS):
        qs = jnp.concatenate([q_ref[:, (2 * hk) * LANES:(2 * hk + 1) * LANES],
                              q_ref[:, (2 * hk + 1) * LANES:(2 * hk + 2) * LANES]], axis=0)
        acc = jnp.zeros((2 * tq, LANES), F32)
        for p in range(2):
            s = _dot_nt(qs, k_ref[0, 2 * hk + p, 0:n_keys, :])
            m = jnp.max(s, axis=-1, keepdims=True)
            e = jnp.exp(s - m)
            l = jnp.sum(e, axis=-1, keepdims=True)
            acc = acc + _dot(e.astype(BF16), v_ref[0, 2 * hk + p, 0:n_keys, :]) / l
        o_ref[:, (2 * hk) * LANES:(2 * hk + 1) * LANES] = acc[0:tq].astype(BF16)
        o_ref[:, (2 * hk + 1) * LANES:(2 * hk + 2) * LANES] = acc[tq:2 * tq].astype(BF16)


def _attn_kernel(q_ref, k_ref, v_ref, o_ref, *, n_ctx, first_block):
    qi = pl.program_id(1) + first_block
    s_tot = k_ref.shape[2]
    if first_block == 0:
        @pl.when(qi == 0)
        def _():
            _attend(q_ref, k_ref, v_ref, o_ref, n_ctx)

        @pl.when(qi > 0)
        def _():
            _attend(q_ref, k_ref, v_ref, o_ref, s_tot)
    else:
        _attend(q_ref, k_ref, v_ref, o_ref, s_tot)


def _attention(qn, kn, vn, bsz, s_tot, n_ctx, latent_only):
    tq = n_ctx
    blocks = s_tot // tq
    first = 1 if latent_only else 0
    kv_spec = pl.BlockSpec((1, 4, s_tot, LANES), lambda b, i: (b, 0, 0, 0))
    return pl.pallas_call(
        functools.partial(_attn_kernel, n_ctx=n_ctx, first_block=first),
        grid=(bsz, blocks - first),
        in_specs=[pl.BlockSpec((tq, ATT_WIDTH), lambda b, i: (b * blocks + i + first, 0)), kv_spec, kv_spec],
        out_specs=pl.BlockSpec((tq, ATT_WIDTH), lambda b, i: (b * (blocks - first) + i, 0)),
        out_shape=jax.ShapeDtypeStruct((bsz * (blocks - first) * tq, ATT_WIDTH), BF16),
        compiler_params=_cparams(("arbitrary", "arbitrary")),
        name="attention",
    )(qn, kn, vn)


def _log_sigmoid(x):
    return jnp.minimum(x, 0.0) - jnp.log(1.0 + jnp.exp(-jnp.abs(x)))


def _mlstm_kernel(qf_ref, kf_ref, vf_ref, gf_ref, qb_ref, kb_ref, vb_ref, gb_ref, bias_ref,
                  hf_ref, hb_ref, c_scr, n_scr, m_scr):
    i = pl.program_id(1)
    t = ML_CHUNK

    @pl.when(i == 0)
    def _():
        c_scr[...] = jnp.zeros_like(c_scr)
        n_scr[...] = jnp.zeros_like(n_scr)
        m_scr[...] = jnp.zeros_like(m_scr)

    r_i = lax.broadcasted_iota(jnp.int32, (t, t), 0)
    c_i = lax.broadcasted_iota(jnp.int32, (t, t), 1)
    lower = r_i >= c_i
    upper = r_i <= c_i
    lower_m = jnp.where(lower, 1.0, 0.0).astype(BF16)
    upper_m = jnp.where(upper, 1.0, 0.0).astype(BF16)

    for d in range(2):
        q_ref, k_ref, v_ref, g_ref, h_ref = ((qf_ref, kf_ref, vf_ref, gf_ref, hf_ref),
                                             (qb_ref, kb_ref, vb_ref, gb_ref, hb_ref))[d]
        g = g_ref[...] + bias_ref[...]
        g_t = g.T
        lf = _log_sigmoid(g)
        lf_t = _log_sigmoid(g_t)
        causal, causal_m, anti_m = (lower, lower_m, upper_m) if d == 0 else (upper, upper_m, lower_m)
        b_cols = _dot_exact_lhs(causal_m, lf)
        b_rows = _dot_exact_rhs(lf_t, anti_m)
        last = t - 1 if d == 0 else 0
        for h in range(ML_HEADS):
            ci = d * 2 * ML_HEADS + h
            cf = ci + ML_HEADS
            i_col = g[:, ci:ci + 1]
            i_row = g_t[ci:ci + 1, :]
            b_col = b_cols[:, cf:cf + 1]
            b_row = b_rows[cf:cf + 1, :]
            b_last = b_cols[last:last + 1, cf:cf + 1]
            m_prev = m_scr[d, h]
            c_prev = c_scr[d, h]
            n_prev = n_scr[d, h]

            q = q_ref[:, h * ML_HEAD_DIM:(h + 1) * ML_HEAD_DIM]
            k = k_ref[:, h * ML_HEAD_DIM:(h + 1) * ML_HEAD_DIM] * (ML_HEAD_DIM ** -0.5)
            v = v_ref[:, h * ML_HEAD_DIM:(h + 1) * ML_HEAD_DIM]
            q16, k16, v16 = q.astype(BF16), k.astype(BF16), v.astype(BF16)

            dmat = jnp.where(causal, b_col - b_row + i_row, NEG_INF)
            inter = b_col + m_prev
            m_t = jnp.maximum(inter, jnp.max(dmat, axis=-1, keepdims=True))
            w = jnp.exp(dmat - m_t)
            a_inter = jnp.exp(inter - m_t)
            qk = _dot_nt(q16, k16) * w
            num = a_inter * _dot(q16, c_prev.astype(BF16)) + _dot(qk.astype(BF16), v16)
            den = a_inter * jnp.sum(q * n_prev, axis=-1, keepdims=True) + jnp.sum(qk, axis=-1, keepdims=True)
            h_ref[:, h * ML_HEAD_DIM:(h + 1) * ML_HEAD_DIM] = num / jnp.maximum(jnp.abs(den), jnp.exp(-m_t))

            d_last_col = b_last - b_col + i_col
            m_new = jnp.maximum(b_last + m_prev, jnp.max(d_last_col, axis=0, keepdims=True))
            w_last = jnp.exp(d_last_col - m_new)
            decay = jnp.exp(b_last + m_prev - m_new)
            kw = k * w_last
            c_scr[d, h] = decay * c_prev + _dot(kw.T.astype(BF16), v16)
            n_scr[d, h] = decay * n_prev + jnp.sum(kw, axis=0, keepdims=True)
            m_scr[d, h] = m_new


def _mlstm(proj, gate_b, bsz, s_tot, n_ctx):
    t = ML_CHUNK
    n_chunks = s_tot // t
    ctx_chunks = n_ctx // t

    def fwd(b, i):
        return b * n_chunks + i

    def bwd(b, i):
        return b * n_chunks + jnp.where(i < ctx_chunks, ctx_chunks - 1 - i, n_chunks - 1 + ctx_chunks - i)

    def col(c0, width):
        return c0 // width

    def specs(rowfn):
        return [pl.BlockSpec((t, ML_WIDTH), lambda b, i: (rowfn(b, i), col(COL_QM, ML_WIDTH))),
                pl.BlockSpec((t, ML_WIDTH), lambda b, i: (rowfn(b, i), col(COL_KM, ML_WIDTH))),
                pl.BlockSpec((t, ML_WIDTH), lambda b, i: (rowfn(b, i), col(COL_VM, ML_WIDTH))),
                pl.BlockSpec((t, LANES), lambda b, i: (rowfn(b, i), col(COL_GM, LANES)))]

    bias = jnp.zeros((1, LANES), F32).at[0, :ML_GATES].set(gate_b.reshape(ML_GATES))
    shp = jax.ShapeDtypeStruct((bsz * s_tot, ML_WIDTH), F32)
    return pl.pallas_call(
        _mlstm_kernel,
        grid=(bsz, n_chunks),
        in_specs=specs(fwd) + specs(bwd) + [pl.BlockSpec((1, LANES), lambda b, i: (0, 0))],
        out_specs=[pl.BlockSpec((t, ML_WIDTH), lambda b, i: (fwd(b, i), 0)),
                   pl.BlockSpec((t, ML_WIDTH), lambda b, i: (bwd(b, i), 0))],
        out_shape=[shp, shp],
        scratch_shapes=[pltpu.VMEM((2, ML_HEADS, ML_HEAD_DIM, ML_HEAD_DIM), F32),
                        pltpu.VMEM((2, ML_HEADS, 1, ML_HEAD_DIM), F32),
                        pltpu.VMEM((2, ML_HEADS, 1, 1), F32)],
        compiler_params=_cparams(("arbitrary", "arbitrary")),
        name="mlstm",
    )(*([proj] * 8), bias)


def _merge_kernel(ys_ref, ya_ref, hf_ref, hb_ref, om_ref, gl_ref, x_ref, mod_ref, mg_ref, n2_ref,
                  wbs_ref, wba_ref, wbm_ref, wo_ref, xo_ref, h2_ref, *, blocks_per_batch, first_block):
    n = pl.program_id(0)
    b = n // (blocks_per_batch - first_block)
    if not first_block:
        b = jnp.where(n % blocks_per_batch == 0, CTX_MOD_ROW, b)
    is_ctx = None

    hs = hf_ref[...] + hb_ref[...]
    parts = []
    for h in range(ML_HEADS):
        sl = slice(h * ML_HEAD_DIM, (h + 1) * ML_HEAD_DIM)
        parts.append(_rms(hs[:, sl], mg_ref[:, sl]))
    hn = jnp.concatenate(parts, axis=1) * _sigmoid(om_ref[...])

    gl = gl_ref[...]
    merged = (_sigmoid(gl[:, 0:D_MODEL]) * _dot(ys_ref[...], wbs_ref[...])
              + _sigmoid(gl[:, D_MODEL:2 * D_MODEL]) * _dot(ya_ref[...], wba_ref[...])
              + _sigmoid(gl[:, 2 * D_MODEL:3 * D_MODEL]) * _dot(hn.astype(BF16), wbm_ref[...]))
    mix = _dot(merged.astype(BF16), wo_ref[...])
    g1 = _mod_rows(mod_ref, 2, b, is_ctx)
    x = x_ref[...] + g1 * mix
    xo_ref[...] = x
    sh2 = _mod_rows(mod_ref, 3, b, is_ctx)
    sc2 = _mod_rows(mod_ref, 4, b, is_ctx)
    h2_ref[...] = (_rms(x, n2_ref[...]) * (1.0 + sc2) + sh2).astype(BF16)


def _merge(ys, ya, hf, hb, proj, stream, mod_l, ml_norm_g, norm2_g, wbs, wba, wbm, wo,
           bsz, s_tot, n_ctx, latent_only):
    tm = n_ctx
    bpb = s_tot // tm
    first = 1 if latent_only else 0
    per = bpb - first

    def rows(n):
        return (n // per) * bpb + n % per + first

    n_blocks = bsz * per
    full = lambda a: pl.BlockSpec(a.shape, lambda n: (0,) * a.ndim)
    wide = lambda w: pl.BlockSpec((tm, w), lambda n: (rows(n), 0))
    mg = ml_norm_g.reshape(1, ML_WIDTH)
    n2 = norm2_g.reshape(1, D_MODEL)
    return pl.pallas_call(
        functools.partial(_merge_kernel, blocks_per_batch=bpb, first_block=first),
        grid=(n_blocks,),
        in_specs=[wide(SSM_WIDTH), pl.BlockSpec((tm, ATT_WIDTH), lambda n: (n, 0)), wide(ML_WIDTH), wide(ML_WIDTH),
                  pl.BlockSpec((tm, ML_WIDTH), lambda n: (rows(n), COL_OM // ML_WIDTH)),
                  pl.BlockSpec((tm, N_BRANCH * D_MODEL), lambda n: (rows(n), COL_GATE // (N_BRANCH * D_MODEL))),
                  wide(D_MODEL), full(mod_l), full(mg), full(n2), full(wbs), full(wba), full(wbm), full(wo)],
        out_specs=[pl.BlockSpec((tm, D_MODEL), lambda n: (n, 0)), pl.BlockSpec((tm, D_MODEL), lambda n: (n, 0))],
        out_shape=[jax.ShapeDtypeStruct((n_blocks * tm, D_MODEL), F32),
                   jax.ShapeDtypeStruct((n_blocks * tm, D_MODEL), BF16)],
        compiler_params=_cparams(("arbitrary",)),
        name="merge",
    )(ys, ya, hf, hb, proj, proj, stream, mod_l, mg, n2, wbs, wba, wbm, wo)


def _peer_score_kernel(x_ref, wq_ref, k1_ref, k2_ref, s1_ref, s2_ref):
    half = PEER_QDIM // 2
    q_t = _dot_nt(wq_ref[...], x_ref[...])
    for key_ref, s_ref, lo in ((k1_ref, s1_ref, 0), (k2_ref, s2_ref, half)):
        khi, klo = _split2(key_ref[...])
        qhi, qlo = _split2(q_t[lo:lo + half, :])
        s_ref[0] = _dot(khi, qhi) + _dot(khi, qlo) + _dot(klo, qhi)


def _peer_scores(h2, wq_t, k1, k2):
    n = h2.shape[0]
    tm = 1024
    shp = jax.ShapeDtypeStruct((PEER_HEADS, PEER_KEYS, n), F32)
    out = pl.BlockSpec((1, PEER_KEYS, tm), lambda i, h: (h, 0, i))
    key = pl.BlockSpec((PEER_KEYS, PEER_QDIM // 2), lambda i, h: (0, 0))
    return pl.pallas_call(
        _peer_score_kernel,
        grid=(n // tm, PEER_HEADS),
        in_specs=[pl.BlockSpec((tm, D_MODEL), lambda i, h: (i, 0)),
                  pl.BlockSpec((PEER_QDIM, D_MODEL), lambda i, h: (h, 0)), key, key],
        out_specs=[out, out],
        out_shape=[shp, shp],
        compiler_params=_cparams(("arbitrary", "arbitrary")),
        name="peer_scores",
    )(h2, wq_t, k1, k2)


def _top_values(s, k):
    work = s
    rank = jnp.full(s.shape, float(PEER_KEYS), F32)
    vals = []
    for r in range(k):
        m = jnp.max(work, axis=0, keepdims=True)
        hit = work == m
        rank = jnp.where(hit, float(r), rank)
        work = jnp.where(hit, NEG_INF, work)
        vals.append(m)
    return vals, rank


def _peer_select_kernel(s1_ref, s2_ref, r2_ref, g2_ref, cnt_ref, e1_ref, v1_scr, v2_scr):
    k = PEER_TOPK
    half = k // 2
    s1 = s1_ref[0]
    s2 = s2_ref[0]
    v1, rank1 = _top_values(s1, k)
    v2, rank2 = _top_values(s2, k)
    for a in range(k):
        v1_scr[a:a + 1, :] = v1[a]
        v2_scr[a:a + 1, :] = v2[a]
    row = lax.broadcasted_iota(jnp.int32, (half, s1.shape[1]), 0)
    pieces = [v1[0] + v2_scr[...]]
    for a in range(1, half):
        pieces.append(jnp.where(row < k // (a + 1), v1[a] + v2_scr[0:half, :], NEG_INF))
    pieces.append(v1_scr[half:k, :] + v2[0])
    top = v1[0] + v2[0]
    z = jnp.zeros_like(top)
    thr = top
    for _ in range(k):
        thr = functools.reduce(jnp.maximum, [jnp.max(p, axis=0, keepdims=True) for p in pieces])
        pieces = [jnp.where(p == thr, NEG_INF, p) for p in pieces]
        z = z + jnp.exp(thr - top)
    cnt = jnp.zeros_like(s1)
    for a in range(k):
        if a == 0:
            sums = v1[0] + v2_scr[...]
        elif a < half:
            sums = jnp.where(row < k // (a + 1), v1[a] + v2_scr[0:half, :], NEG_INF)
        else:
            sums = v1[a] + v2[0]
        n_sel = jnp.sum(jnp.where(sums >= thr, 1.0, 0.0), axis=0, keepdims=True)
        cnt = jnp.where(rank1 == float(a), n_sel, cnt)
    r2_ref[0] = rank2.astype(BF16)
    g2_ref[0] = jnp.exp(s2 - v2[0]).astype(BF16)
    cnt_ref[0] = cnt
    e1_ref[0] = jnp.exp(s1 - v1[0]) / z


def _peer_select(s1, s2):
    heads, keys, n = s1.shape
    tl = 512
    spec = pl.BlockSpec((1, keys, tl), lambda h, i: (h, 0, i))
    shp = jax.ShapeDtypeStruct((heads, keys, n), F32)
    shp16 = jax.ShapeDtypeStruct((heads, keys, n), BF16)
    return pl.pallas_call(
        _peer_select_kernel,
        grid=(heads, n // tl),
        in_specs=[spec, spec], out_specs=[spec] * 4, out_shape=[shp16, shp16, shp, shp],
        scratch_shapes=[pltpu.VMEM((PEER_TOPK, tl), F32), pltpu.VMEM((PEER_TOPK, tl), F32)],
        compiler_params=_cparams(("arbitrary", "arbitrary")),
        name="peer_select",
    )(s1, s2)


def _peer_dense_kernel(x_ref, u_ref, vt_ref, r2_ref, g2_ref, cnt_ref, e1_ref, s_ref, mod_ref, fg_ref,
                       o_ref, acc_scr, a_scr, coef_even, coef_odd, x_scr, r2_scr, g2_scr, row_scr,
                       *, te, nj, blocks_per_batch, n_ctx, final):
    n = pl.program_id(0)
    tm = x_ref.shape[1]
    n_tiles = tm // PEER_TOK
    keys_per_step = te // PEER_KEYS
    assert keys_per_step == F32_SUBLANES

    @pl.when(n == 0)
    def _():
        acc_scr[...] = jnp.zeros_like(acc_scr)
        coef_odd[...] = jnp.zeros_like(coef_odd)

    @pl.when(n % nj == 0)
    def _():
        for t in range(n_tiles):
            tok = slice(t * PEER_TOK, (t + 1) * PEER_TOK)
            x_scr[t] = x_ref[:, tok]
            r2_scr[t] = r2_ref[:, :, tok]
            g2_scr[t] = g2_ref[:, :, tok]

    for t in range(n_tiles):
        tok = slice(t * PEER_TOK, (t + 1) * PEER_TOK)
        row_scr[t, 0] = cnt_ref[:, :, tok]
        row_scr[t, 1] = e1_ref[:, :, tok]

    def step(read_ref, write_ref):
        zero = jnp.zeros((PEER_KEYS, LANES), BF16)

        def tile(t, carry):
            a_scr[...] = _dot(u_ref[...], x_scr[t]).astype(BF16)
            acc_scr[t] += _dot(vt_ref[...], read_ref[t])
            for s in range(keys_per_step):
                rows = slice(s * PEER_KEYS, (s + 1) * PEER_KEYS)
                for l in range(PEER_TOK // LANES):
                    lanes = slice(l * LANES, (l + 1) * LANES)
                    w = zero
                    for h in range(PEER_HEADS):
                        cnt = row_scr[t, 0, h, s:s + 1, lanes].astype(BF16)
                        e1 = row_scr[t, 1, h, s:s + 1, lanes].astype(BF16)
                        w = w + jnp.where(r2_scr[t, h, :, lanes] < cnt, g2_scr[t, h, :, lanes], zero) * e1
                    write_ref[t, rows, lanes] = w * _gelu(a_scr[rows, lanes])
            return carry

        lax.fori_loop(0, n_tiles, tile, 0)

    @pl.when(n % 2 == 0)
    def _():
        step(coef_odd, coef_even)

    @pl.when(n % 2 == 1)
    def _():
        step(coef_even, coef_odd)

    prev = n - 1

    @pl.when(jnp.logical_and(n > 0, prev % nj == nj - 1))
    def _():
        ip = prev // nj
        b = ip // blocks_per_batch
        for t in range(n_tiles):
            tok = slice(t * PEER_TOK, (t + 1) * PEER_TOK)
            if n_ctx:
                row = (ip % blocks_per_batch) * tm + t * PEER_TOK + lax.broadcasted_iota(jnp.int32, (PEER_TOK, 1), 0)
                is_ctx = row < n_ctx
            else:
                is_ctx = None
            g2 = _mod_rows(mod_ref, 5, b, is_ctx)
            x = s_ref[tok, :] + g2 * acc_scr[t].T
            if final:
                x = _rms(x, fg_ref[...])
            o_ref[tok, :] = x
        acc_scr[...] = jnp.zeros_like(acc_scr)


def _peer_dense(h2, u_bf16, vt_bf16, sel, stream, mod_l, final_g, rows_per_batch, n_ctx, final):
    n = h2.shape[0]
    r2, g2, cnt, e1 = sel
    tm = 768 if rows_per_batch % 768 == 0 else 512
    te = 1024
    bpb = rows_per_batch // tm
    nj = PEER_EXPERTS // te
    n_steps = (n // tm) * nj

    def cur(s):
        c = jnp.minimum(s, n_steps - 1)
        return c // nj, c % nj

    def prev(s):
        p = jnp.maximum(s - 1, 0)
        return p // nj, p % nj

    tok = pl.BlockSpec((PEER_HEADS, PEER_KEYS, tm), lambda s: (0, 0, cur(s)[0]))
    sub = pl.BlockSpec((PEER_HEADS, F32_SUBLANES, tm),
                       lambda s: (0, cur(s)[1] * (te // PEER_KEYS) // F32_SUBLANES, cur(s)[0]))
    full = lambda a: pl.BlockSpec(a.shape, lambda s: (0,) * a.ndim)
    fg = final_g.reshape(1, D_MODEL)
    n_tiles = tm // PEER_TOK
    coef = pltpu.VMEM((n_tiles, te, PEER_TOK), BF16)
    sel16 = pltpu.VMEM((n_tiles, PEER_HEADS, PEER_KEYS, PEER_TOK), BF16)
    return pl.pallas_call(
        functools.partial(_peer_dense_kernel, te=te, nj=nj, blocks_per_batch=bpb, n_ctx=n_ctx, final=final),
        grid=(n_steps + 1,),
        in_specs=[pl.BlockSpec((D_MODEL, tm), lambda s: (0, cur(s)[0])),
                  pl.BlockSpec((te, D_MODEL), lambda s: (cur(s)[1], 0)),
                  pl.BlockSpec((D_MODEL, te), lambda s: (0, prev(s)[1])),
                  tok, tok, sub, sub,
                  pl.BlockSpec((tm, D_MODEL), lambda s: (prev(s)[0], 0)), full(mod_l), full(fg)],
        out_specs=pl.BlockSpec((tm, D_MODEL), lambda s: (prev(s)[0], 0)),
        out_shape=jax.ShapeDtypeStruct((n, D_MODEL), F32),
        scratch_shapes=[pltpu.VMEM((n_tiles, D_MODEL, PEER_TOK), F32), pltpu.VMEM((te, PEER_TOK), BF16), coef, coef,
                        pltpu.VMEM((n_tiles, D_MODEL, PEER_TOK), BF16), sel16, sel16,
                        pltpu.VMEM((n_tiles, 2, PEER_HEADS, F32_SUBLANES, PEER_TOK), F32)],
        compiler_params=_cparams(("arbitrary",)),
        name="peer_dense",
    )(h2.T, u_bf16, vt_bf16, r2, g2, cnt, e1, stream, mod_l, fg)


def _reorder_w_in(w):
    offs = [0]
    for width in (SSM_WIDTH, ATT_WIDTH, KV_WIDTH, KV_WIDTH, ML_WIDTH, ML_WIDTH, ML_WIDTH, ML_WIDTH, ML_GATES,
                  N_BRANCH * D_MODEL):
        offs.append(offs[-1] + width)
    u_s, q_a, k_a, v_a, q_m, k_m, v_m, o_m, g_m, gate = (w[:, offs[n]:offs[n + 1]] for n in range(10))
    pad = jnp.zeros((w.shape[0], PROJ_WIDTH - COL_GM - ML_GATES), w.dtype)
    return jnp.concatenate([u_s, q_a, q_m, k_m, v_m, o_m, gate, k_a, v_a, g_m, pad], axis=1).astype(BF16)


def kernel(x, c, ctx, c_ctx, w_mod, b_mod, norm1_g, norm2_g, w_in, ssm_lam_re, ssm_lam_im, ssm_log_step, ssm_b_re, ssm_b_im, ssm_c_re, ssm_c_im, ssm_d, ssm_w_glu, attn_q_norm_g, attn_k_norm_g, mlstm_gate_b, mlstm_norm_g, w_branch_ssm, w_branch_attn, w_branch_mlstm, w_out, peer_w_q, peer_sub_k1, peer_sub_k2, peer_u, peer_v, final_norm_g):
    bsz, lat_len, d = x.shape
    n_ctx = ctx.shape[1]
    s_tot = n_ctx + lat_len
    depth = w_in.shape[0]
    assert d == D_MODEL and n_ctx == 256 and lat_len % n_ctx == 0 and bsz == 8

    cos, sin = _rope_tables(lat_len)
    mod = _modulation(c, c_ctx, w_mod, b_mod)
    stream = jnp.concatenate([ctx, x], axis=1)

    for layer in range(depth):
        last = layer == depth - 1
        mod_l = mod[layer]
        proj = _inproj(stream.reshape(bsz, s_tot, d), norm1_g[layer], mod_l, _reorder_w_in(w_in[layer]), n_ctx)

        u_tm = proj[:, COL_U:COL_U + SSM_WIDTH].reshape(bsz, s_tot, SSM_WIDTH).transpose(1, 0, 2)
        u_tm = u_tm.reshape(s_tot * bsz, SSM_WIDTH)
        params = _s5_params(ssm_lam_re[layer], ssm_lam_im[layer], ssm_log_step[layer], ssm_b_re[layer],
                            ssm_b_im[layer], ssm_c_re[layer], ssm_c_im[layer])
        yf, yb = _s5_scan(u_tm, params, bsz, s_tot, n_ctx)
        ys_tm = _s5_post(u_tm, yf, yb, ssm_d[layer], ssm_w_glu[layer].astype(BF16))
        ys = ys_tm.reshape(s_tot, bsz, SSM_WIDTH).transpose(1, 0, 2).reshape(bsz * s_tot, SSM_WIDTH)

        qn, kn, vn = _qkprep(proj, cos, sin, attn_q_norm_g[layer], attn_k_norm_g[layer], bsz, s_tot, n_ctx)
        ya = _attention(qn, kn, vn, bsz, s_tot, n_ctx, last)

        hf, hb = _mlstm(proj, mlstm_gate_b[layer], bsz, s_tot, n_ctx)

        stream2, h2 = _merge(ys, ya, hf, hb, proj, stream.reshape(bsz * s_tot, d), mod_l, mlstm_norm_g[layer],
                             norm2_g[layer], w_branch_ssm[layer].astype(BF16), w_branch_attn[layer].astype(BF16),
                             w_branch_mlstm[layer].astype(BF16), w_out[layer].astype(BF16),
                             bsz, s_tot, n_ctx, last)

        s1, s2 = _peer_scores(h2, peer_w_q[layer].T.astype(BF16), peer_sub_k1[layer], peer_sub_k2[layer])
        sel = _peer_select(s1, s2)
        rows_per_batch = lat_len if last else s_tot
        stream = _peer_dense(h2, peer_u[layer].astype(BF16), peer_v[layer].T.astype(BF16), sel, stream2, mod_l,
                             final_norm_g, rows_per_batch, 0 if last else n_ctx, last)

    return stream.reshape(bsz, lat_len, d)
```

```python
import functools
import math

import jax
import jax.numpy as jnp
from jax import lax
from jax.experimental import pallas as pl
from jax.experimental.pallas import tpu as pltpu

F32 = jnp.float32
BF16 = jnp.bfloat16

D_MODEL = 1024
GRID_W = 64
EPS = 1e-6
N_MOD = 6
N_BRANCH = 3
SSM_WIDTH = 512
SSM_GROUP = 16
SSM_GROUPS = 32
SSM_STATE = 64
ATT_HEADS = 8
ATT_KV_HEADS = 2
ATT_HEAD_DIM = 64
ATT_WIDTH = 512
KV_WIDTH = 128
ROPE_FREQS = 16
ROPE_BASE = 10000.0
ML_HEADS = 4
ML_HEAD_DIM = 128
ML_WIDTH = 512
ML_GATES = 16
ML_CHUNK = 256
PEER_HEADS = 8
PEER_KEYS = 128
PEER_EXPERTS = PEER_KEYS * PEER_KEYS
PEER_QDIM = 256
PEER_TOPK = 16
PEER_TOK = 256

LANES = 128
F32_SUBLANES = 8
BF16_SUBLANES = 16
VMEM_LIMIT_BYTES = 56 * 1024 * 1024

PROJ_BLOCK = 512
COL_U, COL_QA, COL_QM, COL_KM, COL_VM, COL_OM = (i * PROJ_BLOCK for i in range(6))
COL_GATE = 6 * PROJ_BLOCK
COL_KA = COL_GATE + N_BRANCH * D_MODEL
COL_VA = COL_KA + KV_WIDTH
COL_GM = COL_VA + KV_WIDTH
PROJ_WIDTH = COL_KA + PROJ_BLOCK

S5_BLOCKS = 4
S5_BLOCK_IN = SSM_WIDTH // S5_BLOCKS
S5_BLOCK_STATE = SSM_GROUPS * SSM_STATE // S5_BLOCKS
S5_CHUNK = 128
S5_CHAINS = 4

NEG_INF = float("-inf")


def _cparams(sem, flags=None):
    return pltpu.CompilerParams(dimension_semantics=sem, vmem_limit_bytes=VMEM_LIMIT_BYTES, flags=flags)


def _split2(x):
    hi = x.astype(BF16)
    lo = (x - hi.astype(F32)).astype(BF16)
    return hi, lo


def _split3(x):
    hi = x.astype(BF16)
    r = x - hi.astype(F32)
    mid = r.astype(BF16)
    lo = (r - mid.astype(F32)).astype(BF16)
    return hi, mid, lo


def _dot(a, b):
    return jnp.dot(a, b, preferred_element_type=F32)


def _dot_nt(a, b):
    return lax.dot_general(a, b, (((1,), (1,)), ((), ())), preferred_element_type=F32)


def _dot_exact_rhs(x, m_bf16):
    hi, mid, lo = _split3(x)
    return _dot(hi, m_bf16) + _dot(mid, m_bf16) + _dot(lo, m_bf16)


def _dot_exact_lhs(m_bf16, x):
    hi, mid, lo = _split3(x)
    return _dot(m_bf16, hi) + _dot(m_bf16, mid) + _dot(m_bf16, lo)


def _sigmoid(x):
    return 1.0 / (1.0 + jnp.exp(-x))


def _gelu(x):
    return jax.nn.gelu(x, approximate=True)


def _mod_kernel(v_ref, w_ref, b_ref, o_ref):
    v = v_ref[...]
    sv = v * _sigmoid(v)
    w = w_ref[0]
    hi, mid, lo = _split3(sv)
    whi, wlo = _split2(w)
    acc = _dot(hi, whi) + _dot(mid, whi) + _dot(hi, wlo) + _dot(lo, whi) + _dot(mid, wlo)
    o_ref[0] = acc + b_ref[0]


def _modulation(c, c_ctx, w_mod, b_mod):
    depth = w_mod.shape[0]
    n_out = w_mod.shape[2]
    rows = 16
    v = jnp.zeros((rows, D_MODEL), F32).at[: c.shape[0]].set(c).at[8].set(c_ctx)
    tn = 1536
    return pl.pallas_call(
        _mod_kernel,
        grid=(depth, n_out // tn),
        in_specs=[pl.BlockSpec((rows, D_MODEL), lambda l, j: (0, 0)),
                  pl.BlockSpec((1, D_MODEL, tn), lambda l, j: (l, 0, j)),
                  pl.BlockSpec((1, 1, tn), lambda l, j: (l, 0, j))],
        out_specs=pl.BlockSpec((1, rows, tn), lambda l, j: (l, 0, j)),
        out_shape=jax.ShapeDtypeStruct((depth, rows, n_out), F32),
        compiler_params=_cparams(("arbitrary", "arbitrary")),
        name="modulation",
    )(v, w_mod, b_mod.reshape(depth, 1, n_out))


CTX_MOD_ROW = 8


def _mod_rows(mod_ref, which, b, is_ctx_col):
    lo = which * D_MODEL
    m_l = mod_ref[pl.ds(b, 1), lo:lo + D_MODEL]
    if is_ctx_col is None:
        return m_l
    m_c = mod_ref[CTX_MOD_ROW:CTX_MOD_ROW + 1, lo:lo + D_MODEL]
    return jnp.where(is_ctx_col, m_c, m_l)


def _rms(x, g):
    ms = jnp.mean(x * x, axis=-1, keepdims=True)
    return x * lax.rsqrt(ms + EPS) * g


def _inproj_kernel(x_ref, g_ref, mod_ref, w_ref, o_ref, h_scr, *, n_ctx, chunk):
    b = pl.program_id(0)
    j = pl.program_id(1)
    s_tot = x_ref.shape[1]

    @pl.when(j == 0)
    def _():
        for r0 in range(0, s_tot, chunk):
            x = x_ref[0, r0:r0 + chunk, :]
            xn = _rms(x, g_ref[...])
            row = r0 + lax.broadcasted_iota(jnp.int32, (chunk, 1), 0)
            is_ctx = row < n_ctx
            sh = _mod_rows(mod_ref, 0, b, is_ctx)
            sc = _mod_rows(mod_ref, 1, b, is_ctx)
            h_scr[r0:r0 + chunk, :] = (xn * (1.0 + sc) + sh).astype(BF16)

    o_ref[...] = _dot(h_scr[...], w_ref[...])


def _inproj(stream3, g, mod_l, w_bf16, n_ctx):
    bsz, s_tot, d = stream3.shape
    n_out = w_bf16.shape[1]
    tn = PROJ_BLOCK
    return pl.pallas_call(
        functools.partial(_inproj_kernel, n_ctx=n_ctx, chunk=256),
        grid=(bsz, n_out // tn),
        in_specs=[pl.BlockSpec((1, s_tot, d), lambda b, j: (b, 0, 0)),
                  pl.BlockSpec((1, d), lambda b, j: (0, 0)),
                  pl.BlockSpec(mod_l.shape, lambda b, j: (0, 0)),
                  pl.BlockSpec((d, tn), lambda b, j: (0, j))],
        out_specs=pl.BlockSpec((s_tot, tn), lambda b, j: (b, j)),
        out_shape=jax.ShapeDtypeStruct((bsz * s_tot, n_out), F32),
        scratch_shapes=[pltpu.VMEM((s_tot, d), BF16)],
        compiler_params=_cparams(("arbitrary", "arbitrary")),
        name="inproj",
    )(stream3, g.reshape(1, d), mod_l, w_bf16)


def _s5_param_kernel(lre_ref, lim_ref, ls_ref, bre_ref, bim_ref, are_ref, aim_ref, bbre_ref, bbim_ref):
    lre = lre_ref[...]
    lim = lim_ref[...]
    step = jnp.exp(ls_ref[...])
    mag = jnp.exp(lre * step)
    a_re = mag * jnp.cos(lim * step)
    a_im = mag * jnp.sin(lim * step)
    den = lre * lre + lim * lim
    z_re = ((a_re - 1.0) * lre + a_im * lim) / den
    z_im = (a_im * lre - (a_re - 1.0) * lim) / den
    b_re = bre_ref[...]
    b_im = bim_ref[...]
    are_ref[...] = a_re
    aim_ref[...] = a_im
    bbre_ref[...] = z_re * b_re - z_im * b_im
    bbim_ref[...] = z_re * b_im + z_im * b_re


def _s5_params(lam_re, lam_im, log_step, b_re, b_im, c_re, c_im):
    nd, g, n = lam_re.shape
    c = b_re.shape[-1]
    rows = nd * g
    wide = n * c

    def expand(z):
        return jnp.broadcast_to(z.reshape(rows, n, 1), (rows, n, c)).reshape(rows, wide)

    ls = jnp.broadcast_to(log_step.reshape(rows, 1), (rows, wide))
    spec = pl.BlockSpec((rows, wide), lambda: (0, 0))
    shp = jax.ShapeDtypeStruct((rows, wide), F32)
    a_re, a_im, bb_re, bb_im = pl.pallas_call(
        _s5_param_kernel,
        in_specs=[spec] * 5, out_specs=[spec] * 4, out_shape=[shp] * 4,
        name="s5_params",
    )(expand(lam_re), expand(lam_im), ls, b_re.reshape(rows, wide), b_im.reshape(rows, wide))

    gpb = g // S5_BLOCKS
    eye = jnp.eye(gpb, dtype=F32)

    def diag_in(bb):
        bb = bb.reshape(nd, S5_BLOCKS, gpb, n, c)
        return jnp.einsum("dkgnc,gh->dkgchn", bb, eye).reshape(nd, S5_BLOCKS, gpb * c, gpb * n).astype(BF16)

    def diag_out(cc):
        cc = cc.reshape(nd, S5_BLOCKS, gpb, c, n)
        return jnp.einsum("dkgcn,gh->dkgnhc", cc, eye).reshape(nd, S5_BLOCKS, gpb * n, gpb * c).astype(BF16)

    def decay(a):
        a = a.reshape(nd, S5_BLOCKS, gpb, n, c)[..., 0].reshape(nd, S5_BLOCKS, 1, gpb * n)
        return jnp.broadcast_to(a, (nd, S5_BLOCKS, 8, gpb * n))

    return decay(a_re), decay(a_im), diag_in(bb_re), diag_in(bb_im), diag_out(c_re), diag_out(c_im)


def _s5_kernel(uf_ref, ub_ref, are_ref, aim_ref, bre_ref, bim_ref, cre_ref, cim_ref,
               yf_ref, yb_ref, bur_scr, bui_scr, st_scr, *, steps, bsz):
    i = pl.program_id(0)

    @pl.when(i == 0)
    def _():
        st_scr[...] = jnp.zeros_like(st_scr)

    u_refs = (uf_ref, ub_ref)
    y_refs = (yf_ref, yb_ref)
    per_dir = S5_CHAINS // 2
    for k0 in range(0, S5_BLOCKS, per_dir):
        chains = [(d, k0 + j) for d in range(2) for j in range(per_dir)]
        for c, (d, k) in enumerate(chains):
            u = u_refs[d][:, k * S5_BLOCK_IN:(k + 1) * S5_BLOCK_IN].astype(BF16)
            bur_scr[c] = _dot(u, bre_ref[d, k])
            bui_scr[c] = _dot(u, bim_ref[d, k])

        def step(t, carry, chains=chains):
            out = []
            for c, (d, k) in enumerate(chains):
                sr, si = carry[2 * c], carry[2 * c + 1]
                tt = t if d == 0 else steps - 1 - t
                r0 = pl.multiple_of(tt * bsz, bsz)
                ar = are_ref[d, k]
                ai = aim_ref[d, k]
                nr = ar * sr - ai * si + bur_scr[c, pl.ds(r0, bsz), :]
                ni = ar * si + ai * sr + bui_scr[c, pl.ds(r0, bsz), :]
                bur_scr[c, pl.ds(r0, bsz), :] = nr
                bui_scr[c, pl.ds(r0, bsz), :] = ni
                out += [nr, ni]
            return tuple(out)

        init = tuple(st_scr[d, k, p] for (d, k) in chains for p in range(2))
        final = lax.fori_loop(0, steps, step, init, unroll=2)
        for c, (d, k) in enumerate(chains):
            st_scr[d, k, 0] = final[2 * c]
            st_scr[d, k, 1] = final[2 * c + 1]
            y = (_dot(bur_scr[c].astype(BF16), cre_ref[d, k]) - _dot(bui_scr[c].astype(BF16), cim_ref[d, k]))
            y_refs[d][:, k * S5_BLOCK_IN:(k + 1) * S5_BLOCK_IN] = y


def _s5_scan(u_tm, params, bsz, s_tot, n_ctx):
    a_re, a_im, bb_re, bb_im, cc_re, cc_im = params
    assert bsz == 8
    rows = S5_CHUNK * bsz
    n_chunks = s_tot // S5_CHUNK
    ctx_chunks = n_ctx // S5_CHUNK

    def bwd_chunk(i):
        return jnp.where(i < ctx_chunks, ctx_chunks - 1 - i, n_chunks - 1 + ctx_chunks - i)

    full = lambda a: pl.BlockSpec(a.shape, lambda i: (0,) * a.ndim)
    shp = jax.ShapeDtypeStruct((s_tot * bsz, SSM_WIDTH), F32)
    return pl.pallas_call(
        functools.partial(_s5_kernel, steps=S5_CHUNK, bsz=bsz),
        grid=(n_chunks,),
        in_specs=[pl.BlockSpec((rows, SSM_WIDTH), lambda i: (i, 0)),
                  pl.BlockSpec((rows, SSM_WIDTH), lambda i: (bwd_chunk(i), 0)),
                  full(a_re), full(a_im), full(bb_re), full(bb_im), full(cc_re), full(cc_im)],
        out_specs=[pl.BlockSpec((rows, SSM_WIDTH), lambda i: (i, 0)),
                   pl.BlockSpec((rows, SSM_WIDTH), lambda i: (bwd_chunk(i), 0))],
        out_shape=[shp, shp],
        scratch_shapes=[pltpu.VMEM((S5_CHAINS, rows, S5_BLOCK_STATE), F32),
                        pltpu.VMEM((S5_CHAINS, rows, S5_BLOCK_STATE), F32),
                        pltpu.VMEM((2, S5_BLOCKS, 2, bsz, S5_BLOCK_STATE), F32)],
        compiler_params=_cparams(("arbitrary",)),
        name="s5_scan",
    )(u_tm, u_tm, a_re, a_im, bb_re, bb_im, cc_re, cc_im)


def _s5_post_kernel(u_ref, yf_ref, yb_ref, d_ref, w_ref, o_ref):
    y = d_ref[...] * u_ref[...] + yf_ref[...] + yb_ref[...]
    g = _gelu(y).astype(BF16)
    gate = _sigmoid(_dot(g, w_ref[...]))
    o_ref[...] = (g.astype(F32) * gate).astype(BF16)


def _s5_post(u_tm, yf, yb, d_skip, w_glu_bf16):
    n, w = u_tm.shape
    tm = 1024
    row = pl.BlockSpec((tm, w), lambda i: (i, 0))
    return pl.pallas_call(
        _s5_post_kernel,
        grid=(n // tm,),
        in_specs=[row, row, row, pl.BlockSpec((1, w), lambda i: (0, 0)), pl.BlockSpec((w, w), lambda i: (0, 0))],
        out_specs=row,
        out_shape=jax.ShapeDtypeStruct((n, w), BF16),
        compiler_params=_cparams(("arbitrary",)),
        name="s5_post",
    )(u_tm, yf, yb, d_skip.reshape(1, w), w_glu_bf16)


def _rope_tables(lat_len):
    rows = lat_len // GRID_W
    row = jnp.repeat(jnp.arange(rows, dtype=F32), GRID_W)
    col = jnp.tile(jnp.arange(GRID_W, dtype=F32), rows)
    inv = ROPE_BASE ** (-jnp.arange(ROPE_FREQS, dtype=F32) / ROPE_FREQS)
    ang_r = row[:, None] * inv
    ang_c = col[:, None] * inv
    cos = jnp.concatenate([jnp.cos(ang_r), jnp.cos(ang_r), jnp.cos(ang_c), jnp.cos(ang_c)], axis=1)
    sin = jnp.concatenate([-jnp.sin(ang_r), jnp.sin(ang_r), -jnp.sin(ang_c), jnp.sin(ang_c)], axis=1)
    return jnp.tile(cos, (1, 2)), jnp.tile(sin, (1, 2))


def _head_rms(x, ones_bd, g):
    hi, lo = _split2(x * x)
    ms = (_dot(hi, ones_bd) + _dot(lo, ones_bd)) * (1.0 / ATT_HEAD_DIM)
    return x * lax.rsqrt(ms + EPS) * g


def _rope(x, cos, sin_signed):
    lane = lax.broadcasted_iota(jnp.int32, x.shape, 1)
    first_half = (lane % (2 * ROPE_FREQS)) < ROPE_FREQS
    partner = jnp.where(first_half, pltpu.roll(x, LANES - ROPE_FREQS, 1), pltpu.roll(x, ROPE_FREQS, 1))
    return x * cos + partner * sin_signed


def _qkprep_kernel(q_ref, kv_ref, cos_ref, sin_ref, qg_ref, kg_ref, qo_ref, ko_ref, vo_ref, *, n_ctx, chunk):
    s_tot = q_ref.shape[0]
    r_i = lax.broadcasted_iota(jnp.int32, (LANES, LANES), 0) // ATT_HEAD_DIM
    c_i = lax.broadcasted_iota(jnp.int32, (LANES, LANES), 1) // ATT_HEAD_DIM
    ones_bd = jnp.where(r_i == c_i, 1.0, 0.0).astype(BF16)
    lane = lax.broadcasted_iota(jnp.int32, (chunk, LANES), 1)
    low = lane < ATT_HEAD_DIM
    for r0 in range(0, s_tot, chunk):
        roped = r0 >= n_ctx
        if roped:
            cos = cos_ref[r0 - n_ctx:r0 - n_ctx + chunk, :]
            sin = sin_ref[r0 - n_ctx:r0 - n_ctx + chunk, :]
        for s in range(ATT_WIDTH // LANES):
            x = _head_rms(q_ref[r0:r0 + chunk, s * LANES:(s + 1) * LANES], ones_bd, qg_ref[...])
            if roped:
                x = _rope(x, cos, sin)
            qo_ref[r0:r0 + chunk, s * LANES:(s + 1) * LANES] = (x * (ATT_HEAD_DIM ** -0.5)).astype(BF16)
        k = _head_rms(kv_ref[r0:r0 + chunk, 0:LANES], ones_bd, kg_ref[...])
        if roped:
            k = _rope(k, cos, sin)
        v = kv_ref[r0:r0 + chunk, LANES:2 * LANES]
        k_sw = pltpu.roll(k, ATT_HEAD_DIM, 1)
        v_sw = pltpu.roll(v, ATT_HEAD_DIM, 1)
        zero = jnp.zeros_like(k)
        ks = (jnp.where(low, k, zero), jnp.where(low, zero, k_sw), jnp.where(low, k_sw, zero), jnp.where(low, zero, k))
        vs = (jnp.where(low, v, zero), jnp.where(low, zero, v_sw), jnp.where(low, v_sw, zero), jnp.where(low, zero, v))
        for n in range(4):
            ko_ref[0, n, r0:r0 + chunk, :] = ks[n].astype(BF16)
            vo_ref[0, n, r0:r0 + chunk, :] = vs[n].astype(BF16)


def _qkprep(proj, cos, sin, q_g, k_g, bsz, s_tot, n_ctx):
    qg = jnp.tile(q_g, 2).reshape(1, LANES)
    kg = jnp.tile(k_g, 2).reshape(1, LANES)
    kv_shape = jax.ShapeDtypeStruct((bsz, 4, s_tot, LANES), BF16)
    kv_spec = pl.BlockSpec((1, 4, s_tot, LANES), lambda b: (b, 0, 0, 0))
    return pl.pallas_call(
        functools.partial(_qkprep_kernel, n_ctx=n_ctx, chunk=256),
        grid=(bsz,),
        in_specs=[pl.BlockSpec((s_tot, PROJ_BLOCK), lambda b: (b, COL_QA // PROJ_BLOCK)),
                  pl.BlockSpec((s_tot, PROJ_BLOCK), lambda b: (b, COL_KA // PROJ_BLOCK)),
                  pl.BlockSpec(cos.shape, lambda b: (0, 0)), pl.BlockSpec(sin.shape, lambda b: (0, 0)),
                  pl.BlockSpec((1, LANES), lambda b: (0, 0)), pl.BlockSpec((1, LANES), lambda b: (0, 0))],
        out_specs=[pl.BlockSpec((s_tot, ATT_WIDTH), lambda b: (b, 0)), kv_spec, kv_spec],
        out_shape=[jax.ShapeDtypeStruct((bsz * s_tot, ATT_WIDTH), BF16), kv_shape, kv_shape],
        compiler_params=_cparams(("arbitrary",)),
        name="qk_prep",
    )(proj, proj, cos, sin, qg, kg)


def _attend(q_ref, k_ref, v_ref, o_ref, n_keys):
    tq = q_ref.shape[0]
    for hk in range(ATT_KV_HEADS):
        qs = jnp.concatenate([q_ref[:, (2 * hk) * LANES:(2 * hk + 1) * LANES],
                              q_ref[:, (2 * hk + 1) * LANES:(2 * hk + 2) * LANES]], axis=0)
        acc = jnp.zeros((2 * tq, LANES), F32)
        for p in range(2):
            s = _dot_nt(qs, k_ref[0, 2 * hk + p, 0:n_keys, :])
            m = jnp.max(s, axis=-1, keepdims=True)
            e = jnp.exp(s - m)
            l = jnp.sum(e, axis=-1, keepdims=True)
            acc = acc + _dot(e.astype(BF16), v_ref[0, 2 * hk + p, 0:n_keys, :]) / l
        o_ref[:, (2 * hk) * LANES:(2 * hk + 1) * LANES] = acc[0:tq].astype(BF16)
        o_ref[:, (2 * hk + 1) * LANES:(2 * hk + 2) * LANES] = acc[tq:2 * tq].astype(BF16)


def _attn_kernel(q_ref, k_ref, v_ref, o_ref, *, n_ctx, first_block):
    qi = pl.program_id(1) + first_block
    s_tot = k_ref.shape[2]
    if first_block == 0:
        @pl.when(qi == 0)
        def _():
            _attend(q_ref, k_ref, v_ref, o_ref, n_ctx)

        @pl.when(qi > 0)
        def _():
            _attend(q_ref, k_ref, v_ref, o_ref, s_tot)
    else:
        _attend(q_ref, k_ref, v_ref, o_ref, s_tot)


def _attention(qn, kn, vn, bsz, s_tot, n_ctx, latent_only):
    tq = n_ctx
    blocks = s_tot // tq
    first = 1 if latent_only else 0
    kv_spec = pl.BlockSpec((1, 4, s_tot, LANES), lambda b, i: (b, 0, 0, 0))
    return pl.pallas_call(
        functools.partial(_attn_kernel, n_ctx=n_ctx, first_block=first),
        grid=(bsz, blocks - first),
        in_specs=[pl.BlockSpec((tq, ATT_WIDTH), lambda b, i: (b * blocks + i + first, 0)), kv_spec, kv_spec],
        out_specs=pl.BlockSpec((tq, ATT_WIDTH), lambda b, i: (b * (blocks - first) + i, 0)),
        out_shape=jax.ShapeDtypeStruct((bsz * (blocks - first) * tq, ATT_WIDTH), BF16),
        compiler_params=_cparams(("arbitrary", "arbitrary")),
        name="attention",
    )(qn, kn, vn)


def _log_sigmoid(x):
    return jnp.minimum(x, 0.0) - jnp.log(1.0 + jnp.exp(-jnp.abs(x)))


def _mlstm_kernel(qf_ref, kf_ref, vf_ref, gf_ref, qb_ref, kb_ref, vb_ref, gb_ref, bias_ref,
                  hf_ref, hb_ref, c_scr, n_scr, m_scr):
    i = pl.program_id(1)
    t = ML_CHUNK

    @pl.when(i == 0)
    def _():
        c_scr[...] = jnp.zeros_like(c_scr)
        n_scr[...] = jnp.zeros_like(n_scr)
        m_scr[...] = jnp.zeros_like(m_scr)

    r_i = lax.broadcasted_iota(jnp.int32, (t, t), 0)
    c_i = lax.broadcasted_iota(jnp.int32, (t, t), 1)
    lower = r_i >= c_i
    upper = r_i <= c_i
    lower_m = jnp.where(lower, 1.0, 0.0).astype(BF16)
    upper_m = jnp.where(upper, 1.0, 0.0).astype(BF16)

    for d in range(2):
        q_ref, k_ref, v_ref, g_ref, h_ref = ((qf_ref, kf_ref, vf_ref, gf_ref, hf_ref),
                                             (qb_ref, kb_ref, vb_ref, gb_ref, hb_ref))[d]
        g = g_ref[...] + bias_ref[...]
        g_t = g.T
        lf = _log_sigmoid(g)
        lf_t = _log_sigmoid(g_t)
        causal, causal_m, anti_m = (lower, lower_m, upper_m) if d == 0 else (upper, upper_m, lower_m)
        b_cols = _dot_exact_lhs(causal_m, lf)
        b_rows = _dot_exact_rhs(lf_t, anti_m)
        last = t - 1 if d == 0 else 0
        for h in range(ML_HEADS):
            ci = d * 2 * ML_HEADS + h
            cf = ci + ML_HEADS
            i_col = g[:, ci:ci + 1]
            i_row = g_t[ci:ci + 1, :]
            b_col = b_cols[:, cf:cf + 1]
            b_row = b_rows[cf:cf + 1, :]
            b_last = b_cols[last:last + 1, cf:cf + 1]
            m_prev = m_scr[d, h]
            c_prev = c_scr[d, h]
            n_prev = n_scr[d, h]

            q = q_ref[:, h * ML_HEAD_DIM:(h + 1) * ML_HEAD_DIM]
            k = k_ref[:, h * ML_HEAD_DIM:(h + 1) * ML_HEAD_DIM] * (ML_HEAD_DIM ** -0.5)
            v = v_ref[:, h * ML_HEAD_DIM:(h + 1) * ML_HEAD_DIM]
            q16, k16, v16 = q.astype(BF16), k.astype(BF16), v.astype(BF16)

            dmat = jnp.where(causal, b_col - b_row + i_row, NEG_INF)
            inter = b_col + m_prev
            m_t = jnp.maximum(inter, jnp.max(dmat, axis=-1, keepdims=True))
            w = jnp.exp(dmat - m_t)
            a_inter = jnp.exp(inter - m_t)
            qk = _dot_nt(q16, k16) * w
            num = a_inter * _dot(q16, c_prev.astype(BF16)) + _dot(qk.astype(BF16), v16)
            den = a_inter * jnp.sum(q * n_prev, axis=-1, keepdims=True) + jnp.sum(qk, axis=-1, keepdims=True)
            h_ref[:, h * ML_HEAD_DIM:(h + 1) * ML_HEAD_DIM] = num / jnp.maximum(jnp.abs(den), jnp.exp(-m_t))

            d_last_col = b_last - b_col + i_col
            m_new = jnp.maximum(b_last + m_prev, jnp.max(d_last_col, axis=0, keepdims=True))
            w_last = jnp.exp(d_last_col - m_new)
            decay = jnp.exp(b_last + m_prev - m_new)
            kw = k * w_last
            c_scr[d, h] = decay * c_prev + _dot(kw.T.astype(BF16), v16)
            n_scr[d, h] = decay * n_prev + jnp.sum(kw, axis=0, keepdims=True)
            m_scr[d, h] = m_new


def _mlstm(proj, gate_b, bsz, s_tot, n_ctx):
    t = ML_CHUNK
    n_chunks = s_tot // t
    ctx_chunks = n_ctx // t

    def fwd(b, i):
        return b * n_chunks + i

    def bwd(b, i):
        return b * n_chunks + jnp.where(i < ctx_chunks, ctx_chunks - 1 - i, n_chunks - 1 + ctx_chunks - i)

    def col(c0, width):
        return c0 // width

    def specs(rowfn):
        return [pl.BlockSpec((t, ML_WIDTH), lambda b, i: (rowfn(b, i), col(COL_QM, ML_WIDTH))),
                pl.BlockSpec((t, ML_WIDTH), lambda b, i: (rowfn(b, i), col(COL_KM, ML_WIDTH))),
                pl.BlockSpec((t, ML_WIDTH), lambda b, i: (rowfn(b, i), col(COL_VM, ML_WIDTH))),
                pl.BlockSpec((t, LANES), lambda b, i: (rowfn(b, i), col(COL_GM, LANES)))]

    bias = jnp.zeros((1, LANES), F32).at[0, :ML_GATES].set(gate_b.reshape(ML_GATES))
    shp = jax.ShapeDtypeStruct((bsz * s_tot, ML_WIDTH), F32)
    return pl.pallas_call(
        _mlstm_kernel,
        grid=(bsz, n_chunks),
        in_specs=specs(fwd) + specs(bwd) + [pl.BlockSpec((1, LANES), lambda b, i: (0, 0))],
        out_specs=[pl.BlockSpec((t, ML_WIDTH), lambda b, i: (fwd(b, i), 0)),
                   pl.BlockSpec((t, ML_WIDTH), lambda b, i: (bwd(b, i), 0))],
        out_shape=[shp, shp],
        scratch_shapes=[pltpu.VMEM((2, ML_HEADS, ML_HEAD_DIM, ML_HEAD_DIM), F32),
                        pltpu.VMEM((2, ML_HEADS, 1, ML_HEAD_DIM), F32),
                        pltpu.VMEM((2, ML_HEADS, 1, 1), F32)],
        compiler_params=_cparams(("arbitrary", "arbitrary")),
        name="mlstm",
    )(*([proj] * 8), bias)


def _merge_kernel(ys_ref, ya_ref, hf_ref, hb_ref, om_ref, gl_ref, x_ref, mod_ref, mg_ref, n2_ref,
                  wbs_ref, wba_ref, wbm_ref, wo_ref, xo_ref, h2_ref, *, blocks_per_batch, first_block):
    n = pl.program_id(0)
    b = n // (blocks_per_batch - first_block)
    if not first_block:
        b = jnp.where(n % blocks_per_batch == 0, CTX_MOD_ROW, b)
    is_ctx = None

    hs = hf_ref[...] + hb_ref[...]
    parts = []
    for h in range(ML_HEADS):
        sl = slice(h * ML_HEAD_DIM, (h + 1) * ML_HEAD_DIM)
        parts.append(_rms(hs[:, sl], mg_ref[:, sl]))
    hn = jnp.concatenate(parts, axis=1) * _sigmoid(om_ref[...])

    gl = gl_ref[...]
    merged = (_sigmoid(gl[:, 0:D_MODEL]) * _dot(ys_ref[...], wbs_ref[...])
              + _sigmoid(gl[:, D_MODEL:2 * D_MODEL]) * _dot(ya_ref[...], wba_ref[...])
              + _sigmoid(gl[:, 2 * D_MODEL:3 * D_MODEL]) * _dot(hn.astype(BF16), wbm_ref[...]))
    mix = _dot(merged.astype(BF16), wo_ref[...])
    g1 = _mod_rows(mod_ref, 2, b, is_ctx)
    x = x_ref[...] + g1 * mix
    xo_ref[...] = x
    sh2 = _mod_rows(mod_ref, 3, b, is_ctx)
    sc2 = _mod_rows(mod_ref, 4, b, is_ctx)
    h2_ref[...] = (_rms(x, n2_ref[...]) * (1.0 + sc2) + sh2).astype(BF16)


def _merge(ys, ya, hf, hb, proj, stream, mod_l, ml_norm_g, norm2_g, wbs, wba, wbm, wo,
           bsz, s_tot, n_ctx, latent_only):
    tm = n_ctx
    bpb = s_tot // tm
    first = 1 if latent_only else 0
    per = bpb - first

    def rows(n):
        return (n // per) * bpb + n % per + first

    n_blocks = bsz * per
    full = lambda a: pl.BlockSpec(a.shape, lambda n: (0,) * a.ndim)
    wide = lambda w: pl.BlockSpec((tm, w), lambda n: (rows(n), 0))
    mg = ml_norm_g.reshape(1, ML_WIDTH)
    n2 = norm2_g.reshape(1, D_MODEL)
    return pl.pallas_call(
        functools.partial(_merge_kernel, blocks_per_batch=bpb, first_block=first),
        grid=(n_blocks,),
        in_specs=[wide(SSM_WIDTH), pl.BlockSpec((tm, ATT_WIDTH), lambda n: (n, 0)), wide(ML_WIDTH), wide(ML_WIDTH),
                  pl.BlockSpec((tm, ML_WIDTH), lambda n: (rows(n), COL_OM // ML_WIDTH)),
                  pl.BlockSpec((tm, N_BRANCH * D_MODEL), lambda n: (rows(n), COL_GATE // (N_BRANCH * D_MODEL))),
                  wide(D_MODEL), full(mod_l), full(mg), full(n2), full(wbs), full(wba), full(wbm), full(wo)],
        out_specs=[pl.BlockSpec((tm, D_MODEL), lambda n: (n, 0)), pl.BlockSpec((tm, D_MODEL), lambda n: (n, 0))],
        out_shape=[jax.ShapeDtypeStruct((n_blocks * tm, D_MODEL), F32),
                   jax.ShapeDtypeStruct((n_blocks * tm, D_MODEL), BF16)],
        compiler_params=_cparams(("arbitrary",)),
        name="merge",
    )(ys, ya, hf, hb, proj, proj, stream, mod_l, mg, n2, wbs, wba, wbm, wo)


def _peer_score_kernel(x_ref, wq_ref, k1_ref, k2_ref, s1_ref, s2_ref):
    half = PEER_QDIM // 2
    q_t = _dot_nt(wq_ref[...], x_ref[...])
    for key_ref, s_ref, lo in ((k1_ref, s1_ref, 0), (k2_ref, s2_ref, half)):
        khi, klo = _split2(key_ref[...])
        qhi, qlo = _split2(q_t[lo:lo + half, :])
        s_ref[0] = _dot(khi, qhi) + _dot(khi, qlo) + _dot(klo, qhi)


def _peer_scores(h2, wq_t, k1, k2):
    n = h2.shape[0]
    tm = 1024
    shp = jax.ShapeDtypeStruct((PEER_HEADS, PEER_KEYS, n), F32)
    out = pl.BlockSpec((1, PEER_KEYS, tm), lambda i, h: (h, 0, i))
    key = pl.BlockSpec((PEER_KEYS, PEER_QDIM // 2), lambda i, h: (0, 0))
    return pl.pallas_call(
        _peer_score_kernel,
        grid=(n // tm, PEER_HEADS),
        in_specs=[pl.BlockSpec((tm, D_MODEL), lambda i, h: (i, 0)),
                  pl.BlockSpec((PEER_QDIM, D_MODEL), lambda i, h: (h, 0)), key, key],
        out_specs=[out, out],
        out_shape=[shp, shp],
        compiler_params=_cparams(("arbitrary", "arbitrary")),
        name="peer_scores",
    )(h2, wq_t, k1, k2)


def _top_values(s, k):
    work = s
    rank = jnp.full(s.shape, float(PEER_KEYS), F32)
    vals = []
    for r in range(k):
        m = jnp.max(work, axis=0, keepdims=True)
        hit = work == m
        rank = jnp.where(hit, float(r), rank)
        work = jnp.where(hit, NEG_INF, work)
        vals.append(m)
    return vals, rank


def _peer_select_kernel(s1_ref, s2_ref, r2_ref, g2_ref, cnt_ref, e1_ref, v1_scr, v2_scr):
    k = PEER_TOPK
    half = k // 2
    s1 = s1_ref[0]
    s2 = s2_ref[0]
    v1, rank1 = _top_values(s1, k)
    v2, rank2 = _top_values(s2, k)
    for a in range(k):
        v1_scr[a:a + 1, :] = v1[a]
        v2_scr[a:a + 1, :] = v2[a]
    row = lax.broadcasted_iota(jnp.int32, (half, s1.shape[1]), 0)
    pieces = [v1[0] + v2_scr[...]]
    for a in range(1, half):
        pieces.append(jnp.where(row < k // (a + 1), v1[a] + v2_scr[0:half, :], NEG_INF))
    pieces.append(v1_scr[half:k, :] + v2[0])
    top = v1[0] + v2[0]
    z = jnp.zeros_like(top)
    thr = top
    for _ in range(k):
        thr = functools.reduce(jnp.maximum, [jnp.max(p, axis=0, keepdims=True) for p in pieces])
        pieces = [jnp.where(p == thr, NEG_INF, p) for p in pieces]
        z = z + jnp.exp(thr - top)
    cnt = jnp.zeros_like(s1)
    for a in range(k):
        if a == 0:
            sums = v1[0] + v2_scr[...]
        elif a < half:
            sums = jnp.where(row < k // (a + 1), v1[a] + v2_scr[0:half, :], NEG_INF)
        else:
            sums = v1[a] + v2[0]
        n_sel = jnp.sum(jnp.where(sums >= thr, 1.0, 0.0), axis=0, keepdims=True)
        cnt = jnp.where(rank1 == float(a), n_sel, cnt)
    r2_ref[0] = rank2.astype(BF16)
    g2_ref[0] = jnp.exp(s2 - v2[0]).astype(BF16)
    cnt_ref[0] = cnt
    e1_ref[0] = jnp.exp(s1 - v1[0]) / z


def _peer_select(s1, s2):
    heads, keys, n = s1.shape
    tl = 512
    spec = pl.BlockSpec((1, keys, tl), lambda h, i: (h, 0, i))
    shp = jax.ShapeDtypeStruct((heads, keys, n), F32)
    shp16 = jax.ShapeDtypeStruct((heads, keys, n), BF16)
    return pl.pallas_call(
        _peer_select_kernel,
        grid=(heads, n // tl),
        in_specs=[spec, spec], out_specs=[spec] * 4, out_shape=[shp16, shp16, shp, shp],
        scratch_shapes=[pltpu.VMEM((PEER_TOPK, tl), F32), pltpu.VMEM((PEER_TOPK, tl), F32)],
        compiler_params=_cparams(("arbitrary", "arbitrary")),
        name="peer_select",
    )(s1, s2)


def _peer_dense_kernel(x_ref, u_ref, vt_ref, r2_ref, g2_ref, cnt_ref, e1_ref, s_ref, mod_ref, fg_ref,
                       o_ref, acc_scr, a_scr, w_scr, coef_even, coef_odd, x_scr, r2_scr, g2_scr, row_scr,
                       *, te, nj, blocks_per_batch, n_ctx, final):
    n = pl.program_id(0)
    tm = x_ref.shape[1]
    n_tiles = tm // PEER_TOK
    keys_per_step = te // PEER_KEYS
    assert keys_per_step == F32_SUBLANES

    @pl.when(n == 0)
    def _():
        acc_scr[...] = jnp.zeros_like(acc_scr)
        coef_odd[...] = jnp.zeros_like(coef_odd)

    @pl.when(n % nj == 0)
    def _():
        for t in range(n_tiles):
            tok = slice(t * PEER_TOK, (t + 1) * PEER_TOK)
            x_scr[t] = x_ref[:, tok]
            r2_scr[t] = r2_ref[:, :, tok]
            g2_scr[t] = g2_ref[:, :, tok]

    for t in range(n_tiles):
        tok = slice(t * PEER_TOK, (t + 1) * PEER_TOK)
        row_scr[t, 0] = cnt_ref[:, :, tok]
        row_scr[t, 1] = e1_ref[:, :, tok]

    def step(read_ref, write_ref):
        zero = jnp.zeros((PEER_KEYS, LANES), BF16)

        def tile(t, carry):
            a_scr[...] = _dot(pltpu.bitcast(u_ref[...], BF16), x_scr[t]).astype(BF16)
            acc_scr[t] += _dot(pltpu.bitcast(vt_ref[...], BF16), read_ref[t])
            for s in range(keys_per_step):
                rows = slice(s * PEER_KEYS, (s + 1) * PEER_KEYS)
                for l in range(PEER_TOK // LANES):
                    lanes = slice(l * LANES, (l + 1) * LANES)
                    w = zero
                    for h in range(PEER_HEADS):
                        cnt = row_scr[t, 0, h, s:s + 1, lanes].astype(BF16)
                        e1 = row_scr[t, 1, h, s:s + 1, lanes].astype(BF16)
                        w = w + jnp.where(r2_scr[t, h, :, lanes] < cnt, g2_scr[t, h, :, lanes], zero) * e1
                    w_scr[rows, lanes] = w
            for s in range(keys_per_step):
                rows = slice(s * PEER_KEYS, (s + 1) * PEER_KEYS)
                write_ref[t, rows, :] = w_scr[rows, :] * _gelu(a_scr[rows, :])
            return carry

        lax.fori_loop(0, n_tiles, tile, 0)

    @pl.when(n % 2 == 0)
    def _():
        step(coef_odd, coef_even)

    @pl.when(n % 2 == 1)
    def _():
        step(coef_even, coef_odd)

    prev = n - 1

    @pl.when(jnp.logical_and(n > 0, prev % nj == nj - 1))
    def _():
        ip = prev // nj
        b = ip // blocks_per_batch
        for t in range(n_tiles):
            tok = slice(t * PEER_TOK, (t + 1) * PEER_TOK)
            if n_ctx:
                row = (ip % blocks_per_batch) * tm + t * PEER_TOK + lax.broadcasted_iota(jnp.int32, (PEER_TOK, 1), 0)
                is_ctx = row < n_ctx
            else:
                is_ctx = None
            g2 = _mod_rows(mod_ref, 5, b, is_ctx)
            x = s_ref[tok, :] + g2 * acc_scr[t].T
            if final:
                x = _rms(x, fg_ref[...])
            o_ref[tok, :] = x
        acc_scr[...] = jnp.zeros_like(acc_scr)


def _pack_row_pairs(a):
    r, c = a.shape
    pairs = a.astype(BF16).reshape(r // 2, 2, c).transpose(0, 2, 1)
    return lax.bitcast_convert_type(pairs, F32)


def _peer_dense(h2, u_bf16, vt_bf16, sel, stream, mod_l, final_g, rows_per_batch, n_ctx, final):
    n = h2.shape[0]
    r2, g2, cnt, e1 = sel
    tm = 768 if rows_per_batch % 768 == 0 else 512
    te = 1024
    bpb = rows_per_batch // tm
    nj = PEER_EXPERTS // te
    n_steps = (n // tm) * nj

    def cur(s):
        c = jnp.minimum(s, n_steps - 1)
        return c // nj, c % nj

    def prev(s):
        p = jnp.maximum(s - 1, 0)
        return p // nj, p % nj

    tok = pl.BlockSpec((PEER_HEADS, PEER_KEYS, tm), lambda s: (0, 0, cur(s)[0]))
    sub = pl.BlockSpec((PEER_HEADS, F32_SUBLANES, tm),
                       lambda s: (0, cur(s)[1] * (te // PEER_KEYS) // F32_SUBLANES, cur(s)[0]))
    full = lambda a: pl.BlockSpec(a.shape, lambda s: (0,) * a.ndim)
    fg = final_g.reshape(1, D_MODEL)
    n_tiles = tm // PEER_TOK
    coef = pltpu.VMEM((n_tiles, te, PEER_TOK), BF16)
    sel16 = pltpu.VMEM((n_tiles, PEER_HEADS, PEER_KEYS, PEER_TOK), BF16)
    return pl.pallas_call(
        functools.partial(_peer_dense_kernel, te=te, nj=nj, blocks_per_batch=bpb, n_ctx=n_ctx, final=final),
        grid=(n_steps + 1,),
        in_specs=[pl.BlockSpec((D_MODEL, tm), lambda s: (0, cur(s)[0])),
                  pl.BlockSpec((te // 2, D_MODEL), lambda s: (cur(s)[1], 0)),
                  pl.BlockSpec((D_MODEL // 2, te), lambda s: (0, prev(s)[1])),
                  tok, tok, sub, sub,
                  pl.BlockSpec((tm, D_MODEL), lambda s: (prev(s)[0], 0)), full(mod_l), full(fg)],
        out_specs=pl.BlockSpec((tm, D_MODEL), lambda s: (prev(s)[0], 0)),
        out_shape=jax.ShapeDtypeStruct((n, D_MODEL), F32),
        scratch_shapes=[pltpu.VMEM((n_tiles, D_MODEL, PEER_TOK), F32), pltpu.VMEM((te, PEER_TOK), BF16),
                        pltpu.VMEM((te, PEER_TOK), BF16), coef, coef,
                        pltpu.VMEM((n_tiles, D_MODEL, PEER_TOK), BF16), sel16, sel16,
                        pltpu.VMEM((n_tiles, 2, PEER_HEADS, F32_SUBLANES, PEER_TOK), F32)],
        compiler_params=_cparams(("arbitrary",)),
        name="peer_dense",
    )(h2.T, u_bf16, vt_bf16, r2, g2, cnt, e1, stream, mod_l, fg)


def _reorder_w_in(w):
    offs = [0]
    for width in (SSM_WIDTH, ATT_WIDTH, KV_WIDTH, KV_WIDTH, ML_WIDTH, ML_WIDTH, ML_WIDTH, ML_WIDTH, ML_GATES,
                  N_BRANCH * D_MODEL):
        offs.append(offs[-1] + width)
    u_s, q_a, k_a, v_a, q_m, k_m, v_m, o_m, g_m, gate = (w[:, offs[n]:offs[n + 1]] for n in range(10))
    pad = jnp.zeros((w.shape[0], PROJ_WIDTH - COL_GM - ML_GATES), w.dtype)
    return jnp.concatenate([u_s, q_a, q_m, k_m, v_m, o_m, gate, k_a, v_a, g_m, pad], axis=1).astype(BF16)


def kernel(x, c, ctx, c_ctx, w_mod, b_mod, norm1_g, norm2_g, w_in, ssm_lam_re, ssm_lam_im, ssm_log_step, ssm_b_re, ssm_b_im, ssm_c_re, ssm_c_im, ssm_d, ssm_w_glu, attn_q_norm_g, attn_k_norm_g, mlstm_gate_b, mlstm_norm_g, w_branch_ssm, w_branch_attn, w_branch_mlstm, w_out, peer_w_q, peer_sub_k1, peer_sub_k2, peer_u, peer_v, final_norm_g):
    bsz, lat_len, d = x.shape
    n_ctx = ctx.shape[1]
    s_tot = n_ctx + lat_len
    depth = w_in.shape[0]
    assert d == D_MODEL and n_ctx == 256 and lat_len % n_ctx == 0 and bsz == 8

    cos, sin = _rope_tables(lat_len)
    mod = _modulation(c, c_ctx, w_mod, b_mod)
    stream = jnp.concatenate([ctx, x], axis=1)

    for layer in range(depth):
        last = layer == depth - 1
        mod_l = mod[layer]
        proj = _inproj(stream.reshape(bsz, s_tot, d), norm1_g[layer], mod_l, _reorder_w_in(w_in[layer]), n_ctx)

        u_tm = proj[:, COL_U:COL_U + SSM_WIDTH].reshape(bsz, s_tot, SSM_WIDTH).transpose(1, 0, 2)
        u_tm = u_tm.reshape(s_tot * bsz, SSM_WIDTH)
        params = _s5_params(ssm_lam_re[layer], ssm_lam_im[layer], ssm_log_step[layer], ssm_b_re[layer],
                            ssm_b_im[layer], ssm_c_re[layer], ssm_c_im[layer])
        yf, yb = _s5_scan(u_tm, params, bsz, s_tot, n_ctx)
        ys_tm = _s5_post(u_tm, yf, yb, ssm_d[layer], ssm_w_glu[layer].astype(BF16))
        ys = ys_tm.reshape(s_tot, bsz, SSM_WIDTH).transpose(1, 0, 2).reshape(bsz * s_tot, SSM_WIDTH)

        qn, kn, vn = _qkprep(proj, cos, sin, attn_q_norm_g[layer], attn_k_norm_g[layer], bsz, s_tot, n_ctx)
        ya = _attention(qn, kn, vn, bsz, s_tot, n_ctx, last)

        hf, hb = _mlstm(proj, mlstm_gate_b[layer], bsz, s_tot, n_ctx)

        stream2, h2 = _merge(ys, ya, hf, hb, proj, stream.reshape(bsz * s_tot, d), mod_l, mlstm_norm_g[layer],
                             norm2_g[layer], w_branch_ssm[layer].astype(BF16), w_branch_attn[layer].astype(BF16),
                             w_branch_mlstm[layer].astype(BF16), w_out[layer].astype(BF16),
                             bsz, s_tot, n_ctx, last)

        s1, s2 = _peer_scores(h2, peer_w_q[layer].T.astype(BF16), peer_sub_k1[layer], peer_sub_k2[layer])
        sel = _peer_select(s1, s2)
        rows_per_batch = lat_len if last else s_tot
        stream = _peer_dense(h2, _pack_row_pairs(peer_u[layer]), _pack_row_pairs(peer_v[layer].T), sel, stream2, mod_l,
                             final_norm_g, rows_per_batch, 0 if last else n_ctx, last)

    return stream.reshape(bsz, lat_len, d)
```

```python
import functools
import math

import jax
import jax.numpy as jnp
from jax import lax
from jax.experimental import pallas as pl
from jax.experimental.pallas import tpu as pltpu

F32 = jnp.float32
BF16 = jnp.bfloat16

D_MODEL = 1024
GRID_W = 64
EPS = 1e-6
N_MOD = 6
N_BRANCH = 3
SSM_WIDTH = 512
SSM_GROUP = 16
SSM_GROUPS = 32
SSM_STATE = 64
ATT_HEADS = 8
ATT_KV_HEADS = 2
ATT_HEAD_DIM = 64
ATT_WIDTH = 512
ATT_Q_BLOCK = 512
KV_WIDTH = 128
ROPE_FREQS = 16
ROPE_BASE = 10000.0
ML_HEADS = 4
ML_HEAD_DIM = 128
ML_WIDTH = 512
ML_GATES = 16
ML_CHUNK = 256
PEER_HEADS = 8
PEER_KEYS = 128
PEER_EXPERTS = PEER_KEYS * PEER_KEYS
PEER_QDIM = 256
PEER_TOPK = 16
PEER_TOK = 256

LANES = 128
F32_SUBLANES = 8
BF16_SUBLANES = 16
VMEM_LIMIT_BYTES = 56 * 1024 * 1024

PROJ_BLOCK = 512
COL_U, COL_QA, COL_QM, COL_KM, COL_VM, COL_OM = (i * PROJ_BLOCK for i in range(6))
COL_GATE = 6 * PROJ_BLOCK
COL_KA = COL_GATE + N_BRANCH * D_MODEL
COL_VA = COL_KA + KV_WIDTH
COL_GM = COL_VA + KV_WIDTH
PROJ_WIDTH = COL_KA + PROJ_BLOCK

S5_BLOCKS = 4
S5_BLOCK_IN = SSM_WIDTH // S5_BLOCKS
S5_BLOCK_STATE = SSM_GROUPS * SSM_STATE // S5_BLOCKS
S5_CHUNK = 128
S5_CHAINS = 4

NEG_INF = float("-inf")


def _cparams(sem, flags=None):
    return pltpu.CompilerParams(dimension_semantics=sem, vmem_limit_bytes=VMEM_LIMIT_BYTES, flags=flags)


def _split2(x):
    hi = x.astype(BF16)
    lo = (x - hi.astype(F32)).astype(BF16)
    return hi, lo


def _split3(x):
    hi = x.astype(BF16)
    r = x - hi.astype(F32)
    mid = r.astype(BF16)
    lo = (r - mid.astype(F32)).astype(BF16)
    return hi, mid, lo


def _dot(a, b):
    return jnp.dot(a, b, preferred_element_type=F32)


def _dot_nt(a, b):
    return lax.dot_general(a, b, (((1,), (1,)), ((), ())), preferred_element_type=F32)


def _dot_exact_rhs(x, m_bf16):
    hi, mid, lo = _split3(x)
    return _dot(hi, m_bf16) + _dot(mid, m_bf16) + _dot(lo, m_bf16)


def _dot_exact_lhs(m_bf16, x):
    hi, mid, lo = _split3(x)
    return _dot(m_bf16, hi) + _dot(m_bf16, mid) + _dot(m_bf16, lo)


def _sigmoid(x):
    return 1.0 / (1.0 + jnp.exp(-x))


def _gelu(x):
    return jax.nn.gelu(x, approximate=True)


def _mod_kernel(v_ref, w_ref, b_ref, o_ref):
    v = v_ref[...]
    sv = v * _sigmoid(v)
    w = w_ref[0]
    hi, mid, lo = _split3(sv)
    whi, wlo = _split2(w)
    acc = _dot(hi, whi) + _dot(mid, whi) + _dot(hi, wlo) + _dot(lo, whi) + _dot(mid, wlo)
    o_ref[0] = acc + b_ref[0]


def _modulation(c, c_ctx, w_mod, b_mod):
    depth = w_mod.shape[0]
    n_out = w_mod.shape[2]
    rows = 16
    v = jnp.zeros((rows, D_MODEL), F32).at[: c.shape[0]].set(c).at[8].set(c_ctx)
    tn = 1536
    return pl.pallas_call(
        _mod_kernel,
        grid=(depth, n_out // tn),
        in_specs=[pl.BlockSpec((rows, D_MODEL), lambda l, j: (0, 0)),
                  pl.BlockSpec((1, D_MODEL, tn), lambda l, j: (l, 0, j)),
                  pl.BlockSpec((1, 1, tn), lambda l, j: (l, 0, j))],
        out_specs=pl.BlockSpec((1, rows, tn), lambda l, j: (l, 0, j)),
        out_shape=jax.ShapeDtypeStruct((depth, rows, n_out), F32),
        compiler_params=_cparams(("arbitrary", "arbitrary")),
        name="modulation",
    )(v, w_mod, b_mod.reshape(depth, 1, n_out))


CTX_MOD_ROW = 8


def _mod_rows(mod_ref, which, b, is_ctx_col):
    lo = which * D_MODEL
    m_l = mod_ref[pl.ds(b, 1), lo:lo + D_MODEL]
    if is_ctx_col is None:
        return m_l
    m_c = mod_ref[CTX_MOD_ROW:CTX_MOD_ROW + 1, lo:lo + D_MODEL]
    return jnp.where(is_ctx_col, m_c, m_l)


def _rms(x, g):
    ms = jnp.mean(x * x, axis=-1, keepdims=True)
    return x * lax.rsqrt(ms + EPS) * g


def _inproj_kernel(x_ref, g_ref, mod_ref, w_ref, o_ref, h_scr, *, n_ctx, chunk):
    b = pl.program_id(0)
    j = pl.program_id(1)
    s_tot = x_ref.shape[1]

    @pl.when(j == 0)
    def _():
        for r0 in range(0, s_tot, chunk):
            x = x_ref[0, r0:r0 + chunk, :]
            xn = _rms(x, g_ref[...])
            row = r0 + lax.broadcasted_iota(jnp.int32, (chunk, 1), 0)
            is_ctx = row < n_ctx
            sh = _mod_rows(mod_ref, 0, b, is_ctx)
            sc = _mod_rows(mod_ref, 1, b, is_ctx)
            h_scr[r0:r0 + chunk, :] = (xn * (1.0 + sc) + sh).astype(BF16)

    o_ref[...] = _dot(h_scr[...], w_ref[...])


def _inproj(stream3, g, mod_l, w_bf16, n_ctx):
    bsz, s_tot, d = stream3.shape
    n_out = w_bf16.shape[1]
    tn = PROJ_BLOCK
    return pl.pallas_call(
        functools.partial(_inproj_kernel, n_ctx=n_ctx, chunk=256),
        grid=(bsz, n_out // tn),
        in_specs=[pl.BlockSpec((1, s_tot, d), lambda b, j: (b, 0, 0)),
                  pl.BlockSpec((1, d), lambda b, j: (0, 0)),
                  pl.BlockSpec(mod_l.shape, lambda b, j: (0, 0)),
                  pl.BlockSpec((d, tn), lambda b, j: (0, j))],
        out_specs=pl.BlockSpec((s_tot, tn), lambda b, j: (b, j)),
        out_shape=jax.ShapeDtypeStruct((bsz * s_tot, n_out), F32),
        scratch_shapes=[pltpu.VMEM((s_tot, d), BF16)],
        compiler_params=_cparams(("arbitrary", "arbitrary")),
        name="inproj",
    )(stream3, g.reshape(1, d), mod_l, w_bf16)


def _s5_param_kernel(lre_ref, lim_ref, ls_ref, bre_ref, bim_ref, are_ref, aim_ref, bbre_ref, bbim_ref):
    lre = lre_ref[...]
    lim = lim_ref[...]
    step = jnp.exp(ls_ref[...])
    mag = jnp.exp(lre * step)
    a_re = mag * jnp.cos(lim * step)
    a_im = mag * jnp.sin(lim * step)
    den = lre * lre + lim * lim
    z_re = ((a_re - 1.0) * lre + a_im * lim) / den
    z_im = (a_im * lre - (a_re - 1.0) * lim) / den
    b_re = bre_ref[...]
    b_im = bim_ref[...]
    are_ref[...] = a_re
    aim_ref[...] = a_im
    bbre_ref[...] = z_re * b_re - z_im * b_im
    bbim_ref[...] = z_re * b_im + z_im * b_re


def _s5_params(lam_re, lam_im, log_step, b_re, b_im, c_re, c_im):
    nd, g, n = lam_re.shape
    c = b_re.shape[-1]
    rows = nd * g
    wide = n * c

    def expand(z):
        return jnp.broadcast_to(z.reshape(rows, n, 1), (rows, n, c)).reshape(rows, wide)

    ls = jnp.broadcast_to(log_step.reshape(rows, 1), (rows, wide))
    spec = pl.BlockSpec((rows, wide), lambda: (0, 0))
    shp = jax.ShapeDtypeStruct((rows, wide), F32)
    a_re, a_im, bb_re, bb_im = pl.pallas_call(
        _s5_param_kernel,
        in_specs=[spec] * 5, out_specs=[spec] * 4, out_shape=[shp] * 4,
        name="s5_params",
    )(expand(lam_re), expand(lam_im), ls, b_re.reshape(rows, wide), b_im.reshape(rows, wide))

    gpb = g // S5_BLOCKS
    eye = jnp.eye(gpb, dtype=F32)

    def diag_in(bb):
        bb = bb.reshape(nd, S5_BLOCKS, gpb, n, c)
        return jnp.einsum("dkgnc,gh->dkgchn", bb, eye).reshape(nd, S5_BLOCKS, gpb * c, gpb * n).astype(BF16)

    def diag_out(cc):
        cc = cc.reshape(nd, S5_BLOCKS, gpb, c, n)
        return jnp.einsum("dkgcn,gh->dkgnhc", cc, eye).reshape(nd, S5_BLOCKS, gpb * n, gpb * c).astype(BF16)

    def decay(a):
        a = a.reshape(nd, S5_BLOCKS, gpb, n, c)[..., 0].reshape(nd, S5_BLOCKS, 1, gpb * n)
        return jnp.broadcast_to(a, (nd, S5_BLOCKS, 8, gpb * n))

    return decay(a_re), decay(a_im), diag_in(bb_re), diag_in(bb_im), diag_out(c_re), diag_out(c_im)


def _s5_kernel(uf_ref, ub_ref, are_ref, aim_ref, bre_ref, bim_ref, cre_ref, cim_ref,
               yf_ref, yb_ref, bur_scr, bui_scr, st_scr, *, steps, bsz):
    i = pl.program_id(0)

    @pl.when(i == 0)
    def _():
        st_scr[...] = jnp.zeros_like(st_scr)

    u_refs = (uf_ref, ub_ref)
    y_refs = (yf_ref, yb_ref)
    per_dir = S5_CHAINS // 2
    for k0 in range(0, S5_BLOCKS, per_dir):
        chains = [(d, k0 + j) for d in range(2) for j in range(per_dir)]
        for c, (d, k) in enumerate(chains):
            u = u_refs[d][:, k * S5_BLOCK_IN:(k + 1) * S5_BLOCK_IN].astype(BF16)
            bur_scr[c] = _dot(u, bre_ref[d, k])
            bui_scr[c] = _dot(u, bim_ref[d, k])

        def step(t, carry, chains=chains):
            out = []
            for c, (d, k) in enumerate(chains):
                sr, si = carry[2 * c], carry[2 * c + 1]
                tt = t if d == 0 else steps - 1 - t
                r0 = pl.multiple_of(tt * bsz, bsz)
                ar = are_ref[d, k]
                ai = aim_ref[d, k]
                nr = ar * sr - ai * si + bur_scr[c, pl.ds(r0, bsz), :]
                ni = ar * si + ai * sr + bui_scr[c, pl.ds(r0, bsz), :]
                bur_scr[c, pl.ds(r0, bsz), :] = nr
                bui_scr[c, pl.ds(r0, bsz), :] = ni
                out += [nr, ni]
            return tuple(out)

        init = tuple(st_scr[d, k, p] for (d, k) in chains for p in range(2))
        final = lax.fori_loop(0, steps, step, init, unroll=2)
        for c, (d, k) in enumerate(chains):
            st_scr[d, k, 0] = final[2 * c]
            st_scr[d, k, 1] = final[2 * c + 1]
            y = (_dot(bur_scr[c].astype(BF16), cre_ref[d, k]) - _dot(bui_scr[c].astype(BF16), cim_ref[d, k]))
            y_refs[d][:, k * S5_BLOCK_IN:(k + 1) * S5_BLOCK_IN] = y


def _s5_scan(u_tm, params, bsz, s_tot, n_ctx):
    a_re, a_im, bb_re, bb_im, cc_re, cc_im = params
    assert bsz == 8
    rows = S5_CHUNK * bsz
    n_chunks = s_tot // S5_CHUNK
    ctx_chunks = n_ctx // S5_CHUNK

    def bwd_chunk(i):
        return jnp.where(i < ctx_chunks, ctx_chunks - 1 - i, n_chunks - 1 + ctx_chunks - i)

    full = lambda a: pl.BlockSpec(a.shape, lambda i: (0,) * a.ndim)
    shp = jax.ShapeDtypeStruct((s_tot * bsz, SSM_WIDTH), F32)
    return pl.pallas_call(
        functools.partial(_s5_kernel, steps=S5_CHUNK, bsz=bsz),
        grid=(n_chunks,),
        in_specs=[pl.BlockSpec((rows, SSM_WIDTH), lambda i: (i, 0)),
                  pl.BlockSpec((rows, SSM_WIDTH), lambda i: (bwd_chunk(i), 0)),
                  full(a_re), full(a_im), full(bb_re), full(bb_im), full(cc_re), full(cc_im)],
        out_specs=[pl.BlockSpec((rows, SSM_WIDTH), lambda i: (i, 0)),
                   pl.BlockSpec((rows, SSM_WIDTH), lambda i: (bwd_chunk(i), 0))],
        out_shape=[shp, shp],
        scratch_shapes=[pltpu.VMEM((S5_CHAINS, rows, S5_BLOCK_STATE), F32),
                        pltpu.VMEM((S5_CHAINS, rows, S5_BLOCK_STATE), F32),
                        pltpu.VMEM((2, S5_BLOCKS, 2, bsz, S5_BLOCK_STATE), F32)],
        compiler_params=_cparams(("arbitrary",)),
        name="s5_scan",
    )(u_tm, u_tm, a_re, a_im, bb_re, bb_im, cc_re, cc_im)


def _s5_post_kernel(u_ref, yf_ref, yb_ref, d_ref, w_ref, o_ref):
    y = d_ref[...] * u_ref[...] + yf_ref[...] + yb_ref[...]
    g = _gelu(y).astype(BF16)
    gate = _sigmoid(_dot(g, w_ref[...]))
    o_ref[...] = (g.astype(F32) * gate).astype(BF16)


def _s5_post(u_tm, yf, yb, d_skip, w_glu_bf16):
    n, w = u_tm.shape
    tm = 1024
    row = pl.BlockSpec((tm, w), lambda i: (i, 0))
    return pl.pallas_call(
        _s5_post_kernel,
        grid=(n // tm,),
        in_specs=[row, row, row, pl.BlockSpec((1, w), lambda i: (0, 0)), pl.BlockSpec((w, w), lambda i: (0, 0))],
        out_specs=row,
        out_shape=jax.ShapeDtypeStruct((n, w), BF16),
        compiler_params=_cparams(("arbitrary",)),
        name="s5_post",
    )(u_tm, yf, yb, d_skip.reshape(1, w), w_glu_bf16)


def _rope_tables(lat_len):
    rows = lat_len // GRID_W
    row = jnp.repeat(jnp.arange(rows, dtype=F32), GRID_W)
    col = jnp.tile(jnp.arange(GRID_W, dtype=F32), rows)
    inv = ROPE_BASE ** (-jnp.arange(ROPE_FREQS, dtype=F32) / ROPE_FREQS)
    ang_r = row[:, None] * inv
    ang_c = col[:, None] * inv
    cos = jnp.concatenate([jnp.cos(ang_r), jnp.cos(ang_r), jnp.cos(ang_c), jnp.cos(ang_c)], axis=1)
    sin = jnp.concatenate([-jnp.sin(ang_r), jnp.sin(ang_r), -jnp.sin(ang_c), jnp.sin(ang_c)], axis=1)
    return jnp.tile(cos, (1, 2)), jnp.tile(sin, (1, 2))


def _head_rms(x, ones_bd, g):
    hi, lo = _split2(x * x)
    ms = (_dot(hi, ones_bd) + _dot(lo, ones_bd)) * (1.0 / ATT_HEAD_DIM)
    return x * lax.rsqrt(ms + EPS) * g


def _rope(x, cos, sin_signed):
    lane = lax.broadcasted_iota(jnp.int32, x.shape, 1)
    first_half = (lane % (2 * ROPE_FREQS)) < ROPE_FREQS
    partner = jnp.where(first_half, pltpu.roll(x, LANES - ROPE_FREQS, 1), pltpu.roll(x, ROPE_FREQS, 1))
    return x * cos + partner * sin_signed


def _qkprep_kernel(q_ref, kv_ref, cos_ref, sin_ref, qg_ref, kg_ref, qc_ref, ql_ref, ko_ref, vo_ref, *, n_ctx, chunk):
    s_tot = q_ref.shape[0]
    r_i = lax.broadcasted_iota(jnp.int32, (LANES, LANES), 0) // ATT_HEAD_DIM
    c_i = lax.broadcasted_iota(jnp.int32, (LANES, LANES), 1) // ATT_HEAD_DIM
    ones_bd = jnp.where(r_i == c_i, 1.0, 0.0).astype(BF16)
    lane = lax.broadcasted_iota(jnp.int32, (chunk, LANES), 1)
    low = lane < ATT_HEAD_DIM
    for r0 in range(0, s_tot, chunk):
        roped = r0 >= n_ctx
        if roped:
            cos = cos_ref[r0 - n_ctx:r0 - n_ctx + chunk, :]
            sin = sin_ref[r0 - n_ctx:r0 - n_ctx + chunk, :]
        for s in range(ATT_WIDTH // LANES):
            x = _head_rms(q_ref[r0:r0 + chunk, s * LANES:(s + 1) * LANES], ones_bd, qg_ref[...])
            if roped:
                x = _rope(x, cos, sin)
            qo_ref, q0 = (ql_ref, r0 - n_ctx) if roped else (qc_ref, r0)
            qo_ref[q0:q0 + chunk, s * LANES:(s + 1) * LANES] = (x * (ATT_HEAD_DIM ** -0.5)).astype(BF16)
        k = _head_rms(kv_ref[r0:r0 + chunk, 0:LANES], ones_bd, kg_ref[...])
        if roped:
            k = _rope(k, cos, sin)
        v = kv_ref[r0:r0 + chunk, LANES:2 * LANES]
        k_sw = pltpu.roll(k, ATT_HEAD_DIM, 1)
        v_sw = pltpu.roll(v, ATT_HEAD_DIM, 1)
        zero = jnp.zeros_like(k)
        ks = (jnp.where(low, k, zero), jnp.where(low, zero, k_sw), jnp.where(low, k_sw, zero), jnp.where(low, zero, k))
        vs = (jnp.where(low, v, zero), jnp.where(low, zero, v_sw), jnp.where(low, v_sw, zero), jnp.where(low, zero, v))
        for n in range(4):
            ko_ref[0, n, r0:r0 + chunk, :] = ks[n].astype(BF16)
            vo_ref[0, n, r0:r0 + chunk, :] = vs[n].astype(BF16)


def _qkprep(proj, cos, sin, q_g, k_g, bsz, s_tot, n_ctx):
    qg = jnp.tile(q_g, 2).reshape(1, LANES)
    kg = jnp.tile(k_g, 2).reshape(1, LANES)
    kv_shape = jax.ShapeDtypeStruct((bsz, 4, s_tot, LANES), BF16)
    kv_spec = pl.BlockSpec((1, 4, s_tot, LANES), lambda b: (b, 0, 0, 0))
    return pl.pallas_call(
        functools.partial(_qkprep_kernel, n_ctx=n_ctx, chunk=256),
        grid=(bsz,),
        in_specs=[pl.BlockSpec((s_tot, PROJ_BLOCK), lambda b: (b, COL_QA // PROJ_BLOCK)),
                  pl.BlockSpec((s_tot, PROJ_BLOCK), lambda b: (b, COL_KA // PROJ_BLOCK)),
                  pl.BlockSpec(cos.shape, lambda b: (0, 0)), pl.BlockSpec(sin.shape, lambda b: (0, 0)),
                  pl.BlockSpec((1, LANES), lambda b: (0, 0)), pl.BlockSpec((1, LANES), lambda b: (0, 0))],
        out_specs=[pl.BlockSpec((n_ctx, ATT_WIDTH), lambda b: (b, 0)),
                   pl.BlockSpec((s_tot - n_ctx, ATT_WIDTH), lambda b: (b, 0)), kv_spec, kv_spec],
        out_shape=[jax.ShapeDtypeStruct((bsz * n_ctx, ATT_WIDTH), BF16),
                   jax.ShapeDtypeStruct((bsz * (s_tot - n_ctx), ATT_WIDTH), BF16), kv_shape, kv_shape],
        compiler_params=_cparams(("arbitrary",)),
        name="qk_prep",
    )(proj, proj, cos, sin, qg, kg)


def _attend(q_ref, k_ref, v_ref, o_ref, n_keys):
    tq = q_ref.shape[0]
    for hk in range(ATT_KV_HEADS):
        qs = jnp.concatenate([q_ref[:, (2 * hk) * LANES:(2 * hk + 1) * LANES],
                              q_ref[:, (2 * hk + 1) * LANES:(2 * hk + 2) * LANES]], axis=0)
        acc = jnp.zeros((2 * tq, LANES), F32)
        for p in range(2):
            s = _dot_nt(qs, k_ref[0, 2 * hk + p, 0:n_keys, :])
            m = jnp.max(s, axis=-1, keepdims=True)
            e = jnp.exp(s - m)
            l = jnp.sum(e, axis=-1, keepdims=True)
            acc = acc + _dot(e.astype(BF16), v_ref[0, 2 * hk + p, 0:n_keys, :]) / l
        o_ref[:, (2 * hk) * LANES:(2 * hk + 1) * LANES] = acc[0:tq].astype(BF16)
        o_ref[:, (2 * hk + 1) * LANES:(2 * hk + 2) * LANES] = acc[tq:2 * tq].astype(BF16)


def _attn_kernel(q_ref, k_ref, v_ref, o_ref, *, n_keys):
    _attend(q_ref, k_ref, v_ref, o_ref, n_keys)


def _attention(q, kn, vn, bsz, s_tot, tq, n_keys):
    blocks = q.shape[0] // bsz // tq
    kv_spec = pl.BlockSpec((1, 4, s_tot, LANES), lambda b, i: (b, 0, 0, 0))
    qo_spec = pl.BlockSpec((tq, ATT_WIDTH), lambda b, i: (b * blocks + i, 0))
    return pl.pallas_call(
        functools.partial(_attn_kernel, n_keys=n_keys),
        grid=(bsz, blocks),
        in_specs=[qo_spec, kv_spec, kv_spec],
        out_specs=qo_spec,
        out_shape=jax.ShapeDtypeStruct(q.shape, BF16),
        compiler_params=_cparams(("arbitrary", "arbitrary")),
        name="attention",
    )(q, kn, vn)


def _log_sigmoid(x):
    return jnp.minimum(x, 0.0) - jnp.log(1.0 + jnp.exp(-jnp.abs(x)))


def _mlstm_kernel(qf_ref, kf_ref, vf_ref, gf_ref, qb_ref, kb_ref, vb_ref, gb_ref, bias_ref,
                  hf_ref, hb_ref, c_scr, n_scr, m_scr):
    i = pl.program_id(1)
    t = ML_CHUNK

    @pl.when(i == 0)
    def _():
        c_scr[...] = jnp.zeros_like(c_scr)
        n_scr[...] = jnp.zeros_like(n_scr)
        m_scr[...] = jnp.zeros_like(m_scr)

    r_i = lax.broadcasted_iota(jnp.int32, (t, t), 0)
    c_i = lax.broadcasted_iota(jnp.int32, (t, t), 1)
    lower = r_i >= c_i
    upper = r_i <= c_i
    lower_m = jnp.where(lower, 1.0, 0.0).astype(BF16)
    upper_m = jnp.where(upper, 1.0, 0.0).astype(BF16)

    for d in range(2):
        q_ref, k_ref, v_ref, g_ref, h_ref = ((qf_ref, kf_ref, vf_ref, gf_ref, hf_ref),
                                             (qb_ref, kb_ref, vb_ref, gb_ref, hb_ref))[d]
        g = g_ref[...] + bias_ref[...]
        g_t = g.T
        lf = _log_sigmoid(g)
        lf_t = _log_sigmoid(g_t)
        causal, causal_m, anti_m = (lower, lower_m, upper_m) if d == 0 else (upper, upper_m, lower_m)
        b_cols = _dot_exact_lhs(causal_m, lf)
        b_rows = _dot_exact_rhs(lf_t, anti_m)
        last = t - 1 if d == 0 else 0
        for h in range(ML_HEADS):
            ci = d * 2 * ML_HEADS + h
            cf = ci + ML_HEADS
            i_col = g[:, ci:ci + 1]
            i_row = g_t[ci:ci + 1, :]
            b_col = b_cols[:, cf:cf + 1]
            b_row = b_rows[cf:cf + 1, :]
            b_last = b_cols[last:last + 1, cf:cf + 1]
            m_prev = m_scr[d, h]
            c_prev = c_scr[d, h]
            n_prev = n_scr[d, h]

            q = q_ref[:, h * ML_HEAD_DIM:(h + 1) * ML_HEAD_DIM]
            k = k_ref[:, h * ML_HEAD_DIM:(h + 1) * ML_HEAD_DIM] * (ML_HEAD_DIM ** -0.5)
            v = v_ref[:, h * ML_HEAD_DIM:(h + 1) * ML_HEAD_DIM]
            q16, k16, v16 = q.astype(BF16), k.astype(BF16), v.astype(BF16)

            dmat = jnp.where(causal, b_col - b_row + i_row, NEG_INF)
            inter = b_col + m_prev
            m_t = jnp.maximum(inter, jnp.max(dmat, axis=-1, keepdims=True))
            w = jnp.exp(dmat - m_t)
            a_inter = jnp.exp(inter - m_t)
            qk = _dot_nt(q16, k16) * w
            num = a_inter * _dot(q16, c_prev.astype(BF16)) + _dot(qk.astype(BF16), v16)
            den = a_inter * jnp.sum(q * n_prev, axis=-1, keepdims=True) + jnp.sum(qk, axis=-1, keepdims=True)
            h_ref[:, h * ML_HEAD_DIM:(h + 1) * ML_HEAD_DIM] = num / jnp.maximum(jnp.abs(den), jnp.exp(-m_t))

            d_last_col = b_last - b_col + i_col
            m_new = jnp.maximum(b_last + m_prev, jnp.max(d_last_col, axis=0, keepdims=True))
            w_last = jnp.exp(d_last_col - m_new)
            decay = jnp.exp(b_last + m_prev - m_new)
            kw = k * w_last
            c_scr[d, h] = decay * c_prev + _dot(kw.T.astype(BF16), v16)
            n_scr[d, h] = decay * n_prev + jnp.sum(kw, axis=0, keepdims=True)
            m_scr[d, h] = m_new


def _mlstm(proj, gate_b, bsz, s_tot, n_ctx):
    t = ML_CHUNK
    n_chunks = s_tot // t
    ctx_chunks = n_ctx // t

    def fwd(b, i):
        return b * n_chunks + i

    def bwd(b, i):
        return b * n_chunks + jnp.where(i < ctx_chunks, ctx_chunks - 1 - i, n_chunks - 1 + ctx_chunks - i)

    def col(c0, width):
        return c0 // width

    def specs(rowfn):
        return [pl.BlockSpec((t, ML_WIDTH), lambda b, i: (rowfn(b, i), col(COL_QM, ML_WIDTH))),
                pl.BlockSpec((t, ML_WIDTH), lambda b, i: (rowfn(b, i), col(COL_KM, ML_WIDTH))),
                pl.BlockSpec((t, ML_WIDTH), lambda b, i: (rowfn(b, i), col(COL_VM, ML_WIDTH))),
                pl.BlockSpec((t, LANES), lambda b, i: (rowfn(b, i), col(COL_GM, LANES)))]

    bias = jnp.zeros((1, LANES), F32).at[0, :ML_GATES].set(gate_b.reshape(ML_GATES))
    shp = jax.ShapeDtypeStruct((bsz * s_tot, ML_WIDTH), F32)
    return pl.pallas_call(
        _mlstm_kernel,
        grid=(bsz, n_chunks),
        in_specs=specs(fwd) + specs(bwd) + [pl.BlockSpec((1, LANES), lambda b, i: (0, 0))],
        out_specs=[pl.BlockSpec((t, ML_WIDTH), lambda b, i: (fwd(b, i), 0)),
                   pl.BlockSpec((t, ML_WIDTH), lambda b, i: (bwd(b, i), 0))],
        out_shape=[shp, shp],
        scratch_shapes=[pltpu.VMEM((2, ML_HEADS, ML_HEAD_DIM, ML_HEAD_DIM), F32),
                        pltpu.VMEM((2, ML_HEADS, 1, ML_HEAD_DIM), F32),
                        pltpu.VMEM((2, ML_HEADS, 1, 1), F32)],
        compiler_params=_cparams(("arbitrary", "arbitrary")),
        name="mlstm",
    )(*([proj] * 8), bias)


def _merge_kernel(ys_ref, ya_ref, hf_ref, hb_ref, om_ref, gl_ref, x_ref, mod_ref, mg_ref, n2_ref,
                  wbs_ref, wba_ref, wbm_ref, wo_ref, xo_ref, h2_ref, *, blocks_per_batch, first_block):
    n = pl.program_id(0)
    b = n // (blocks_per_batch - first_block)
    if not first_block:
        b = jnp.where(n % blocks_per_batch == 0, CTX_MOD_ROW, b)
    is_ctx = None

    hs = hf_ref[...] + hb_ref[...]
    parts = []
    for h in range(ML_HEADS):
        sl = slice(h * ML_HEAD_DIM, (h + 1) * ML_HEAD_DIM)
        parts.append(_rms(hs[:, sl], mg_ref[:, sl]))
    hn = jnp.concatenate(parts, axis=1) * _sigmoid(om_ref[...])

    gl = gl_ref[...]
    merged = (_sigmoid(gl[:, 0:D_MODEL]) * _dot(ys_ref[...], wbs_ref[...])
              + _sigmoid(gl[:, D_MODEL:2 * D_MODEL]) * _dot(ya_ref[...], wba_ref[...])
              + _sigmoid(gl[:, 2 * D_MODEL:3 * D_MODEL]) * _dot(hn.astype(BF16), wbm_ref[...]))
    mix = _dot(merged.astype(BF16), wo_ref[...])
    g1 = _mod_rows(mod_ref, 2, b, is_ctx)
    x = x_ref[...] + g1 * mix
    xo_ref[...] = x
    sh2 = _mod_rows(mod_ref, 3, b, is_ctx)
    sc2 = _mod_rows(mod_ref, 4, b, is_ctx)
    h2_ref[...] = (_rms(x, n2_ref[...]) * (1.0 + sc2) + sh2).astype(BF16)


def _merge(ys, ya, hf, hb, proj, stream, mod_l, ml_norm_g, norm2_g, wbs, wba, wbm, wo,
           bsz, s_tot, n_ctx, latent_only):
    tm = n_ctx
    bpb = s_tot // tm
    first = 1 if latent_only else 0
    per = bpb - first

    def rows(n):
        return (n // per) * bpb + n % per + first

    n_blocks = bsz * per
    full = lambda a: pl.BlockSpec(a.shape, lambda n: (0,) * a.ndim)
    wide = lambda w: pl.BlockSpec((tm, w), lambda n: (rows(n), 0))
    mg = ml_norm_g.reshape(1, ML_WIDTH)
    n2 = norm2_g.reshape(1, D_MODEL)
    return pl.pallas_call(
        functools.partial(_merge_kernel, blocks_per_batch=bpb, first_block=first),
        grid=(n_blocks,),
        in_specs=[wide(SSM_WIDTH), pl.BlockSpec((tm, ATT_WIDTH), lambda n: (n, 0)), wide(ML_WIDTH), wide(ML_WIDTH),
                  pl.BlockSpec((tm, ML_WIDTH), lambda n: (rows(n), COL_OM // ML_WIDTH)),
                  pl.BlockSpec((tm, N_BRANCH * D_MODEL), lambda n: (rows(n), COL_GATE // (N_BRANCH * D_MODEL))),
                  wide(D_MODEL), full(mod_l), full(mg), full(n2), full(wbs), full(wba), full(wbm), full(wo)],
        out_specs=[pl.BlockSpec((tm, D_MODEL), lambda n: (n, 0)), pl.BlockSpec((tm, D_MODEL), lambda n: (n, 0))],
        out_shape=[jax.ShapeDtypeStruct((n_blocks * tm, D_MODEL), F32),
                   jax.ShapeDtypeStruct((n_blocks * tm, D_MODEL), BF16)],
        compiler_params=_cparams(("arbitrary",)),
        name="merge",
    )(ys, ya, hf, hb, proj, proj, stream, mod_l, mg, n2, wbs, wba, wbm, wo)


def _peer_score_kernel(x_ref, wq_ref, k1_ref, k2_ref, s1_ref, s2_ref):
    half = PEER_QDIM // 2
    q_t = _dot_nt(wq_ref[...], x_ref[...])
    for key_ref, s_ref, lo in ((k1_ref, s1_ref, 0), (k2_ref, s2_ref, half)):
        khi, klo = _split2(key_ref[...])
        qhi, qlo = _split2(q_t[lo:lo + half, :])
        s_ref[0] = _dot(khi, qhi) + _dot(khi, qlo) + _dot(klo, qhi)


def _peer_scores(h2, wq_t, k1, k2):
    n = h2.shape[0]
    tm = 1024
    shp = jax.ShapeDtypeStruct((PEER_HEADS, PEER_KEYS, n), F32)
    out = pl.BlockSpec((1, PEER_KEYS, tm), lambda i, h: (h, 0, i))
    key = pl.BlockSpec((PEER_KEYS, PEER_QDIM // 2), lambda i, h: (0, 0))
    return pl.pallas_call(
        _peer_score_kernel,
        grid=(n // tm, PEER_HEADS),
        in_specs=[pl.BlockSpec((tm, D_MODEL), lambda i, h: (i, 0)),
                  pl.BlockSpec((PEER_QDIM, D_MODEL), lambda i, h: (h, 0)), key, key],
        out_specs=[out, out],
        out_shape=[shp, shp],
        compiler_params=_cparams(("arbitrary", "arbitrary")),
        name="peer_scores",
    )(h2, wq_t, k1, k2)


def _top_values(s, k):
    work = s
    rank = jnp.full(s.shape, float(PEER_KEYS), F32)
    vals = []
    for r in range(k):
        m = jnp.max(work, axis=0, keepdims=True)
        hit = work == m
        rank = jnp.where(hit, float(r), rank)
        work = jnp.where(hit, NEG_INF, work)
        vals.append(m)
    return vals, rank


def _peer_select_kernel(s1_ref, s2_ref, r2_ref, g2_ref, cnt_ref, e1_ref, v1_scr, v2_scr):
    k = PEER_TOPK
    half = k // 2
    s1 = s1_ref[0]
    s2 = s2_ref[0]
    v1, rank1 = _top_values(s1, k)
    v2, rank2 = _top_values(s2, k)
    for a in range(k):
        v1_scr[a:a + 1, :] = v1[a]
        v2_scr[a:a + 1, :] = v2[a]
    row = lax.broadcasted_iota(jnp.int32, (half, s1.shape[1]), 0)
    pieces = [v1[0] + v2_scr[...]]
    for a in range(1, half):
        pieces.append(jnp.where(row < k // (a + 1), v1[a] + v2_scr[0:half, :], NEG_INF))
    pieces.append(v1_scr[half:k, :] + v2[0])
    top = v1[0] + v2[0]
    z = jnp.zeros_like(top)
    thr = top
    for _ in range(k):
        thr = functools.reduce(jnp.maximum, [jnp.max(p, axis=0, keepdims=True) for p in pieces])
        pieces = [jnp.where(p == thr, NEG_INF, p) for p in pieces]
        z = z + jnp.exp(thr - top)
    cnt = jnp.zeros_like(s1)
    for a in range(k):
        if a == 0:
            sums = v1[0] + v2_scr[...]
        elif a < half:
            sums = jnp.where(row < k // (a + 1), v1[a] + v2_scr[0:half, :], NEG_INF)
        else:
            sums = v1[a] + v2[0]
        n_sel = jnp.sum(jnp.where(sums >= thr, 1.0, 0.0), axis=0, keepdims=True)
        cnt = jnp.where(rank1 == float(a), n_sel, cnt)
    r2_ref[0] = rank2.astype(BF16)
    g2_ref[0] = jnp.exp(s2 - v2[0]).astype(BF16)
    cnt_ref[0] = cnt
    e1_ref[0] = jnp.exp(s1 - v1[0]) / z


def _peer_select(s1, s2):
    heads, keys, n = s1.shape
    tl = 512
    spec = pl.BlockSpec((1, keys, tl), lambda h, i: (h, 0, i))
    shp = jax.ShapeDtypeStruct((heads, keys, n), F32)
    shp16 = jax.ShapeDtypeStruct((heads, keys, n), BF16)
    return pl.pallas_call(
        _peer_select_kernel,
        grid=(heads, n // tl),
        in_specs=[spec, spec], out_specs=[spec] * 4, out_shape=[shp16, shp16, shp, shp],
        scratch_shapes=[pltpu.VMEM((PEER_TOPK, tl), F32), pltpu.VMEM((PEER_TOPK, tl), F32)],
        compiler_params=_cparams(("arbitrary", "arbitrary")),
        name="peer_select",
    )(s1, s2)


def _peer_dense_kernel(x_ref, u_ref, vt_ref, r2_ref, g2_ref, cnt_ref, e1_ref, s_ref, mod_ref, fg_ref,
                       o_ref, acc_scr, a_scr, w_scr, coef_even, coef_odd, x_scr, r2_scr, g2_scr, row_scr,
                       *, te, nj, blocks_per_batch, n_ctx, final):
    n = pl.program_id(0)
    tm = x_ref.shape[1]
    n_tiles = tm // PEER_TOK
    keys_per_step = te // PEER_KEYS
    assert keys_per_step % F32_SUBLANES == 0

    @pl.when(n == 0)
    def _():
        acc_scr[...] = jnp.zeros_like(acc_scr)
        coef_odd[...] = jnp.zeros_like(coef_odd)

    @pl.when(n % nj == 0)
    def _():
        for t in range(n_tiles):
            tok = slice(t * PEER_TOK, (t + 1) * PEER_TOK)
            x_scr[t] = x_ref[:, tok]
            r2_scr[t] = r2_ref[:, :, tok]
            g2_scr[t] = g2_ref[:, :, tok]

    for t in range(n_tiles):
        tok = slice(t * PEER_TOK, (t + 1) * PEER_TOK)
        row_scr[t, 0] = cnt_ref[:, :, tok]
        row_scr[t, 1] = e1_ref[:, :, tok]

    def step(read_ref, write_ref):
        zero = jnp.zeros((PEER_KEYS, LANES), BF16)

        def tile(t, carry):
            a_scr[...] = _dot(pltpu.bitcast(u_ref[...], BF16), x_scr[t]).astype(BF16)
            acc_scr[t] += _dot(pltpu.bitcast(vt_ref[...], BF16), read_ref[t])
            for s in range(keys_per_step):
                rows = slice(s * PEER_KEYS, (s + 1) * PEER_KEYS)
                for l in range(PEER_TOK // LANES):
                    lanes = slice(l * LANES, (l + 1) * LANES)
                    w = zero
                    for h in range(PEER_HEADS):
                        cnt = row_scr[t, 0, h, s:s + 1, lanes].astype(BF16)
                        e1 = row_scr[t, 1, h, s:s + 1, lanes].astype(BF16)
                        w = w + jnp.where(r2_scr[t, h, :, lanes] < cnt, g2_scr[t, h, :, lanes], zero) * e1
                    w_scr[rows, lanes] = w
            for s in range(keys_per_step):
                rows = slice(s * PEER_KEYS, (s + 1) * PEER_KEYS)
                write_ref[t, rows, :] = w_scr[rows, :] * _gelu(a_scr[rows, :])
            return carry

        lax.fori_loop(0, n_tiles, tile, 0)

    @pl.when(n % 2 == 0)
    def _():
        step(coef_odd, coef_even)

    @pl.when(n % 2 == 1)
    def _():
        step(coef_even, coef_odd)

    prev = n - 1

    @pl.when(jnp.logical_and(n > 0, prev % nj == nj - 1))
    def _():
        ip = prev // nj
        b = ip // blocks_per_batch
        for t in range(n_tiles):
            tok = slice(t * PEER_TOK, (t + 1) * PEER_TOK)
            if n_ctx:
                row = (ip % blocks_per_batch) * tm + t * PEER_TOK + lax.broadcasted_iota(jnp.int32, (PEER_TOK, 1), 0)
                is_ctx = row < n_ctx
            else:
                is_ctx = None
            g2 = _mod_rows(mod_ref, 5, b, is_ctx)
            x = s_ref[tok, :] + g2 * acc_scr[t].T
            if final:
                x = _rms(x, fg_ref[...])
            o_ref[tok, :] = x
        acc_scr[...] = jnp.zeros_like(acc_scr)


def _pack_tables_kernel(u_ref, v_ref, uo_ref, vo_ref):
    uo_ref[...] = pltpu.bitcast(u_ref[...].astype(BF16), F32)
    vo_ref[...] = pltpu.bitcast(v_ref[...].T.astype(BF16), F32)


def _pack_tables(u, v):
    e, d = u.shape
    te = 512
    return pl.pallas_call(
        _pack_tables_kernel,
        grid=(e // te,),
        in_specs=[pl.BlockSpec((te, d), lambda j: (j, 0)), pl.BlockSpec((te, d), lambda j: (j, 0))],
        out_specs=[pl.BlockSpec((te // 2, d), lambda j: (j, 0)), pl.BlockSpec((d // 2, te), lambda j: (0, j))],
        out_shape=[jax.ShapeDtypeStruct((e // 2, d), F32), jax.ShapeDtypeStruct((d // 2, e), F32)],
        compiler_params=_cparams(("arbitrary",)),
        name="pack_tables",
    )(u, v)


def _peer_dense(h2, u_bf16, vt_bf16, sel, stream, mod_l, final_g, rows_per_batch, n_ctx, final):
    n = h2.shape[0]
    r2, g2, cnt, e1 = sel
    tm = 768 if rows_per_batch % 768 == 0 else 512
    te = 1024 if tm == 768 else 2048
    bpb = rows_per_batch // tm
    nj = PEER_EXPERTS // te
    n_steps = (n // tm) * nj

    def cur(s):
        c = jnp.minimum(s, n_steps - 1)
        return c // nj, c % nj

    def prev(s):
        p = jnp.maximum(s - 1, 0)
        return p // nj, p % nj

    tok = pl.BlockSpec((PEER_HEADS, PEER_KEYS, tm), lambda s: (0, 0, cur(s)[0]))
    sub = pl.BlockSpec((PEER_HEADS, te // PEER_KEYS, tm), lambda s: (0, cur(s)[1], cur(s)[0]))
    full = lambda a: pl.BlockSpec(a.shape, lambda s: (0,) * a.ndim)
    fg = final_g.reshape(1, D_MODEL)
    n_tiles = tm // PEER_TOK
    coef = pltpu.VMEM((n_tiles, te, PEER_TOK), BF16)
    sel16 = pltpu.VMEM((n_tiles, PEER_HEADS, PEER_KEYS, PEER_TOK), BF16)
    return pl.pallas_call(
        functools.partial(_peer_dense_kernel, te=te, nj=nj, blocks_per_batch=bpb, n_ctx=n_ctx, final=final),
        grid=(n_steps + 1,),
        in_specs=[pl.BlockSpec((D_MODEL, tm), lambda s: (0, cur(s)[0])),
                  pl.BlockSpec((te // 2, D_MODEL), lambda s: (cur(s)[1], 0)),
                  pl.BlockSpec((D_MODEL // 2, te), lambda s: (0, prev(s)[1])),
                  tok, tok, sub, sub,
                  pl.BlockSpec((tm, D_MODEL), lambda s: (prev(s)[0], 0)), full(mod_l), full(fg)],
        out_specs=pl.BlockSpec((tm, D_MODEL), lambda s: (prev(s)[0], 0)),
        out_shape=jax.ShapeDtypeStruct((n, D_MODEL), F32),
        scratch_shapes=[pltpu.VMEM((n_tiles, D_MODEL, PEER_TOK), F32), pltpu.VMEM((te, PEER_TOK), BF16),
                        pltpu.VMEM((te, PEER_TOK), BF16), coef, coef,
                        pltpu.VMEM((n_tiles, D_MODEL, PEER_TOK), BF16), sel16, sel16,
                        pltpu.VMEM((n_tiles, 2, PEER_HEADS, te // PEER_KEYS, PEER_TOK), F32)],
        compiler_params=_cparams(("arbitrary",)),
        name="peer_dense",
    )(h2.T, u_bf16, vt_bf16, r2, g2, cnt, e1, stream, mod_l, fg)


def _reorder_w_in(w):
    offs = [0]
    for width in (SSM_WIDTH, ATT_WIDTH, KV_WIDTH, KV_WIDTH, ML_WIDTH, ML_WIDTH, ML_WIDTH, ML_WIDTH, ML_GATES,
                  N_BRANCH * D_MODEL):
        offs.append(offs[-1] + width)
    u_s, q_a, k_a, v_a, q_m, k_m, v_m, o_m, g_m, gate = (w[:, offs[n]:offs[n + 1]] for n in range(10))
    pad = jnp.zeros((w.shape[0], PROJ_WIDTH - COL_GM - ML_GATES), w.dtype)
    return jnp.concatenate([u_s, q_a, q_m, k_m, v_m, o_m, gate, k_a, v_a, g_m, pad], axis=1).astype(BF16)


def kernel(x, c, ctx, c_ctx, w_mod, b_mod, norm1_g, norm2_g, w_in, ssm_lam_re, ssm_lam_im, ssm_log_step, ssm_b_re, ssm_b_im, ssm_c_re, ssm_c_im, ssm_d, ssm_w_glu, attn_q_norm_g, attn_k_norm_g, mlstm_gate_b, mlstm_norm_g, w_branch_ssm, w_branch_attn, w_branch_mlstm, w_out, peer_w_q, peer_sub_k1, peer_sub_k2, peer_u, peer_v, final_norm_g):
    bsz, lat_len, d = x.shape
    n_ctx = ctx.shape[1]
    s_tot = n_ctx + lat_len
    depth = w_in.shape[0]
    assert d == D_MODEL and n_ctx == 256 and lat_len % n_ctx == 0 and bsz == 8

    cos, sin = _rope_tables(lat_len)
    mod = _modulation(c, c_ctx, w_mod, b_mod)
    stream = jnp.concatenate([ctx, x], axis=1)

    for layer in range(depth):
        last = layer == depth - 1
        mod_l = mod[layer]
        proj = _inproj(stream.reshape(bsz, s_tot, d), norm1_g[layer], mod_l, _reorder_w_in(w_in[layer]), n_ctx)

        u_tm = proj[:, COL_U:COL_U + SSM_WIDTH].reshape(bsz, s_tot, SSM_WIDTH).transpose(1, 0, 2)
        u_tm = u_tm.reshape(s_tot * bsz, SSM_WIDTH)
        params = _s5_params(ssm_lam_re[layer], ssm_lam_im[layer], ssm_log_step[layer], ssm_b_re[layer],
                            ssm_b_im[layer], ssm_c_re[layer], ssm_c_im[layer])
        yf, yb = _s5_scan(u_tm, params, bsz, s_tot, n_ctx)
        ys_tm = _s5_post(u_tm, yf, yb, ssm_d[layer], ssm_w_glu[layer].astype(BF16))
        ys = ys_tm.reshape(s_tot, bsz, SSM_WIDTH).transpose(1, 0, 2).reshape(bsz * s_tot, SSM_WIDTH)

        q_ctx, q_lat, kn, vn = _qkprep(proj, cos, sin, attn_q_norm_g[layer], attn_k_norm_g[layer], bsz, s_tot, n_ctx)
        ya = _attention(q_lat, kn, vn, bsz, s_tot, ATT_Q_BLOCK, s_tot)
        if not last:
            ya_ctx = _attention(q_ctx, kn, vn, bsz, s_tot, n_ctx, n_ctx)
            ya = jnp.concatenate([ya_ctx.reshape(bsz, n_ctx, ATT_WIDTH), ya.reshape(bsz, lat_len, ATT_WIDTH)],
                                 axis=1).reshape(bsz * s_tot, ATT_WIDTH)

        hf, hb = _mlstm(proj, mlstm_gate_b[layer], bsz, s_tot, n_ctx)

        stream2, h2 = _merge(ys, ya, hf, hb, proj, stream.reshape(bsz * s_tot, d), mod_l, mlstm_norm_g[layer],
                             norm2_g[layer], w_branch_ssm[layer].astype(BF16), w_branch_attn[layer].astype(BF16),
                             w_branch_mlstm[layer].astype(BF16), w_out[layer].astype(BF16),
                             bsz, s_tot, n_ctx, last)

        s1, s2 = _peer_scores(h2, peer_w_q[layer].T.astype(BF16), peer_sub_k1[layer], peer_sub_k2[layer])
        sel = _peer_select(s1, s2)
        rows_per_batch = lat_len if last else s_tot
        u_words, vt_words = _pack_tables(peer_u[layer], peer_v[layer])
        stream = _peer_dense(h2, u_words, vt_words, sel, stream2, mod_l,
                             final_norm_g, rows_per_batch, 0 if last else n_ctx, last)

    return stream.reshape(bsz, lat_len, d)
```

```python
import functools
import math

import jax
import jax.numpy as jnp
from jax import lax
from jax.experimental import pallas as pl
from jax.experimental.pallas import tpu as pltpu

F32 = jnp.float32
BF16 = jnp.bfloat16

D_MODEL = 1024
GRID_W = 64
EPS = 1e-6
N_MOD = 6
N_BRANCH = 3
SSM_WIDTH = 512
SSM_GROUP = 16
SSM_GROUPS = 32
SSM_STATE = 64
ATT_HEADS = 8
ATT_KV_HEADS = 2
ATT_HEAD_DIM = 64
ATT_WIDTH = 512
ATT_Q_BLOCK = 512
KV_WIDTH = 128
ROPE_FREQS = 16
ROPE_BASE = 10000.0
ML_HEADS = 4
ML_HEAD_DIM = 128
ML_WIDTH = 512
ML_GATES = 16
ML_CHUNK = 256
PEER_HEADS = 8
PEER_KEYS = 128
PEER_EXPERTS = PEER_KEYS * PEER_KEYS
PEER_QDIM = 256
PEER_TOPK = 16
PEER_TOK = 256

LANES = 128
F32_SUBLANES = 8
BF16_SUBLANES = 16
VMEM_LIMIT_BYTES = 56 * 1024 * 1024

PROJ_BLOCK = 512
COL_U, COL_QA, COL_QM, COL_KM, COL_VM, COL_OM = (i * PROJ_BLOCK for i in range(6))
COL_GATE = 6 * PROJ_BLOCK
COL_KA = COL_GATE + N_BRANCH * D_MODEL
COL_VA = COL_KA + KV_WIDTH
COL_GM = COL_VA + KV_WIDTH
PROJ_WIDTH = COL_KA + PROJ_BLOCK

S5_BLOCKS = 4
S5_BLOCK_IN = SSM_WIDTH // S5_BLOCKS
S5_BLOCK_STATE = SSM_GROUPS * SSM_STATE // S5_BLOCKS
S5_CHUNK = 128
S5_CHAINS = 4

NEG_INF = float("-inf")


def _cparams(sem, flags=None):
    return pltpu.CompilerParams(dimension_semantics=sem, vmem_limit_bytes=VMEM_LIMIT_BYTES, flags=flags)


def _split2(x):
    hi = x.astype(BF16)
    lo = (x - hi.astype(F32)).astype(BF16)
    return hi, lo


def _split3(x):
    hi = x.astype(BF16)
    r = x - hi.astype(F32)
    mid = r.astype(BF16)
    lo = (r - mid.astype(F32)).astype(BF16)
    return hi, mid, lo


def _dot(a, b):
    return jnp.dot(a, b, preferred_element_type=F32)


def _dot_nt(a, b):
    return lax.dot_general(a, b, (((1,), (1,)), ((), ())), preferred_element_type=F32)


def _dot_exact_rhs(x, m_bf16):
    hi, mid, lo = _split3(x)
    return _dot(hi, m_bf16) + _dot(mid, m_bf16) + _dot(lo, m_bf16)


def _dot_exact_lhs(m_bf16, x):
    hi, mid, lo = _split3(x)
    return _dot(m_bf16, hi) + _dot(m_bf16, mid) + _dot(m_bf16, lo)


def _sigmoid(x):
    return 1.0 / (1.0 + jnp.exp(-x))


def _gelu(x):
    return jax.nn.gelu(x, approximate=True)


def _mod_kernel(v_ref, w_ref, b_ref, o_ref):
    v = v_ref[...]
    sv = v * _sigmoid(v)
    w = w_ref[0]
    hi, mid, lo = _split3(sv)
    whi, wlo = _split2(w)
    acc = _dot(hi, whi) + _dot(mid, whi) + _dot(hi, wlo) + _dot(lo, whi) + _dot(mid, wlo)
    o_ref[0] = acc + b_ref[0]


def _modulation(c, c_ctx, w_mod, b_mod):
    depth = w_mod.shape[0]
    n_out = w_mod.shape[2]
    rows = 16
    v = jnp.zeros((rows, D_MODEL), F32).at[: c.shape[0]].set(c).at[8].set(c_ctx)
    tn = 1536
    return pl.pallas_call(
        _mod_kernel,
        grid=(depth, n_out // tn),
        in_specs=[pl.BlockSpec((rows, D_MODEL), lambda l, j: (0, 0)),
                  pl.BlockSpec((1, D_MODEL, tn), lambda l, j: (l, 0, j)),
                  pl.BlockSpec((1, 1, tn), lambda l, j: (l, 0, j))],
        out_specs=pl.BlockSpec((1, rows, tn), lambda l, j: (l, 0, j)),
        out_shape=jax.ShapeDtypeStruct((depth, rows, n_out), F32),
        compiler_params=_cparams(("arbitrary", "arbitrary")),
        name="modulation",
    )(v, w_mod, b_mod.reshape(depth, 1, n_out))


CTX_MOD_ROW = 8


def _mod_rows(mod_ref, which, b, is_ctx_col):
    lo = which * D_MODEL
    m_l = mod_ref[pl.ds(b, 1), lo:lo + D_MODEL]
    if is_ctx_col is None:
        return m_l
    m_c = mod_ref[CTX_MOD_ROW:CTX_MOD_ROW + 1, lo:lo + D_MODEL]
    return jnp.where(is_ctx_col, m_c, m_l)


def _rms(x, g):
    ms = jnp.mean(x * x, axis=-1, keepdims=True)
    return x * lax.rsqrt(ms + EPS) * g


def _inproj_kernel(x_ref, g_ref, mod_ref, w_ref, o_ref, h_scr, *, n_ctx, chunk):
    b = pl.program_id(0)
    j = pl.program_id(1)
    s_tot = x_ref.shape[1]

    @pl.when(j == 0)
    def _():
        for r0 in range(0, s_tot, chunk):
            x = x_ref[0, r0:r0 + chunk, :]
            xn = _rms(x, g_ref[...])
            row = r0 + lax.broadcasted_iota(jnp.int32, (chunk, 1), 0)
            is_ctx = row < n_ctx
            sh = _mod_rows(mod_ref, 0, b, is_ctx)
            sc = _mod_rows(mod_ref, 1, b, is_ctx)
            h_scr[r0:r0 + chunk, :] = (xn * (1.0 + sc) + sh).astype(BF16)

    o_ref[...] = _dot(h_scr[...], w_ref[...])


def _inproj(stream3, g, mod_l, w_bf16, n_ctx):
    bsz, s_tot, d = stream3.shape
    n_out = w_bf16.shape[1]
    tn = PROJ_BLOCK
    return pl.pallas_call(
        functools.partial(_inproj_kernel, n_ctx=n_ctx, chunk=256),
        grid=(bsz, n_out // tn),
        in_specs=[pl.BlockSpec((1, s_tot, d), lambda b, j: (b, 0, 0)),
                  pl.BlockSpec((1, d), lambda b, j: (0, 0)),
                  pl.BlockSpec(mod_l.shape, lambda b, j: (0, 0)),
                  pl.BlockSpec((d, tn), lambda b, j: (0, j))],
        out_specs=pl.BlockSpec((s_tot, tn), lambda b, j: (b, j)),
        out_shape=jax.ShapeDtypeStruct((bsz * s_tot, n_out), F32),
        scratch_shapes=[pltpu.VMEM((s_tot, d), BF16)],
        compiler_params=_cparams(("arbitrary", "arbitrary")),
        name="inproj",
    )(stream3, g.reshape(1, d), mod_l, w_bf16)


def _s5_param_kernel(lre_ref, lim_ref, ls_ref, bre_ref, bim_ref, are_ref, aim_ref, bbre_ref, bbim_ref):
    lre = lre_ref[...]
    lim = lim_ref[...]
    step = jnp.exp(ls_ref[...])
    mag = jnp.exp(lre * step)
    a_re = mag * jnp.cos(lim * step)
    a_im = mag * jnp.sin(lim * step)
    den = lre * lre + lim * lim
    z_re = ((a_re - 1.0) * lre + a_im * lim) / den
    z_im = (a_im * lre - (a_re - 1.0) * lim) / den
    b_re = bre_ref[...]
    b_im = bim_ref[...]
    are_ref[...] = a_re
    aim_ref[...] = a_im
    bbre_ref[...] = z_re * b_re - z_im * b_im
    bbim_ref[...] = z_re * b_im + z_im * b_re


def _s5_params(lam_re, lam_im, log_step, b_re, b_im, c_re, c_im):
    nd, g, n = lam_re.shape
    c = b_re.shape[-1]
    rows = nd * g
    wide = n * c

    def expand(z):
        return jnp.broadcast_to(z.reshape(rows, n, 1), (rows, n, c)).reshape(rows, wide)

    ls = jnp.broadcast_to(log_step.reshape(rows, 1), (rows, wide))
    spec = pl.BlockSpec((rows, wide), lambda: (0, 0))
    shp = jax.ShapeDtypeStruct((rows, wide), F32)
    a_re, a_im, bb_re, bb_im = pl.pallas_call(
        _s5_param_kernel,
        in_specs=[spec] * 5, out_specs=[spec] * 4, out_shape=[shp] * 4,
        name="s5_params",
    )(expand(lam_re), expand(lam_im), ls, b_re.reshape(rows, wide), b_im.reshape(rows, wide))

    gpb = g // S5_BLOCKS
    eye = jnp.eye(gpb, dtype=F32)

    def diag_in(bb):
        bb = bb.reshape(nd, S5_BLOCKS, gpb, n, c)
        return jnp.einsum("dkgnc,gh->dkgchn", bb, eye).reshape(nd, S5_BLOCKS, gpb * c, gpb * n).astype(BF16)

    def diag_out(cc):
        cc = cc.reshape(nd, S5_BLOCKS, gpb, c, n)
        return jnp.einsum("dkgcn,gh->dkgnhc", cc, eye).reshape(nd, S5_BLOCKS, gpb * n, gpb * c).astype(BF16)

    def decay(a):
        a = a.reshape(nd, S5_BLOCKS, gpb, n, c)[..., 0].reshape(nd, S5_BLOCKS, 1, gpb * n)
        return jnp.broadcast_to(a, (nd, S5_BLOCKS, 8, gpb * n))

    return decay(a_re), decay(a_im), diag_in(bb_re), diag_in(bb_im), diag_out(c_re), diag_out(c_im)


def _s5_kernel(uf_ref, ub_ref, are_ref, aim_ref, bre_ref, bim_ref, cre_ref, cim_ref,
               yf_ref, yb_ref, bur_scr, bui_scr, st_scr, *, steps, bsz):
    i = pl.program_id(0)

    @pl.when(i == 0)
    def _():
        st_scr[...] = jnp.zeros_like(st_scr)

    u_refs = (uf_ref, ub_ref)
    y_refs = (yf_ref, yb_ref)
    per_dir = S5_CHAINS // 2
    for k0 in range(0, S5_BLOCKS, per_dir):
        chains = [(d, k0 + j) for d in range(2) for j in range(per_dir)]
        for c, (d, k) in enumerate(chains):
            u = u_refs[d][:, k * S5_BLOCK_IN:(k + 1) * S5_BLOCK_IN].astype(BF16)
            bur_scr[c] = _dot(u, bre_ref[d, k])
            bui_scr[c] = _dot(u, bim_ref[d, k])

        def step(t, carry, chains=chains):
            out = []
            for c, (d, k) in enumerate(chains):
                sr, si = carry[2 * c], carry[2 * c + 1]
                tt = t if d == 0 else steps - 1 - t
                r0 = pl.multiple_of(tt * bsz, bsz)
                ar = are_ref[d, k]
                ai = aim_ref[d, k]
                nr = ar * sr - ai * si + bur_scr[c, pl.ds(r0, bsz), :]
                ni = ar * si + ai * sr + bui_scr[c, pl.ds(r0, bsz), :]
                bur_scr[c, pl.ds(r0, bsz), :] = nr
                bui_scr[c, pl.ds(r0, bsz), :] = ni
                out += [nr, ni]
            return tuple(out)

        init = tuple(st_scr[d, k, p] for (d, k) in chains for p in range(2))
        final = lax.fori_loop(0, steps, step, init, unroll=2)
        for c, (d, k) in enumerate(chains):
            st_scr[d, k, 0] = final[2 * c]
            st_scr[d, k, 1] = final[2 * c + 1]
            y = (_dot(bur_scr[c].astype(BF16), cre_ref[d, k]) - _dot(bui_scr[c].astype(BF16), cim_ref[d, k]))
            y_refs[d][:, k * S5_BLOCK_IN:(k + 1) * S5_BLOCK_IN] = y


def _s5_scan(u_tm, params, bsz, s_tot, n_ctx):
    a_re, a_im, bb_re, bb_im, cc_re, cc_im = params
    assert bsz == 8
    rows = S5_CHUNK * bsz
    n_chunks = s_tot // S5_CHUNK
    ctx_chunks = n_ctx // S5_CHUNK

    def bwd_chunk(i):
        return jnp.where(i < ctx_chunks, ctx_chunks - 1 - i, n_chunks - 1 + ctx_chunks - i)

    full = lambda a: pl.BlockSpec(a.shape, lambda i: (0,) * a.ndim)
    shp = jax.ShapeDtypeStruct((s_tot * bsz, SSM_WIDTH), F32)
    return pl.pallas_call(
        functools.partial(_s5_kernel, steps=S5_CHUNK, bsz=bsz),
        grid=(n_chunks,),
        in_specs=[pl.BlockSpec((rows, SSM_WIDTH), lambda i: (i, 0)),
                  pl.BlockSpec((rows, SSM_WIDTH), lambda i: (bwd_chunk(i), 0)),
                  full(a_re), full(a_im), full(bb_re), full(bb_im), full(cc_re), full(cc_im)],
        out_specs=[pl.BlockSpec((rows, SSM_WIDTH), lambda i: (i, 0)),
                   pl.BlockSpec((rows, SSM_WIDTH), lambda i: (bwd_chunk(i), 0))],
        out_shape=[shp, shp],
        scratch_shapes=[pltpu.VMEM((S5_CHAINS, rows, S5_BLOCK_STATE), F32),
                        pltpu.VMEM((S5_CHAINS, rows, S5_BLOCK_STATE), F32),
                        pltpu.VMEM((2, S5_BLOCKS, 2, bsz, S5_BLOCK_STATE), F32)],
        compiler_params=_cparams(("arbitrary",)),
        name="s5_scan",
    )(u_tm, u_tm, a_re, a_im, bb_re, bb_im, cc_re, cc_im)


def _s5_post_kernel(u_ref, yf_ref, yb_ref, d_ref, w_ref, o_ref):
    y = d_ref[...] * u_ref[...] + yf_ref[...] + yb_ref[...]
    g = _gelu(y).astype(BF16)
    gate = _sigmoid(_dot(g, w_ref[...]))
    o_ref[...] = (g.astype(F32) * gate).astype(BF16)


def _s5_post(u_tm, yf, yb, d_skip, w_glu_bf16):
    n, w = u_tm.shape
    tm = 1024
    row = pl.BlockSpec((tm, w), lambda i: (i, 0))
    return pl.pallas_call(
        _s5_post_kernel,
        grid=(n // tm,),
        in_specs=[row, row, row, pl.BlockSpec((1, w), lambda i: (0, 0)), pl.BlockSpec((w, w), lambda i: (0, 0))],
        out_specs=row,
        out_shape=jax.ShapeDtypeStruct((n, w), BF16),
        compiler_params=_cparams(("arbitrary",)),
        name="s5_post",
    )(u_tm, yf, yb, d_skip.reshape(1, w), w_glu_bf16)


def _rope_tables(lat_len):
    rows = lat_len // GRID_W
    row = jnp.repeat(jnp.arange(rows, dtype=F32), GRID_W)
    col = jnp.tile(jnp.arange(GRID_W, dtype=F32), rows)
    inv = ROPE_BASE ** (-jnp.arange(ROPE_FREQS, dtype=F32) / ROPE_FREQS)
    ang_r = row[:, None] * inv
    ang_c = col[:, None] * inv
    cos = jnp.concatenate([jnp.cos(ang_r), jnp.cos(ang_r), jnp.cos(ang_c), jnp.cos(ang_c)], axis=1)
    sin = jnp.concatenate([-jnp.sin(ang_r), jnp.sin(ang_r), -jnp.sin(ang_c), jnp.sin(ang_c)], axis=1)
    return jnp.tile(cos, (1, 2)), jnp.tile(sin, (1, 2))


def _head_rms(x, ones_bd, g):
    hi, lo = _split2(x * x)
    ms = (_dot(hi, ones_bd) + _dot(lo, ones_bd)) * (1.0 / ATT_HEAD_DIM)
    return x * lax.rsqrt(ms + EPS) * g


def _rope(x, cos, sin_signed):
    lane = lax.broadcasted_iota(jnp.int32, x.shape, 1)
    first_half = (lane % (2 * ROPE_FREQS)) < ROPE_FREQS
    partner = jnp.where(first_half, pltpu.roll(x, LANES - ROPE_FREQS, 1), pltpu.roll(x, ROPE_FREQS, 1))
    return x * cos + partner * sin_signed


def _qkprep_kernel(q_ref, kv_ref, cos_ref, sin_ref, qg_ref, kg_ref, qc_ref, ql_ref, ko_ref, vo_ref, *, n_ctx, chunk):
    s_tot = q_ref.shape[0]
    r_i = lax.broadcasted_iota(jnp.int32, (LANES, LANES), 0) // ATT_HEAD_DIM
    c_i = lax.broadcasted_iota(jnp.int32, (LANES, LANES), 1) // ATT_HEAD_DIM
    ones_bd = jnp.where(r_i == c_i, 1.0, 0.0).astype(BF16)
    lane = lax.broadcasted_iota(jnp.int32, (chunk, LANES), 1)
    low = lane < ATT_HEAD_DIM
    for r0 in range(0, s_tot, chunk):
        roped = r0 >= n_ctx
        if roped:
            cos = cos_ref[r0 - n_ctx:r0 - n_ctx + chunk, :]
            sin = sin_ref[r0 - n_ctx:r0 - n_ctx + chunk, :]
        for s in range(ATT_WIDTH // LANES):
            x = _head_rms(q_ref[r0:r0 + chunk, s * LANES:(s + 1) * LANES], ones_bd, qg_ref[...])
            if roped:
                x = _rope(x, cos, sin)
            qo_ref, q0 = (ql_ref, r0 - n_ctx) if roped else (qc_ref, r0)
            qo_ref[q0:q0 + chunk, s * LANES:(s + 1) * LANES] = (x * (ATT_HEAD_DIM ** -0.5)).astype(BF16)
        k = _head_rms(kv_ref[r0:r0 + chunk, 0:LANES], ones_bd, kg_ref[...])
        if roped:
            k = _rope(k, cos, sin)
        v = kv_ref[r0:r0 + chunk, LANES:2 * LANES]
        k_sw = pltpu.roll(k, ATT_HEAD_DIM, 1)
        v_sw = pltpu.roll(v, ATT_HEAD_DIM, 1)
        zero = jnp.zeros_like(k)
        ks = (jnp.where(low, k, zero), jnp.where(low, zero, k_sw), jnp.where(low, k_sw, zero), jnp.where(low, zero, k))
        vs = (jnp.where(low, v, zero), jnp.where(low, zero, v_sw), jnp.where(low, v_sw, zero), jnp.where(low, zero, v))
        for n in range(4):
            ko_ref[0, n, r0:r0 + chunk, :] = ks[n].astype(BF16)
            vo_ref[0, n, r0:r0 + chunk, :] = vs[n].astype(BF16)


def _qkprep(proj, cos, sin, q_g, k_g, bsz, s_tot, n_ctx):
    qg = jnp.tile(q_g, 2).reshape(1, LANES)
    kg = jnp.tile(k_g, 2).reshape(1, LANES)
    kv_shape = jax.ShapeDtypeStruct((bsz, 4, s_tot, LANES), BF16)
    kv_spec = pl.BlockSpec((1, 4, s_tot, LANES), lambda b: (b, 0, 0, 0))
    return pl.pallas_call(
        functools.partial(_qkprep_kernel, n_ctx=n_ctx, chunk=256),
        grid=(bsz,),
        in_specs=[pl.BlockSpec((s_tot, PROJ_BLOCK), lambda b: (b, COL_QA // PROJ_BLOCK)),
                  pl.BlockSpec((s_tot, PROJ_BLOCK), lambda b: (b, COL_KA // PROJ_BLOCK)),
                  pl.BlockSpec(cos.shape, lambda b: (0, 0)), pl.BlockSpec(sin.shape, lambda b: (0, 0)),
                  pl.BlockSpec((1, LANES), lambda b: (0, 0)), pl.BlockSpec((1, LANES), lambda b: (0, 0))],
        out_specs=[pl.BlockSpec((n_ctx, ATT_WIDTH), lambda b: (b, 0)),
                   pl.BlockSpec((s_tot - n_ctx, ATT_WIDTH), lambda b: (b, 0)), kv_spec, kv_spec],
        out_shape=[jax.ShapeDtypeStruct((bsz * n_ctx, ATT_WIDTH), BF16),
                   jax.ShapeDtypeStruct((bsz * (s_tot - n_ctx), ATT_WIDTH), BF16), kv_shape, kv_shape],
        compiler_params=_cparams(("arbitrary",)),
        name="qk_prep",
    )(proj, proj, cos, sin, qg, kg)


def _attend(q_ref, k_ref, v_ref, o_ref, n_keys):
    tq = q_ref.shape[0]
    for hk in range(ATT_KV_HEADS):
        qs = jnp.concatenate([q_ref[:, (2 * hk) * LANES:(2 * hk + 1) * LANES],
                              q_ref[:, (2 * hk + 1) * LANES:(2 * hk + 2) * LANES]], axis=0)
        acc = jnp.zeros((2 * tq, LANES), F32)
        for p in range(2):
            s = _dot_nt(qs, k_ref[0, 2 * hk + p, 0:n_keys, :])
            m = jnp.max(s, axis=-1, keepdims=True)
            e = jnp.exp(s - m)
            l = jnp.sum(e, axis=-1, keepdims=True)
            acc = acc + _dot(e.astype(BF16), v_ref[0, 2 * hk + p, 0:n_keys, :]) / l
        o_ref[:, (2 * hk) * LANES:(2 * hk + 1) * LANES] = acc[0:tq].astype(BF16)
        o_ref[:, (2 * hk + 1) * LANES:(2 * hk + 2) * LANES] = acc[tq:2 * tq].astype(BF16)


def _attn_kernel(q_ref, k_ref, v_ref, o_ref, *, n_keys):
    _attend(q_ref, k_ref, v_ref, o_ref, n_keys)


def _attention(q, kn, vn, bsz, s_tot, tq, n_keys):
    blocks = q.shape[0] // bsz // tq
    kv_spec = pl.BlockSpec((1, 4, s_tot, LANES), lambda b, i: (b, 0, 0, 0))
    qo_spec = pl.BlockSpec((tq, ATT_WIDTH), lambda b, i: (b * blocks + i, 0))
    return pl.pallas_call(
        functools.partial(_attn_kernel, n_keys=n_keys),
        grid=(bsz, blocks),
        in_specs=[qo_spec, kv_spec, kv_spec],
        out_specs=qo_spec,
        out_shape=jax.ShapeDtypeStruct(q.shape, BF16),
        compiler_params=_cparams(("arbitrary", "arbitrary")),
        name="attention",
    )(q, kn, vn)


def _log_sigmoid(x):
    return jnp.minimum(x, 0.0) - jnp.log(1.0 + jnp.exp(-jnp.abs(x)))


def _mlstm_kernel(qf_ref, kf_ref, vf_ref, gf_ref, qb_ref, kb_ref, vb_ref, gb_ref, bias_ref,
                  hf_ref, hb_ref, c_scr, n_scr, m_scr):
    i = pl.program_id(1)
    t = ML_CHUNK

    @pl.when(i == 0)
    def _():
        c_scr[...] = jnp.zeros_like(c_scr)
        n_scr[...] = jnp.zeros_like(n_scr)
        m_scr[...] = jnp.zeros_like(m_scr)

    r_i = lax.broadcasted_iota(jnp.int32, (t, t), 0)
    c_i = lax.broadcasted_iota(jnp.int32, (t, t), 1)
    lower = r_i >= c_i
    upper = r_i <= c_i
    lower_m = jnp.where(lower, 1.0, 0.0).astype(BF16)
    upper_m = jnp.where(upper, 1.0, 0.0).astype(BF16)

    for d in range(2):
        q_ref, k_ref, v_ref, g_ref, h_ref = ((qf_ref, kf_ref, vf_ref, gf_ref, hf_ref),
                                             (qb_ref, kb_ref, vb_ref, gb_ref, hb_ref))[d]
        g = g_ref[...] + bias_ref[...]
        g_t = g.T
        lf = _log_sigmoid(g)
        lf_t = _log_sigmoid(g_t)
        causal, causal_m, anti_m = (lower, lower_m, upper_m) if d == 0 else (upper, upper_m, lower_m)
        b_cols = _dot_exact_lhs(causal_m, lf)
        b_rows = _dot_exact_rhs(lf_t, anti_m)
        last = t - 1 if d == 0 else 0
        for h in range(ML_HEADS):
            ci = d * 2 * ML_HEADS + h
            cf = ci + ML_HEADS
            i_col = g[:, ci:ci + 1]
            i_row = g_t[ci:ci + 1, :]
            b_col = b_cols[:, cf:cf + 1]
            b_row = b_rows[cf:cf + 1, :]
            b_last = b_cols[last:last + 1, cf:cf + 1]
            m_prev = m_scr[d, h]
            c_prev = c_scr[d, h]
            n_prev = n_scr[d, h]

            q = q_ref[:, h * ML_HEAD_DIM:(h + 1) * ML_HEAD_DIM]
            k = k_ref[:, h * ML_HEAD_DIM:(h + 1) * ML_HEAD_DIM] * (ML_HEAD_DIM ** -0.5)
            v = v_ref[:, h * ML_HEAD_DIM:(h + 1) * ML_HEAD_DIM]
            q16, k16, v16 = q.astype(BF16), k.astype(BF16), v.astype(BF16)

            dmat = jnp.where(causal, b_col - b_row + i_row, NEG_INF)
            inter = b_col + m_prev
            m_t = jnp.maximum(inter, jnp.max(dmat, axis=-1, keepdims=True))
            w = jnp.exp(dmat - m_t)
            a_inter = jnp.exp(inter - m_t)
            qk = _dot_nt(q16, k16) * w
            num = a_inter * _dot(q16, c_prev.astype(BF16)) + _dot(qk.astype(BF16), v16)
            den = a_inter * jnp.sum(q * n_prev, axis=-1, keepdims=True) + jnp.sum(qk, axis=-1, keepdims=True)
            h_ref[:, h * ML_HEAD_DIM:(h + 1) * ML_HEAD_DIM] = num / jnp.maximum(jnp.abs(den), jnp.exp(-m_t))

            d_last_col = b_last - b_col + i_col
            m_new = jnp.maximum(b_last + m_prev, jnp.max(d_last_col, axis=0, keepdims=True))
            w_last = jnp.exp(d_last_col - m_new)
            decay = jnp.exp(b_last + m_prev - m_new)
            kw = k * w_last
            c_scr[d, h] = decay * c_prev + _dot(kw.T.astype(BF16), v16)
            n_scr[d, h] = decay * n_prev + jnp.sum(kw, axis=0, keepdims=True)
            m_scr[d, h] = m_new


def _mlstm(proj, gate_b, bsz, s_tot, n_ctx):
    t = ML_CHUNK
    n_chunks = s_tot // t
    ctx_chunks = n_ctx // t

    def fwd(b, i):
        return b * n_chunks + i

    def bwd(b, i):
        return b * n_chunks + jnp.where(i < ctx_chunks, ctx_chunks - 1 - i, n_chunks - 1 + ctx_chunks - i)

    def col(c0, width):
        return c0 // width

    def specs(rowfn):
        return [pl.BlockSpec((t, ML_WIDTH), lambda b, i: (rowfn(b, i), col(COL_QM, ML_WIDTH))),
                pl.BlockSpec((t, ML_WIDTH), lambda b, i: (rowfn(b, i), col(COL_KM, ML_WIDTH))),
                pl.BlockSpec((t, ML_WIDTH), lambda b, i: (rowfn(b, i), col(COL_VM, ML_WIDTH))),
                pl.BlockSpec((t, LANES), lambda b, i: (rowfn(b, i), col(COL_GM, LANES)))]

    bias = jnp.zeros((1, LANES), F32).at[0, :ML_GATES].set(gate_b.reshape(ML_GATES))
    shp = jax.ShapeDtypeStruct((bsz * s_tot, ML_WIDTH), F32)
    return pl.pallas_call(
        _mlstm_kernel,
        grid=(bsz, n_chunks),
        in_specs=specs(fwd) + specs(bwd) + [pl.BlockSpec((1, LANES), lambda b, i: (0, 0))],
        out_specs=[pl.BlockSpec((t, ML_WIDTH), lambda b, i: (fwd(b, i), 0)),
                   pl.BlockSpec((t, ML_WIDTH), lambda b, i: (bwd(b, i), 0))],
        out_shape=[shp, shp],
        scratch_shapes=[pltpu.VMEM((2, ML_HEADS, ML_HEAD_DIM, ML_HEAD_DIM), F32),
                        pltpu.VMEM((2, ML_HEADS, 1, ML_HEAD_DIM), F32),
                        pltpu.VMEM((2, ML_HEADS, 1, 1), F32)],
        compiler_params=_cparams(("arbitrary", "arbitrary")),
        name="mlstm",
    )(*([proj] * 8), bias)


def _merge_kernel(ys_ref, ya_ref, hf_ref, hb_ref, om_ref, gl_ref, x_ref, mod_ref, mg_ref, n2_ref,
                  wbs_ref, wba_ref, wbm_ref, wo_ref, xo_ref, h2_ref, *, blocks_per_batch, first_block):
    n = pl.program_id(0)
    b = n // (blocks_per_batch - first_block)
    if not first_block:
        b = jnp.where(n % blocks_per_batch == 0, CTX_MOD_ROW, b)
    is_ctx = None

    hs = hf_ref[...] + hb_ref[...]
    parts = []
    for h in range(ML_HEADS):
        sl = slice(h * ML_HEAD_DIM, (h + 1) * ML_HEAD_DIM)
        parts.append(_rms(hs[:, sl], mg_ref[:, sl]))
    hn = jnp.concatenate(parts, axis=1) * _sigmoid(om_ref[...])

    gl = gl_ref[...]
    merged = (_sigmoid(gl[:, 0:D_MODEL]) * _dot(ys_ref[...], wbs_ref[...])
              + _sigmoid(gl[:, D_MODEL:2 * D_MODEL]) * _dot(ya_ref[...], wba_ref[...])
              + _sigmoid(gl[:, 2 * D_MODEL:3 * D_MODEL]) * _dot(hn.astype(BF16), wbm_ref[...]))
    mix = _dot(merged.astype(BF16), wo_ref[...])
    g1 = _mod_rows(mod_ref, 2, b, is_ctx)
    x = x_ref[...] + g1 * mix
    xo_ref[...] = x
    sh2 = _mod_rows(mod_ref, 3, b, is_ctx)
    sc2 = _mod_rows(mod_ref, 4, b, is_ctx)
    h2_ref[...] = (_rms(x, n2_ref[...]) * (1.0 + sc2) + sh2).T.astype(BF16)


def _merge(ys, ya, hf, hb, proj, stream, mod_l, ml_norm_g, norm2_g, wbs, wba, wbm, wo,
           bsz, s_tot, n_ctx, latent_only):
    tm = n_ctx
    bpb = s_tot // tm
    first = 1 if latent_only else 0
    per = bpb - first

    def rows(n):
        return (n // per) * bpb + n % per + first

    n_blocks = bsz * per
    full = lambda a: pl.BlockSpec(a.shape, lambda n: (0,) * a.ndim)
    wide = lambda w: pl.BlockSpec((tm, w), lambda n: (rows(n), 0))
    mg = ml_norm_g.reshape(1, ML_WIDTH)
    n2 = norm2_g.reshape(1, D_MODEL)
    return pl.pallas_call(
        functools.partial(_merge_kernel, blocks_per_batch=bpb, first_block=first),
        grid=(n_blocks,),
        in_specs=[wide(SSM_WIDTH), pl.BlockSpec((tm, ATT_WIDTH), lambda n: (n, 0)), wide(ML_WIDTH), wide(ML_WIDTH),
                  pl.BlockSpec((tm, ML_WIDTH), lambda n: (rows(n), COL_OM // ML_WIDTH)),
                  pl.BlockSpec((tm, N_BRANCH * D_MODEL), lambda n: (rows(n), COL_GATE // (N_BRANCH * D_MODEL))),
                  wide(D_MODEL), full(mod_l), full(mg), full(n2), full(wbs), full(wba), full(wbm), full(wo)],
        out_specs=[pl.BlockSpec((tm, D_MODEL), lambda n: (n, 0)), pl.BlockSpec((D_MODEL, tm), lambda n: (0, n))],
        out_shape=[jax.ShapeDtypeStruct((n_blocks * tm, D_MODEL), F32),
                   jax.ShapeDtypeStruct((D_MODEL, n_blocks * tm), BF16)],
        compiler_params=_cparams(("arbitrary",)),
        name="merge",
    )(ys, ya, hf, hb, proj, proj, stream, mod_l, mg, n2, wbs, wba, wbm, wo)


def _peer_score_kernel(xt_ref, wq_ref, k1_ref, k2_ref, s1_ref, s2_ref):
    half = PEER_QDIM // 2
    q_t = _dot(wq_ref[...], xt_ref[...])
    for key_ref, s_ref, lo in ((k1_ref, s1_ref, 0), (k2_ref, s2_ref, half)):
        khi, klo = _split2(key_ref[...])
        for h in range(PEER_HEADS):
            qhi, qlo = _split2(q_t[h * PEER_QDIM + lo:h * PEER_QDIM + lo + half, :])
            s_ref[h] = _dot(khi, qhi) + _dot(khi, qlo) + _dot(klo, qhi)


def _peer_scores(h2_t, wq_t, k1, k2):
    n = h2_t.shape[1]
    tm = 1024
    shp = jax.ShapeDtypeStruct((PEER_HEADS, PEER_KEYS, n), F32)
    out = pl.BlockSpec((PEER_HEADS, PEER_KEYS, tm), lambda i: (0, 0, i))
    key = pl.BlockSpec((PEER_KEYS, PEER_QDIM // 2), lambda i: (0, 0))
    return pl.pallas_call(
        _peer_score_kernel,
        grid=(n // tm,),
        in_specs=[pl.BlockSpec((D_MODEL, tm), lambda i: (0, i)), pl.BlockSpec(wq_t.shape, lambda i: (0, 0)), key, key],
        out_specs=[out, out],
        out_shape=[shp, shp],
        compiler_params=_cparams(("arbitrary",)),
        name="peer_scores",
    )(h2_t, wq_t, k1, k2)


def _top_values(s, k):
    work = s
    rank = jnp.full(s.shape, float(PEER_KEYS), F32)
    vals = []
    for r in range(k):
        m = jnp.max(work, axis=0, keepdims=True)
        hit = work == m
        rank = jnp.where(hit, float(r), rank)
        work = jnp.where(hit, NEG_INF, work)
        vals.append(m)
    return vals, rank


def _peer_select_kernel(s1_ref, s2_ref, r2_ref, g2_ref, cnt_ref, e1_ref, v1_scr, v2_scr):
    k = PEER_TOPK
    half = k // 2
    s1 = s1_ref[0]
    s2 = s2_ref[0]
    v1, rank1 = _top_values(s1, k)
    v2, rank2 = _top_values(s2, k)
    for a in range(k):
        v1_scr[a:a + 1, :] = v1[a]
        v2_scr[a:a + 1, :] = v2[a]
    row = lax.broadcasted_iota(jnp.int32, (half, s1.shape[1]), 0)
    pieces = [v1[0] + v2_scr[...]]
    for a in range(1, half):
        pieces.append(jnp.where(row < k // (a + 1), v1[a] + v2_scr[0:half, :], NEG_INF))
    pieces.append(v1_scr[half:k, :] + v2[0])
    top = v1[0] + v2[0]
    z = jnp.zeros_like(top)
    thr = top
    for _ in range(k):
        thr = functools.reduce(jnp.maximum, [jnp.max(p, axis=0, keepdims=True) for p in pieces])
        pieces = [jnp.where(p == thr, NEG_INF, p) for p in pieces]
        z = z + jnp.exp(thr - top)
    cnt = jnp.zeros_like(s1)
    for a in range(k):
        if a == 0:
            sums = v1[0] + v2_scr[...]
        elif a < half:
            sums = jnp.where(row < k // (a + 1), v1[a] + v2_scr[0:half, :], NEG_INF)
        else:
            sums = v1[a] + v2[0]
        n_sel = jnp.sum(jnp.where(sums >= thr, 1.0, 0.0), axis=0, keepdims=True)
        cnt = jnp.where(rank1 == float(a), n_sel, cnt)
    r2_ref[0] = rank2.astype(BF16)
    g2_ref[0] = jnp.exp(s2 - v2[0]).astype(BF16)
    cnt_ref[0] = cnt
    e1_ref[0] = jnp.exp(s1 - v1[0]) / z


def _peer_select(s1, s2):
    heads, keys, n = s1.shape
    tl = 512
    spec = pl.BlockSpec((1, keys, tl), lambda h, i: (h, 0, i))
    shp = jax.ShapeDtypeStruct((heads, keys, n), F32)
    shp16 = jax.ShapeDtypeStruct((heads, keys, n), BF16)
    return pl.pallas_call(
        _peer_select_kernel,
        grid=(heads, n // tl),
        in_specs=[spec, spec], out_specs=[spec] * 4, out_shape=[shp16, shp16, shp, shp],
        scratch_shapes=[pltpu.VMEM((PEER_TOPK, tl), F32), pltpu.VMEM((PEER_TOPK, tl), F32)],
        compiler_params=_cparams(("arbitrary", "arbitrary")),
        name="peer_select",
    )(s1, s2)


def _peer_dense_kernel(x_ref, u_ref, vt_ref, r2_ref, g2_ref, cnt_ref, e1_ref, s_ref, mod_ref, fg_ref,
                       o_ref, acc_scr, a_scr, w_scr, coef_even, coef_odd, x_scr, r2_scr, g2_scr, row_scr,
                       *, te, nj, blocks_per_batch, n_ctx, final):
    n = pl.program_id(0)
    tm = x_ref.shape[1]
    n_tiles = tm // PEER_TOK
    keys_per_step = te // PEER_KEYS
    assert keys_per_step % F32_SUBLANES == 0

    @pl.when(n == 0)
    def _():
        acc_scr[...] = jnp.zeros_like(acc_scr)
        coef_odd[...] = jnp.zeros_like(coef_odd)

    @pl.when(n % nj == 0)
    def _():
        for t in range(n_tiles):
            tok = slice(t * PEER_TOK, (t + 1) * PEER_TOK)
            x_scr[t] = x_ref[:, tok]
            r2_scr[t] = r2_ref[:, :, tok]
            g2_scr[t] = g2_ref[:, :, tok]

    for t in range(n_tiles):
        tok = slice(t * PEER_TOK, (t + 1) * PEER_TOK)
        row_scr[t, 0] = cnt_ref[:, :, tok]
        row_scr[t, 1] = e1_ref[:, :, tok]

    def step(read_ref, write_ref):
        zero = jnp.zeros((PEER_KEYS, LANES), BF16)

        def tile(t, carry):
            a_scr[...] = _dot(pltpu.bitcast(u_ref[...], BF16), x_scr[t]).astype(BF16)
            acc_scr[t] += _dot(pltpu.bitcast(vt_ref[...], BF16), read_ref[t])
            for s in range(keys_per_step):
                rows = slice(s * PEER_KEYS, (s + 1) * PEER_KEYS)
                for l in range(PEER_TOK // LANES):
                    lanes = slice(l * LANES, (l + 1) * LANES)
                    w = zero
                    for h in range(PEER_HEADS):
                        cnt = row_scr[t, 0, h, s:s + 1, lanes].astype(BF16)
                        e1 = row_scr[t, 1, h, s:s + 1, lanes].astype(BF16)
                        w = w + jnp.where(r2_scr[t, h, :, lanes] < cnt, g2_scr[t, h, :, lanes], zero) * e1
                    w_scr[rows, lanes] = w
            for s in range(keys_per_step):
                rows = slice(s * PEER_KEYS, (s + 1) * PEER_KEYS)
                write_ref[t, rows, :] = w_scr[rows, :] * _gelu(a_scr[rows, :])
            return carry

        lax.fori_loop(0, n_tiles, tile, 0)

    @pl.when(n % 2 == 0)
    def _():
        step(coef_odd, coef_even)

    @pl.when(n % 2 == 1)
    def _():
        step(coef_even, coef_odd)

    prev = n - 1

    @pl.when(jnp.logical_and(n > 0, prev % nj == nj - 1))
    def _():
        ip = prev // nj
        b = ip // blocks_per_batch
        for t in range(n_tiles):
            tok = slice(t * PEER_TOK, (t + 1) * PEER_TOK)
            if n_ctx:
                row = (ip % blocks_per_batch) * tm + t * PEER_TOK + lax.broadcasted_iota(jnp.int32, (PEER_TOK, 1), 0)
                is_ctx = row < n_ctx
            else:
                is_ctx = None
            g2 = _mod_rows(mod_ref, 5, b, is_ctx)
            x = s_ref[tok, :] + g2 * acc_scr[t].T
            if final:
                x = _rms(x, fg_ref[...])
            o_ref[tok, :] = x
        acc_scr[...] = jnp.zeros_like(acc_scr)


def _pack_tables_kernel(u_ref, v_ref, uo_ref, vo_ref):
    uo_ref[...] = pltpu.bitcast(u_ref[...].astype(BF16), F32)
    vo_ref[...] = pltpu.bitcast(v_ref[...].T.astype(BF16), F32)


def _pack_tables(u, v):
    e, d = u.shape
    te = 512
    return pl.pallas_call(
        _pack_tables_kernel,
        grid=(e // te,),
        in_specs=[pl.BlockSpec((te, d), lambda j: (j, 0)), pl.BlockSpec((te, d), lambda j: (j, 0))],
        out_specs=[pl.BlockSpec((te // 2, d), lambda j: (j, 0)), pl.BlockSpec((d // 2, te), lambda j: (0, j))],
        out_shape=[jax.ShapeDtypeStruct((e // 2, d), F32), jax.ShapeDtypeStruct((d // 2, e), F32)],
        compiler_params=_cparams(("arbitrary",)),
        name="pack_tables",
    )(u, v)


def _peer_dense(h2_t, u_words, vt_words, sel, stream, mod_l, final_g, rows_per_batch, n_ctx, final):
    n = h2_t.shape[1]
    r2, g2, cnt, e1 = sel
    tm = 768 if rows_per_batch % 768 == 0 else 512
    te = 1024 if tm == 768 else 2048
    bpb = rows_per_batch // tm
    nj = PEER_EXPERTS // te
    n_steps = (n // tm) * nj

    def cur(s):
        c = jnp.minimum(s, n_steps - 1)
        return c // nj, c % nj

    def prev(s):
        p = jnp.maximum(s - 1, 0)
        return p // nj, p % nj

    tok = pl.BlockSpec((PEER_HEADS, PEER_KEYS, tm), lambda s: (0, 0, cur(s)[0]))
    sub = pl.BlockSpec((PEER_HEADS, te // PEER_KEYS, tm), lambda s: (0, cur(s)[1], cur(s)[0]))
    full = lambda a: pl.BlockSpec(a.shape, lambda s: (0,) * a.ndim)
    fg = final_g.reshape(1, D_MODEL)
    n_tiles = tm // PEER_TOK
    coef = pltpu.VMEM((n_tiles, te, PEER_TOK), BF16)
    sel16 = pltpu.VMEM((n_tiles, PEER_HEADS, PEER_KEYS, PEER_TOK), BF16)
    return pl.pallas_call(
        functools.partial(_peer_dense_kernel, te=te, nj=nj, blocks_per_batch=bpb, n_ctx=n_ctx, final=final),
        grid=(n_steps + 1,),
        in_specs=[pl.BlockSpec((D_MODEL, tm), lambda s: (0, cur(s)[0])),
                  pl.BlockSpec((te // 2, D_MODEL), lambda s: (cur(s)[1], 0)),
                  pl.BlockSpec((D_MODEL // 2, te), lambda s: (0, prev(s)[1])),
                  tok, tok, sub, sub,
                  pl.BlockSpec((tm, D_MODEL), lambda s: (prev(s)[0], 0)), full(mod_l), full(fg)],
        out_specs=pl.BlockSpec((tm, D_MODEL), lambda s: (prev(s)[0], 0)),
        out_shape=jax.ShapeDtypeStruct((n, D_MODEL), F32),
        scratch_shapes=[pltpu.VMEM((n_tiles, D_MODEL, PEER_TOK), F32), pltpu.VMEM((te, PEER_TOK), BF16),
                        pltpu.VMEM((te, PEER_TOK), BF16), coef, coef,
                        pltpu.VMEM((n_tiles, D_MODEL, PEER_TOK), BF16), sel16, sel16,
                        pltpu.VMEM((n_tiles, 2, PEER_HEADS, te // PEER_KEYS, PEER_TOK), F32)],
        compiler_params=_cparams(("arbitrary",)),
        name="peer_dense",
    )(h2_t, u_words, vt_words, r2, g2, cnt, e1, stream, mod_l, fg)


def _reorder_w_in(w):
    offs = [0]
    for width in (SSM_WIDTH, ATT_WIDTH, KV_WIDTH, KV_WIDTH, ML_WIDTH, ML_WIDTH, ML_WIDTH, ML_WIDTH, ML_GATES,
                  N_BRANCH * D_MODEL):
        offs.append(offs[-1] + width)
    u_s, q_a, k_a, v_a, q_m, k_m, v_m, o_m, g_m, gate = (w[:, offs[n]:offs[n + 1]] for n in range(10))
    pad = jnp.zeros((w.shape[0], PROJ_WIDTH - COL_GM - ML_GATES), w.dtype)
    return jnp.concatenate([u_s, q_a, q_m, k_m, v_m, o_m, gate, k_a, v_a, g_m, pad], axis=1).astype(BF16)


def kernel(x, c, ctx, c_ctx, w_mod, b_mod, norm1_g, norm2_g, w_in, ssm_lam_re, ssm_lam_im, ssm_log_step, ssm_b_re, ssm_b_im, ssm_c_re, ssm_c_im, ssm_d, ssm_w_glu, attn_q_norm_g, attn_k_norm_g, mlstm_gate_b, mlstm_norm_g, w_branch_ssm, w_branch_attn, w_branch_mlstm, w_out, peer_w_q, peer_sub_k1, peer_sub_k2, peer_u, peer_v, final_norm_g):
    bsz, lat_len, d = x.shape
    n_ctx = ctx.shape[1]
    s_tot = n_ctx + lat_len
    depth = w_in.shape[0]
    assert d == D_MODEL and n_ctx == 256 and lat_len % n_ctx == 0 and bsz == 8

    cos, sin = _rope_tables(lat_len)
    mod = _modulation(c, c_ctx, w_mod, b_mod)
    stream = jnp.concatenate([ctx, x], axis=1)

    for layer in range(depth):
        last = layer == depth - 1
        mod_l = mod[layer]
        proj = _inproj(stream.reshape(bsz, s_tot, d), norm1_g[layer], mod_l, _reorder_w_in(w_in[layer]), n_ctx)

        u_tm = proj[:, COL_U:COL_U + SSM_WIDTH].reshape(bsz, s_tot, SSM_WIDTH).transpose(1, 0, 2)
        u_tm = u_tm.reshape(s_tot * bsz, SSM_WIDTH)
        params = _s5_params(ssm_lam_re[layer], ssm_lam_im[layer], ssm_log_step[layer], ssm_b_re[layer],
                            ssm_b_im[layer], ssm_c_re[layer], ssm_c_im[layer])
        yf, yb = _s5_scan(u_tm, params, bsz, s_tot, n_ctx)
        ys_tm = _s5_post(u_tm, yf, yb, ssm_d[layer], ssm_w_glu[layer].astype(BF16))
        ys = ys_tm.reshape(s_tot, bsz, SSM_WIDTH).transpose(1, 0, 2).reshape(bsz * s_tot, SSM_WIDTH)

        q_ctx, q_lat, kn, vn = _qkprep(proj, cos, sin, attn_q_norm_g[layer], attn_k_norm_g[layer], bsz, s_tot, n_ctx)
        ya = _attention(q_lat, kn, vn, bsz, s_tot, ATT_Q_BLOCK, s_tot)
        if not last:
            ya_ctx = _attention(q_ctx, kn, vn, bsz, s_tot, n_ctx, n_ctx)
            ya = jnp.concatenate([ya_ctx.reshape(bsz, n_ctx, ATT_WIDTH), ya.reshape(bsz, lat_len, ATT_WIDTH)],
                                 axis=1).reshape(bsz * s_tot, ATT_WIDTH)

        hf, hb = _mlstm(proj, mlstm_gate_b[layer], bsz, s_tot, n_ctx)

        stream2, h2_t = _merge(ys, ya, hf, hb, proj, stream.reshape(bsz * s_tot, d), mod_l, mlstm_norm_g[layer],
                             norm2_g[layer], w_branch_ssm[layer].astype(BF16), w_branch_attn[layer].astype(BF16),
                             w_branch_mlstm[layer].astype(BF16), w_out[layer].astype(BF16),
                             bsz, s_tot, n_ctx, last)

        s1, s2 = _peer_scores(h2_t, peer_w_q[layer].T.astype(BF16), peer_sub_k1[layer], peer_sub_k2[layer])
        sel = _peer_select(s1, s2)
        rows_per_batch = lat_len if last else s_tot
        u_words, vt_words = _pack_tables(peer_u[layer], peer_v[layer])
        stream = _peer_dense(h2_t, u_words, vt_words, sel, stream2, mod_l,
                             final_norm_g, rows_per_batch, 0 if last else n_ctx, last)

    return stream.reshape(bsz, lat_len, d)
```

```python
import functools
import math

import jax
import jax.numpy as jnp
from jax import lax
from jax.experimental import pallas as pl
from jax.experimental.pallas import tpu as pltpu

F32 = jnp.float32
BF16 = jnp.bfloat16

D_MODEL = 1024
GRID_W = 64
EPS = 1e-6
N_MOD = 6
N_BRANCH = 3
SSM_WIDTH = 512
SSM_GROUP = 16
SSM_GROUPS = 32
SSM_STATE = 64
ATT_HEADS = 8
ATT_KV_HEADS = 2
ATT_HEAD_DIM = 64
ATT_WIDTH = 512
ATT_Q_BLOCK = 512
KV_WIDTH = 128
ROPE_FREQS = 16
ROPE_BASE = 10000.0
ML_HEADS = 4
ML_HEAD_DIM = 128
ML_WIDTH = 512
ML_GATES = 16
ML_CHUNK = 256
PEER_HEADS = 8
PEER_KEYS = 128
PEER_EXPERTS = PEER_KEYS * PEER_KEYS
PEER_QDIM = 256
PEER_TOPK = 16
PEER_TOK = 256

LANES = 128
F32_SUBLANES = 8
BF16_SUBLANES = 16
VMEM_LIMIT_BYTES = 56 * 1024 * 1024

PROJ_BLOCK = 512
COL_U, COL_QA, COL_QM, COL_KM, COL_VM, COL_OM = (i * PROJ_BLOCK for i in range(6))
COL_GATE = 6 * PROJ_BLOCK
COL_KA = COL_GATE + N_BRANCH * D_MODEL
COL_VA = COL_KA + KV_WIDTH
COL_GM = COL_VA + KV_WIDTH
PROJ_WIDTH = COL_KA + PROJ_BLOCK

S5_BLOCKS = 4
S5_BLOCK_IN = SSM_WIDTH // S5_BLOCKS
S5_BLOCK_STATE = SSM_GROUPS * SSM_STATE // S5_BLOCKS
S5_CHUNK = 128
S5_CHAINS = 4

NEG_INF = float("-inf")


def _cparams(sem, flags=None):
    return pltpu.CompilerParams(dimension_semantics=sem, vmem_limit_bytes=VMEM_LIMIT_BYTES, flags=flags)


def _split2(x):
    hi = x.astype(BF16)
    lo = (x - hi.astype(F32)).astype(BF16)
    return hi, lo


def _split3(x):
    hi = x.astype(BF16)
    r = x - hi.astype(F32)
    mid = r.astype(BF16)
    lo = (r - mid.astype(F32)).astype(BF16)
    return hi, mid, lo


def _dot(a, b):
    return jnp.dot(a, b, preferred_element_type=F32)


def _dot_nt(a, b):
    return lax.dot_general(a, b, (((1,), (1,)), ((), ())), preferred_element_type=F32)


def _dot_exact_rhs(x, m_bf16):
    hi, mid, lo = _split3(x)
    return _dot(hi, m_bf16) + _dot(mid, m_bf16) + _dot(lo, m_bf16)


def _dot_exact_lhs(m_bf16, x):
    hi, mid, lo = _split3(x)
    return _dot(m_bf16, hi) + _dot(m_bf16, mid) + _dot(m_bf16, lo)


def _sigmoid(x):
    return 1.0 / (1.0 + jnp.exp(-x))


def _gelu(x):
    return jax.nn.gelu(x, approximate=True)


def _mod_kernel(v_ref, w_ref, b_ref, o_ref):
    v = v_ref[...]
    sv = v * _sigmoid(v)
    w = w_ref[0]
    hi, mid, lo = _split3(sv)
    whi, wlo = _split2(w)
    acc = _dot(hi, whi) + _dot(mid, whi) + _dot(hi, wlo) + _dot(lo, whi) + _dot(mid, wlo)
    o_ref[0] = acc + b_ref[0]


def _modulation(c, c_ctx, w_mod, b_mod):
    depth = w_mod.shape[0]
    n_out = w_mod.shape[2]
    rows = 16
    v = jnp.zeros((rows, D_MODEL), F32).at[: c.shape[0]].set(c).at[8].set(c_ctx)
    tn = 1536
    return pl.pallas_call(
        _mod_kernel,
        grid=(depth, n_out // tn),
        in_specs=[pl.BlockSpec((rows, D_MODEL), lambda l, j: (0, 0)),
                  pl.BlockSpec((1, D_MODEL, tn), lambda l, j: (l, 0, j)),
                  pl.BlockSpec((1, 1, tn), lambda l, j: (l, 0, j))],
        out_specs=pl.BlockSpec((1, rows, tn), lambda l, j: (l, 0, j)),
        out_shape=jax.ShapeDtypeStruct((depth, rows, n_out), F32),
        compiler_params=_cparams(("arbitrary", "arbitrary")),
        name="modulation",
    )(v, w_mod, b_mod.reshape(depth, 1, n_out))


CTX_MOD_ROW = 8


def _mod_rows(mod_ref, which, b, is_ctx_col):
    lo = which * D_MODEL
    m_l = mod_ref[pl.ds(b, 1), lo:lo + D_MODEL]
    if is_ctx_col is None:
        return m_l
    m_c = mod_ref[CTX_MOD_ROW:CTX_MOD_ROW + 1, lo:lo + D_MODEL]
    return jnp.where(is_ctx_col, m_c, m_l)


def _rms(x, g):
    ms = jnp.mean(x * x, axis=-1, keepdims=True)
    return x * lax.rsqrt(ms + EPS) * g


def _inproj_kernel(x_ref, g_ref, mod_ref, w_ref, o_ref, h_scr, *, n_ctx, chunk):
    b = pl.program_id(0)
    j = pl.program_id(1)
    s_tot = x_ref.shape[1]

    @pl.when(j == 0)
    def _():
        for r0 in range(0, s_tot, chunk):
            x = x_ref[0, r0:r0 + chunk, :]
            xn = _rms(x, g_ref[...])
            row = r0 + lax.broadcasted_iota(jnp.int32, (chunk, 1), 0)
            is_ctx = row < n_ctx
            sh = _mod_rows(mod_ref, 0, b, is_ctx)
            sc = _mod_rows(mod_ref, 1, b, is_ctx)
            h_scr[r0:r0 + chunk, :] = (xn * (1.0 + sc) + sh).astype(BF16)

    o_ref[...] = _dot(h_scr[...], w_ref[...])


def _inproj(stream3, g, mod_l, w_bf16, n_ctx):
    bsz, s_tot, d = stream3.shape
    n_out = w_bf16.shape[1]
    tn = PROJ_BLOCK
    return pl.pallas_call(
        functools.partial(_inproj_kernel, n_ctx=n_ctx, chunk=256),
        grid=(bsz, n_out // tn),
        in_specs=[pl.BlockSpec((1, s_tot, d), lambda b, j: (b, 0, 0)),
                  pl.BlockSpec((1, d), lambda b, j: (0, 0)),
                  pl.BlockSpec(mod_l.shape, lambda b, j: (0, 0)),
                  pl.BlockSpec((d, tn), lambda b, j: (0, j))],
        out_specs=pl.BlockSpec((s_tot, tn), lambda b, j: (b, j)),
        out_shape=jax.ShapeDtypeStruct((bsz * s_tot, n_out), F32),
        scratch_shapes=[pltpu.VMEM((s_tot, d), BF16)],
        compiler_params=_cparams(("arbitrary", "arbitrary")),
        name="inproj",
    )(stream3, g.reshape(1, d), mod_l, w_bf16)


def _s5_param_kernel(lre_ref, lim_ref, ls_ref, bre_ref, bim_ref, are_ref, aim_ref, bbre_ref, bbim_ref):
    lre = lre_ref[...]
    lim = lim_ref[...]
    step = jnp.exp(ls_ref[...])
    mag = jnp.exp(lre * step)
    a_re = mag * jnp.cos(lim * step)
    a_im = mag * jnp.sin(lim * step)
    den = lre * lre + lim * lim
    z_re = ((a_re - 1.0) * lre + a_im * lim) / den
    z_im = (a_im * lre - (a_re - 1.0) * lim) / den
    b_re = bre_ref[...]
    b_im = bim_ref[...]
    are_ref[...] = a_re
    aim_ref[...] = a_im
    bbre_ref[...] = z_re * b_re - z_im * b_im
    bbim_ref[...] = z_re * b_im + z_im * b_re


def _s5_params(lam_re, lam_im, log_step, b_re, b_im, c_re, c_im):
    nd, g, n = lam_re.shape
    c = b_re.shape[-1]
    rows = nd * g
    wide = n * c

    def expand(z):
        return jnp.broadcast_to(z.reshape(rows, n, 1), (rows, n, c)).reshape(rows, wide)

    ls = jnp.broadcast_to(log_step.reshape(rows, 1), (rows, wide))
    spec = pl.BlockSpec((rows, wide), lambda: (0, 0))
    shp = jax.ShapeDtypeStruct((rows, wide), F32)
    a_re, a_im, bb_re, bb_im = pl.pallas_call(
        _s5_param_kernel,
        in_specs=[spec] * 5, out_specs=[spec] * 4, out_shape=[shp] * 4,
        name="s5_params",
    )(expand(lam_re), expand(lam_im), ls, b_re.reshape(rows, wide), b_im.reshape(rows, wide))

    gpb = g // S5_BLOCKS
    eye = jnp.eye(gpb, dtype=F32)

    def diag_in(bb):
        bb = bb.reshape(nd, S5_BLOCKS, gpb, n, c)
        return jnp.einsum("dkgnc,gh->dkgchn", bb, eye).reshape(nd, S5_BLOCKS, gpb * c, gpb * n).astype(BF16)

    def diag_out(cc):
        cc = cc.reshape(nd, S5_BLOCKS, gpb, c, n)
        return jnp.einsum("dkgcn,gh->dkgnhc", cc, eye).reshape(nd, S5_BLOCKS, gpb * n, gpb * c).astype(BF16)

    def decay(a):
        a = a.reshape(nd, S5_BLOCKS, gpb, n, c)[..., 0].reshape(nd, S5_BLOCKS, 1, gpb * n)
        return jnp.broadcast_to(a, (nd, S5_BLOCKS, 8, gpb * n))

    return decay(a_re), decay(a_im), diag_in(bb_re), diag_in(bb_im), diag_out(c_re), diag_out(c_im)


def _s5_kernel(uf_ref, ub_ref, are_ref, aim_ref, bre_ref, bim_ref, cre_ref, cim_ref,
               yf_ref, yb_ref, bur_scr, bui_scr, st_scr, *, steps, bsz):
    i = pl.program_id(0)

    @pl.when(i == 0)
    def _():
        st_scr[...] = jnp.zeros_like(st_scr)

    u_refs = (uf_ref, ub_ref)
    y_refs = (yf_ref, yb_ref)
    per_dir = S5_CHAINS // 2
    for k0 in range(0, S5_BLOCKS, per_dir):
        chains = [(d, k0 + j) for d in range(2) for j in range(per_dir)]
        for c, (d, k) in enumerate(chains):
            u = u_refs[d][:, k * S5_BLOCK_IN:(k + 1) * S5_BLOCK_IN].astype(BF16)
            bur_scr[c] = _dot(u, bre_ref[d, k])
            bui_scr[c] = _dot(u, bim_ref[d, k])

        def step(t, carry, chains=chains):
            out = []
            for c, (d, k) in enumerate(chains):
                sr, si = carry[2 * c], carry[2 * c + 1]
                tt = t if d == 0 else steps - 1 - t
                r0 = pl.multiple_of(tt * bsz, bsz)
                ar = are_ref[d, k]
                ai = aim_ref[d, k]
                nr = ar * sr - ai * si + bur_scr[c, pl.ds(r0, bsz), :]
                ni = ar * si + ai * sr + bui_scr[c, pl.ds(r0, bsz), :]
                bur_scr[c, pl.ds(r0, bsz), :] = nr
                bui_scr[c, pl.ds(r0, bsz), :] = ni
                out += [nr, ni]
            return tuple(out)

        init = tuple(st_scr[d, k, p] for (d, k) in chains for p in range(2))
        final = lax.fori_loop(0, steps, step, init, unroll=2)
        for c, (d, k) in enumerate(chains):
            st_scr[d, k, 0] = final[2 * c]
            st_scr[d, k, 1] = final[2 * c + 1]
            y = (_dot(bur_scr[c].astype(BF16), cre_ref[d, k]) - _dot(bui_scr[c].astype(BF16), cim_ref[d, k]))
            y_refs[d][:, k * S5_BLOCK_IN:(k + 1) * S5_BLOCK_IN] = y


def _s5_scan(u_tm, params, bsz, s_tot, n_ctx):
    a_re, a_im, bb_re, bb_im, cc_re, cc_im = params
    assert bsz == 8
    rows = S5_CHUNK * bsz
    n_chunks = s_tot // S5_CHUNK
    ctx_chunks = n_ctx // S5_CHUNK

    def bwd_chunk(i):
        return jnp.where(i < ctx_chunks, ctx_chunks - 1 - i, n_chunks - 1 + ctx_chunks - i)

    full = lambda a: pl.BlockSpec(a.shape, lambda i: (0,) * a.ndim)
    shp = jax.ShapeDtypeStruct((s_tot * bsz, SSM_WIDTH), F32)
    return pl.pallas_call(
        functools.partial(_s5_kernel, steps=S5_CHUNK, bsz=bsz),
        grid=(n_chunks,),
        in_specs=[pl.BlockSpec((rows, SSM_WIDTH), lambda i: (i, 0)),
                  pl.BlockSpec((rows, SSM_WIDTH), lambda i: (bwd_chunk(i), 0)),
                  full(a_re), full(a_im), full(bb_re), full(bb_im), full(cc_re), full(cc_im)],
        out_specs=[pl.BlockSpec((rows, SSM_WIDTH), lambda i: (i, 0)),
                   pl.BlockSpec((rows, SSM_WIDTH), lambda i: (bwd_chunk(i), 0))],
        out_shape=[shp, shp],
        scratch_shapes=[pltpu.VMEM((S5_CHAINS, rows, S5_BLOCK_STATE), F32),
                        pltpu.VMEM((S5_CHAINS, rows, S5_BLOCK_STATE), F32),
                        pltpu.VMEM((2, S5_BLOCKS, 2, bsz, S5_BLOCK_STATE), F32)],
        compiler_params=_cparams(("arbitrary",)),
        name="s5_scan",
    )(u_tm, u_tm, a_re, a_im, bb_re, bb_im, cc_re, cc_im)


def _s5_post_kernel(u_ref, yf_ref, yb_ref, d_ref, w_ref, o_ref):
    y = d_ref[...] * u_ref[...] + yf_ref[...] + yb_ref[...]
    g = _gelu(y).astype(BF16)
    gate = _sigmoid(_dot(g, w_ref[...]))
    o_ref[...] = (g.astype(F32) * gate).astype(BF16)


def _s5_post(u_tm, yf, yb, d_skip, w_glu_bf16):
    n, w = u_tm.shape
    tm = 1024
    row = pl.BlockSpec((tm, w), lambda i: (i, 0))
    return pl.pallas_call(
        _s5_post_kernel,
        grid=(n // tm,),
        in_specs=[row, row, row, pl.BlockSpec((1, w), lambda i: (0, 0)), pl.BlockSpec((w, w), lambda i: (0, 0))],
        out_specs=row,
        out_shape=jax.ShapeDtypeStruct((n, w), BF16),
        compiler_params=_cparams(("arbitrary",)),
        name="s5_post",
    )(u_tm, yf, yb, d_skip.reshape(1, w), w_glu_bf16)


def _rope_tables(lat_len):
    rows = lat_len // GRID_W
    row = jnp.repeat(jnp.arange(rows, dtype=F32), GRID_W)
    col = jnp.tile(jnp.arange(GRID_W, dtype=F32), rows)
    inv = ROPE_BASE ** (-jnp.arange(ROPE_FREQS, dtype=F32) / ROPE_FREQS)
    ang_r = row[:, None] * inv
    ang_c = col[:, None] * inv
    cos = jnp.concatenate([jnp.cos(ang_r), jnp.cos(ang_r), jnp.cos(ang_c), jnp.cos(ang_c)], axis=1)
    sin = jnp.concatenate([-jnp.sin(ang_r), jnp.sin(ang_r), -jnp.sin(ang_c), jnp.sin(ang_c)], axis=1)
    return jnp.tile(cos, (1, 2)), jnp.tile(sin, (1, 2))


def _head_rms(x, ones_bd, g):
    hi, lo = _split2(x * x)
    ms = (_dot(hi, ones_bd) + _dot(lo, ones_bd)) * (1.0 / ATT_HEAD_DIM)
    return x * lax.rsqrt(ms + EPS) * g


def _rope(x, cos, sin_signed):
    lane = lax.broadcasted_iota(jnp.int32, x.shape, 1)
    first_half = (lane % (2 * ROPE_FREQS)) < ROPE_FREQS
    partner = jnp.where(first_half, pltpu.roll(x, LANES - ROPE_FREQS, 1), pltpu.roll(x, ROPE_FREQS, 1))
    return x * cos + partner * sin_signed


def _qkprep_kernel(q_ref, kv_ref, cos_ref, sin_ref, qg_ref, kg_ref, qc_ref, ql_ref, ko_ref, vo_ref, *, n_ctx, chunk):
    s_tot = q_ref.shape[0]
    r_i = lax.broadcasted_iota(jnp.int32, (LANES, LANES), 0) // ATT_HEAD_DIM
    c_i = lax.broadcasted_iota(jnp.int32, (LANES, LANES), 1) // ATT_HEAD_DIM
    ones_bd = jnp.where(r_i == c_i, 1.0, 0.0).astype(BF16)
    lane = lax.broadcasted_iota(jnp.int32, (chunk, LANES), 1)
    low = lane < ATT_HEAD_DIM
    for r0 in range(0, s_tot, chunk):
        roped = r0 >= n_ctx
        if roped:
            cos = cos_ref[r0 - n_ctx:r0 - n_ctx + chunk, :]
            sin = sin_ref[r0 - n_ctx:r0 - n_ctx + chunk, :]
        for s in range(ATT_WIDTH // LANES):
            x = _head_rms(q_ref[r0:r0 + chunk, s * LANES:(s + 1) * LANES], ones_bd, qg_ref[...])
            if roped:
                x = _rope(x, cos, sin)
            qo_ref, q0 = (ql_ref, r0 - n_ctx) if roped else (qc_ref, r0)
            qo_ref[q0:q0 + chunk, s * LANES:(s + 1) * LANES] = (x * (ATT_HEAD_DIM ** -0.5)).astype(BF16)
        k = _head_rms(kv_ref[r0:r0 + chunk, 0:LANES], ones_bd, kg_ref[...])
        if roped:
            k = _rope(k, cos, sin)
        v = kv_ref[r0:r0 + chunk, LANES:2 * LANES]
        k_sw = pltpu.roll(k, ATT_HEAD_DIM, 1)
        v_sw = pltpu.roll(v, ATT_HEAD_DIM, 1)
        zero = jnp.zeros_like(k)
        ks = (jnp.where(low, k, zero), jnp.where(low, zero, k_sw), jnp.where(low, k_sw, zero), jnp.where(low, zero, k))
        vs = (jnp.where(low, v, zero), jnp.where(low, zero, v_sw), jnp.where(low, v_sw, zero), jnp.where(low, zero, v))
        for n in range(4):
            ko_ref[0, n, r0:r0 + chunk, :] = ks[n].astype(BF16)
            vo_ref[0, n, r0:r0 + chunk, :] = vs[n].astype(BF16)


def _qkprep(proj, cos, sin, q_g, k_g, bsz, s_tot, n_ctx):
    qg = jnp.tile(q_g, 2).reshape(1, LANES)
    kg = jnp.tile(k_g, 2).reshape(1, LANES)
    kv_shape = jax.ShapeDtypeStruct((bsz, 4, s_tot, LANES), BF16)
    kv_spec = pl.BlockSpec((1, 4, s_tot, LANES), lambda b: (b, 0, 0, 0))
    return pl.pallas_call(
        functools.partial(_qkprep_kernel, n_ctx=n_ctx, chunk=256),
        grid=(bsz,),
        in_specs=[pl.BlockSpec((s_tot, PROJ_BLOCK), lambda b: (b, COL_QA // PROJ_BLOCK)),
                  pl.BlockSpec((s_tot, PROJ_BLOCK), lambda b: (b, COL_KA // PROJ_BLOCK)),
                  pl.BlockSpec(cos.shape, lambda b: (0, 0)), pl.BlockSpec(sin.shape, lambda b: (0, 0)),
                  pl.BlockSpec((1, LANES), lambda b: (0, 0)), pl.BlockSpec((1, LANES), lambda b: (0, 0))],
        out_specs=[pl.BlockSpec((n_ctx, ATT_WIDTH), lambda b: (b, 0)),
                   pl.BlockSpec((s_tot - n_ctx, ATT_WIDTH), lambda b: (b, 0)), kv_spec, kv_spec],
        out_shape=[jax.ShapeDtypeStruct((bsz * n_ctx, ATT_WIDTH), BF16),
                   jax.ShapeDtypeStruct((bsz * (s_tot - n_ctx), ATT_WIDTH), BF16), kv_shape, kv_shape],
        compiler_params=_cparams(("arbitrary",)),
        name="qk_prep",
    )(proj, proj, cos, sin, qg, kg)


def _attend(q_ref, k_ref, v_ref, o_ref, n_keys):
    tq = q_ref.shape[0]
    for hk in range(ATT_KV_HEADS):
        qs = jnp.concatenate([q_ref[:, (2 * hk) * LANES:(2 * hk + 1) * LANES],
                              q_ref[:, (2 * hk + 1) * LANES:(2 * hk + 2) * LANES]], axis=0)
        acc = jnp.zeros((2 * tq, LANES), F32)
        for p in range(2):
            s = _dot_nt(qs, k_ref[0, 2 * hk + p, 0:n_keys, :])
            m = jnp.max(s, axis=-1, keepdims=True)
            e = jnp.exp(s - m)
            l = jnp.sum(e, axis=-1, keepdims=True)
            acc = acc + _dot(e.astype(BF16), v_ref[0, 2 * hk + p, 0:n_keys, :]) / l
        o_ref[:, (2 * hk) * LANES:(2 * hk + 1) * LANES] = acc[0:tq].astype(BF16)
        o_ref[:, (2 * hk + 1) * LANES:(2 * hk + 2) * LANES] = acc[tq:2 * tq].astype(BF16)


def _attn_kernel(q_ref, k_ref, v_ref, o_ref, *, n_keys):
    _attend(q_ref, k_ref, v_ref, o_ref, n_keys)


def _attention(q, kn, vn, bsz, s_tot, tq, n_keys):
    blocks = q.shape[0] // bsz // tq
    kv_spec = pl.BlockSpec((1, 4, s_tot, LANES), lambda b, i: (b, 0, 0, 0))
    qo_spec = pl.BlockSpec((tq, ATT_WIDTH), lambda b, i: (b * blocks + i, 0))
    return pl.pallas_call(
        functools.partial(_attn_kernel, n_keys=n_keys),
        grid=(bsz, blocks),
        in_specs=[qo_spec, kv_spec, kv_spec],
        out_specs=qo_spec,
        out_shape=jax.ShapeDtypeStruct(q.shape, BF16),
        compiler_params=_cparams(("arbitrary", "arbitrary")),
        name="attention",
    )(q, kn, vn)


def _log_sigmoid(x):
    return jnp.minimum(x, 0.0) - jnp.log(1.0 + jnp.exp(-jnp.abs(x)))


def _mlstm_kernel(qf_ref, kf_ref, vf_ref, gf_ref, qb_ref, kb_ref, vb_ref, gb_ref, bias_ref,
                  hf_ref, hb_ref, c_scr, n_scr, m_scr):
    i = pl.program_id(1)
    t = ML_CHUNK

    @pl.when(i == 0)
    def _():
        c_scr[...] = jnp.zeros_like(c_scr)
        n_scr[...] = jnp.zeros_like(n_scr)
        m_scr[...] = jnp.zeros_like(m_scr)

    r_i = lax.broadcasted_iota(jnp.int32, (t, t), 0)
    c_i = lax.broadcasted_iota(jnp.int32, (t, t), 1)
    lower = r_i >= c_i
    upper = r_i <= c_i
    lower_m = jnp.where(lower, 1.0, 0.0).astype(BF16)

    for d in range(2):
        q_ref, k_ref, v_ref, g_ref, h_ref = ((qf_ref, kf_ref, vf_ref, gf_ref, hf_ref),
                                             (qb_ref, kb_ref, vb_ref, gb_ref, hb_ref))[d]
        g = g_ref[...] + bias_ref[...]
        g_t = g.T
        lf = _log_sigmoid(g)
        causal = lower if d == 0 else upper
        lf_hi, lf_lo = _split2(lf)
        prefix = _dot(lower_m, lf_hi) + _dot(lower_m, lf_lo)
        b_cols = prefix if d == 0 else prefix[t - 1:t, :] - prefix + lf
        b_rows = b_cols.T
        last = t - 1 if d == 0 else 0
        for h in range(ML_HEADS):
            ci = d * 2 * ML_HEADS + h
            cf = ci + ML_HEADS
            i_col = g[:, ci:ci + 1]
            i_row = g_t[ci:ci + 1, :]
            b_col = b_cols[:, cf:cf + 1]
            b_row = b_rows[cf:cf + 1, :]
            b_last = b_cols[last:last + 1, cf:cf + 1]
            m_prev = m_scr[d, h]
            c_prev = c_scr[d, h]
            n_prev = n_scr[d, h]

            q = q_ref[:, h * ML_HEAD_DIM:(h + 1) * ML_HEAD_DIM]
            k = k_ref[:, h * ML_HEAD_DIM:(h + 1) * ML_HEAD_DIM] * (ML_HEAD_DIM ** -0.5)
            v = v_ref[:, h * ML_HEAD_DIM:(h + 1) * ML_HEAD_DIM]
            q16, k16, v16 = q.astype(BF16), k.astype(BF16), v.astype(BF16)

            dmat = jnp.where(causal, b_col - b_row + i_row, NEG_INF)
            inter = b_col + m_prev
            m_t = jnp.maximum(inter, jnp.max(dmat, axis=-1, keepdims=True))
            w = jnp.exp(dmat - m_t)
            a_inter = jnp.exp(inter - m_t)
            qk = _dot_nt(q16, k16) * w
            num = a_inter * _dot(q16, c_prev.astype(BF16)) + _dot(qk.astype(BF16), v16)
            den = a_inter * jnp.sum(q * n_prev, axis=-1, keepdims=True) + jnp.sum(qk, axis=-1, keepdims=True)
            h_ref[:, h * ML_HEAD_DIM:(h + 1) * ML_HEAD_DIM] = num / jnp.maximum(jnp.abs(den), jnp.exp(-m_t))

            d_last_col = b_last - b_col + i_col
            m_new = jnp.maximum(b_last + m_prev, jnp.max(d_last_col, axis=0, keepdims=True))
            w_last = jnp.exp(d_last_col - m_new)
            decay = jnp.exp(b_last + m_prev - m_new)
            kw = k * w_last
            c_scr[d, h] = decay * c_prev + _dot(kw.T.astype(BF16), v16)
            n_scr[d, h] = decay * n_prev + jnp.sum(kw, axis=0, keepdims=True)
            m_scr[d, h] = m_new


def _mlstm(proj, gate_b, bsz, s_tot, n_ctx):
    t = ML_CHUNK
    n_chunks = s_tot // t
    ctx_chunks = n_ctx // t

    def fwd(b, i):
        return b * n_chunks + i

    def bwd(b, i):
        return b * n_chunks + jnp.where(i < ctx_chunks, ctx_chunks - 1 - i, n_chunks - 1 + ctx_chunks - i)

    def col(c0, width):
        return c0 // width

    def specs(rowfn):
        return [pl.BlockSpec((t, ML_WIDTH), lambda b, i: (rowfn(b, i), col(COL_QM, ML_WIDTH))),
                pl.BlockSpec((t, ML_WIDTH), lambda b, i: (rowfn(b, i), col(COL_KM, ML_WIDTH))),
                pl.BlockSpec((t, ML_WIDTH), lambda b, i: (rowfn(b, i), col(COL_VM, ML_WIDTH))),
                pl.BlockSpec((t, LANES), lambda b, i: (rowfn(b, i), col(COL_GM, LANES)))]

    bias = jnp.zeros((1, LANES), F32).at[0, :ML_GATES].set(gate_b.reshape(ML_GATES))
    shp = jax.ShapeDtypeStruct((bsz * s_tot, ML_WIDTH), F32)
    return pl.pallas_call(
        _mlstm_kernel,
        grid=(bsz, n_chunks),
        in_specs=specs(fwd) + specs(bwd) + [pl.BlockSpec((1, LANES), lambda b, i: (0, 0))],
        out_specs=[pl.BlockSpec((t, ML_WIDTH), lambda b, i: (fwd(b, i), 0)),
                   pl.BlockSpec((t, ML_WIDTH), lambda b, i: (bwd(b, i), 0))],
        out_shape=[shp, shp],
        scratch_shapes=[pltpu.VMEM((2, ML_HEADS, ML_HEAD_DIM, ML_HEAD_DIM), F32),
                        pltpu.VMEM((2, ML_HEADS, 1, ML_HEAD_DIM), F32),
                        pltpu.VMEM((2, ML_HEADS, 1, 1), F32)],
        compiler_params=_cparams(("arbitrary", "arbitrary")),
        name="mlstm",
    )(*([proj] * 8), bias)


def _merge_kernel(ys_ref, ya_ref, hf_ref, hb_ref, om_ref, gl_ref, x_ref, mod_ref, mg_ref, n2_ref,
                  wbs_ref, wba_ref, wbm_ref, wo_ref, xo_ref, h2_ref, *, blocks_per_batch, first_block):
    n = pl.program_id(0)
    b = n // (blocks_per_batch - first_block)
    if not first_block:
        b = jnp.where(n % blocks_per_batch == 0, CTX_MOD_ROW, b)
    is_ctx = None

    hs = hf_ref[...] + hb_ref[...]
    parts = []
    for h in range(ML_HEADS):
        sl = slice(h * ML_HEAD_DIM, (h + 1) * ML_HEAD_DIM)
        parts.append(_rms(hs[:, sl], mg_ref[:, sl]))
    hn = jnp.concatenate(parts, axis=1) * _sigmoid(om_ref[...])

    gl = gl_ref[...]
    merged = (_sigmoid(gl[:, 0:D_MODEL]) * _dot(ys_ref[...], wbs_ref[...])
              + _sigmoid(gl[:, D_MODEL:2 * D_MODEL]) * _dot(ya_ref[...], wba_ref[...])
              + _sigmoid(gl[:, 2 * D_MODEL:3 * D_MODEL]) * _dot(hn.astype(BF16), wbm_ref[...]))
    mix = _dot(merged.astype(BF16), wo_ref[...])
    g1 = _mod_rows(mod_ref, 2, b, is_ctx)
    x = x_ref[...] + g1 * mix
    xo_ref[...] = x
    sh2 = _mod_rows(mod_ref, 3, b, is_ctx)
    sc2 = _mod_rows(mod_ref, 4, b, is_ctx)
    h2_ref[...] = (_rms(x, n2_ref[...]) * (1.0 + sc2) + sh2).T.astype(BF16)


def _merge(ys, ya, hf, hb, proj, stream, mod_l, ml_norm_g, norm2_g, wbs, wba, wbm, wo,
           bsz, s_tot, n_ctx, latent_only):
    tm = n_ctx
    bpb = s_tot // tm
    first = 1 if latent_only else 0
    per = bpb - first

    def rows(n):
        return (n // per) * bpb + n % per + first

    n_blocks = bsz * per
    full = lambda a: pl.BlockSpec(a.shape, lambda n: (0,) * a.ndim)
    wide = lambda w: pl.BlockSpec((tm, w), lambda n: (rows(n), 0))
    mg = ml_norm_g.reshape(1, ML_WIDTH)
    n2 = norm2_g.reshape(1, D_MODEL)
    return pl.pallas_call(
        functools.partial(_merge_kernel, blocks_per_batch=bpb, first_block=first),
        grid=(n_blocks,),
        in_specs=[wide(SSM_WIDTH), pl.BlockSpec((tm, ATT_WIDTH), lambda n: (n, 0)), wide(ML_WIDTH), wide(ML_WIDTH),
                  pl.BlockSpec((tm, ML_WIDTH), lambda n: (rows(n), COL_OM // ML_WIDTH)),
                  pl.BlockSpec((tm, N_BRANCH * D_MODEL), lambda n: (rows(n), COL_GATE // (N_BRANCH * D_MODEL))),
                  wide(D_MODEL), full(mod_l), full(mg), full(n2), full(wbs), full(wba), full(wbm), full(wo)],
        out_specs=[pl.BlockSpec((tm, D_MODEL), lambda n: (n, 0)), pl.BlockSpec((D_MODEL, tm), lambda n: (0, n))],
        out_shape=[jax.ShapeDtypeStruct((n_blocks * tm, D_MODEL), F32),
                   jax.ShapeDtypeStruct((D_MODEL, n_blocks * tm), BF16)],
        compiler_params=_cparams(("arbitrary",)),
        name="merge",
    )(ys, ya, hf, hb, proj, proj, stream, mod_l, mg, n2, wbs, wba, wbm, wo)


def _peer_score_kernel(xt_ref, wq_ref, k1_ref, k2_ref, s1_ref, s2_ref):
    half = PEER_QDIM // 2
    q_t = _dot(wq_ref[...], xt_ref[...])
    for key_ref, s_ref, lo in ((k1_ref, s1_ref, 0), (k2_ref, s2_ref, half)):
        khi, klo = _split2(key_ref[...])
        for h in range(PEER_HEADS):
            qhi, qlo = _split2(q_t[h * PEER_QDIM + lo:h * PEER_QDIM + lo + half, :])
            s_ref[h] = _dot(khi, qhi) + _dot(khi, qlo) + _dot(klo, qhi)


def _peer_scores(h2_t, wq_t, k1, k2):
    n = h2_t.shape[1]
    tm = 1024
    shp = jax.ShapeDtypeStruct((PEER_HEADS, PEER_KEYS, n), F32)
    out = pl.BlockSpec((PEER_HEADS, PEER_KEYS, tm), lambda i: (0, 0, i))
    key = pl.BlockSpec((PEER_KEYS, PEER_QDIM // 2), lambda i: (0, 0))
    return pl.pallas_call(
        _peer_score_kernel,
        grid=(n // tm,),
        in_specs=[pl.BlockSpec((D_MODEL, tm), lambda i: (0, i)), pl.BlockSpec(wq_t.shape, lambda i: (0, 0)), key, key],
        out_specs=[out, out],
        out_shape=[shp, shp],
        compiler_params=_cparams(("arbitrary",)),
        name="peer_scores",
    )(h2_t, wq_t, k1, k2)


def _top_values(s, k):
    work = s
    rank = jnp.full(s.shape, float(PEER_KEYS), F32)
    vals = []
    for r in range(k):
        m = jnp.max(work, axis=0, keepdims=True)
        hit = work == m
        rank = jnp.where(hit, float(r), rank)
        work = jnp.where(hit, NEG_INF, work)
        vals.append(m)
    return vals, rank


def _peer_select_kernel(s1_ref, s2_ref, r2_ref, g2_ref, cnt_ref, e1_ref, v1_scr, v2_scr):
    k = PEER_TOPK
    half = k // 2
    s1 = s1_ref[0]
    s2 = s2_ref[0]
    v1, rank1 = _top_values(s1, k)
    v2, rank2 = _top_values(s2, k)
    for a in range(k):
        v1_scr[a:a + 1, :] = v1[a]
        v2_scr[a:a + 1, :] = v2[a]
    row = lax.broadcasted_iota(jnp.int32, (half, s1.shape[1]), 0)
    pieces = [v1[0] + v2_scr[...]]
    for a in range(1, half):
        pieces.append(jnp.where(row < k // (a + 1), v1[a] + v2_scr[0:half, :], NEG_INF))
    pieces.append(v1_scr[half:k, :] + v2[0])
    top = v1[0] + v2[0]
    z = jnp.zeros_like(top)
    thr = top
    for _ in range(k):
        thr = functools.reduce(jnp.maximum, [jnp.max(p, axis=0, keepdims=True) for p in pieces])
        pieces = [jnp.where(p == thr, NEG_INF, p) for p in pieces]
        z = z + jnp.exp(thr - top)
    cnt = jnp.zeros_like(s1)
    for a in range(k):
        if a == 0:
            sums = v1[0] + v2_scr[...]
        elif a < half:
            sums = jnp.where(row < k // (a + 1), v1[a] + v2_scr[0:half, :], NEG_INF)
        else:
            sums = v1[a] + v2[0]
        n_sel = jnp.sum(jnp.where(sums >= thr, 1.0, 0.0), axis=0, keepdims=True)
        cnt = jnp.where(rank1 == float(a), n_sel, cnt)
    r2_ref[0] = rank2.astype(BF16)
    g2_ref[0] = jnp.exp(s2 - v2[0]).astype(BF16)
    cnt_ref[0] = cnt
    e1_ref[0] = jnp.exp(s1 - v1[0]) / z


def _peer_select(s1, s2):
    heads, keys, n = s1.shape
    tl = 1024
    spec = pl.BlockSpec((1, keys, tl), lambda h, i: (h, 0, i))
    shp = jax.ShapeDtypeStruct((heads, keys, n), F32)
    shp16 = jax.ShapeDtypeStruct((heads, keys, n), BF16)
    return pl.pallas_call(
        _peer_select_kernel,
        grid=(heads, n // tl),
        in_specs=[spec, spec], out_specs=[spec] * 4, out_shape=[shp16, shp16, shp, shp],
        scratch_shapes=[pltpu.VMEM((PEER_TOPK, tl), F32), pltpu.VMEM((PEER_TOPK, tl), F32)],
        compiler_params=_cparams(("arbitrary", "arbitrary")),
        name="peer_select",
    )(s1, s2)


def _peer_dense_kernel(x_ref, u_ref, vt_ref, r2_ref, g2_ref, cnt_ref, e1_ref, s_ref, mod_ref, fg_ref,
                       o_ref, acc_scr, a_scr, w_scr, coef_even, coef_odd, x_scr, r2_scr, g2_scr, row_scr,
                       *, te, nj, blocks_per_batch, n_ctx, final):
    n = pl.program_id(0)
    tm = x_ref.shape[1]
    n_tiles = tm // PEER_TOK
    keys_per_step = te // PEER_KEYS
    assert keys_per_step % F32_SUBLANES == 0

    @pl.when(n == 0)
    def _():
        acc_scr[...] = jnp.zeros_like(acc_scr)
        coef_odd[...] = jnp.zeros_like(coef_odd)

    @pl.when(n % nj == 0)
    def _():
        for t in range(n_tiles):
            tok = slice(t * PEER_TOK, (t + 1) * PEER_TOK)
            x_scr[t] = x_ref[:, tok]
            r2_scr[t] = r2_ref[:, :, tok]
            g2_scr[t] = g2_ref[:, :, tok]

    for t in range(n_tiles):
        tok = slice(t * PEER_TOK, (t + 1) * PEER_TOK)
        row_scr[t, 0] = cnt_ref[:, :, tok]
        row_scr[t, 1] = e1_ref[:, :, tok]

    def step(read_ref, write_ref):
        zero = jnp.zeros((PEER_KEYS, LANES), BF16)

        def tile(t, carry):
            a_scr[...] = _dot(pltpu.bitcast(u_ref[...], BF16), x_scr[t]).astype(BF16)
            acc_scr[t] += _dot(pltpu.bitcast(vt_ref[...], BF16), read_ref[t])
            for s in range(keys_per_step):
                rows = slice(s * PEER_KEYS, (s + 1) * PEER_KEYS)
                for l in range(PEER_TOK // LANES):
                    lanes = slice(l * LANES, (l + 1) * LANES)
                    w = zero
                    for h in range(PEER_HEADS):
                        cnt = row_scr[t, 0, h, s:s + 1, lanes].astype(BF16)
                        e1 = row_scr[t, 1, h, s:s + 1, lanes].astype(BF16)
                        w = w + jnp.where(r2_scr[t, h, :, lanes] < cnt, g2_scr[t, h, :, lanes], zero) * e1
                    w_scr[rows, lanes] = w
            for s in range(keys_per_step):
                rows = slice(s * PEER_KEYS, (s + 1) * PEER_KEYS)
                write_ref[t, rows, :] = w_scr[rows, :] * _gelu(a_scr[rows, :])
            return carry

        lax.fori_loop(0, n_tiles, tile, 0)

    @pl.when(n % 2 == 0)
    def _():
        step(coef_odd, coef_even)

    @pl.when(n % 2 == 1)
    def _():
        step(coef_even, coef_odd)

    prev = n - 1

    @pl.when(jnp.logical_and(n > 0, prev % nj == nj - 1))
    def _():
        ip = prev // nj
        b = ip // blocks_per_batch
        for t in range(n_tiles):
            tok = slice(t * PEER_TOK, (t + 1) * PEER_TOK)
            if n_ctx:
                row = (ip % blocks_per_batch) * tm + t * PEER_TOK + lax.broadcasted_iota(jnp.int32, (PEER_TOK, 1), 0)
                is_ctx = row < n_ctx
            else:
                is_ctx = None
            g2 = _mod_rows(mod_ref, 5, b, is_ctx)
            x = s_ref[tok, :] + g2 * acc_scr[t].T
            if final:
                x = _rms(x, fg_ref[...])
            o_ref[tok, :] = x
        acc_scr[...] = jnp.zeros_like(acc_scr)


def _pack_tables_kernel(u_ref, v_ref, uo_ref, vo_ref):
    uo_ref[...] = pltpu.bitcast(u_ref[...].astype(BF16), F32)
    vo_ref[...] = pltpu.bitcast(v_ref[...].T.astype(BF16), F32)


def _pack_tables(u, v):
    e, d = u.shape
    te = 512
    return pl.pallas_call(
        _pack_tables_kernel,
        grid=(e // te,),
        in_specs=[pl.BlockSpec((te, d), lambda j: (j, 0)), pl.BlockSpec((te, d), lambda j: (j, 0))],
        out_specs=[pl.BlockSpec((te // 2, d), lambda j: (j, 0)), pl.BlockSpec((d // 2, te), lambda j: (0, j))],
        out_shape=[jax.ShapeDtypeStruct((e // 2, d), F32), jax.ShapeDtypeStruct((d // 2, e), F32)],
        compiler_params=_cparams(("arbitrary",)),
        name="pack_tables",
    )(u, v)


def _peer_dense(h2_t, u_words, vt_words, sel, stream, mod_l, final_g, rows_per_batch, n_ctx, final):
    n = h2_t.shape[1]
    r2, g2, cnt, e1 = sel
    tm = 768 if rows_per_batch % 768 == 0 else 512
    te = 1024 if tm == 768 else 2048
    bpb = rows_per_batch // tm
    nj = PEER_EXPERTS // te
    n_steps = (n // tm) * nj

    def cur(s):
        c = jnp.minimum(s, n_steps - 1)
        return c // nj, c % nj

    def prev(s):
        p = jnp.maximum(s - 1, 0)
        return p // nj, p % nj

    tok = pl.BlockSpec((PEER_HEADS, PEER_KEYS, tm), lambda s: (0, 0, cur(s)[0]))
    sub = pl.BlockSpec((PEER_HEADS, te // PEER_KEYS, tm), lambda s: (0, cur(s)[1], cur(s)[0]))
    full = lambda a: pl.BlockSpec(a.shape, lambda s: (0,) * a.ndim)
    fg = final_g.reshape(1, D_MODEL)
    n_tiles = tm // PEER_TOK
    coef = pltpu.VMEM((n_tiles, te, PEER_TOK), BF16)
    sel16 = pltpu.VMEM((n_tiles, PEER_HEADS, PEER_KEYS, PEER_TOK), BF16)
    return pl.pallas_call(
        functools.partial(_peer_dense_kernel, te=te, nj=nj, blocks_per_batch=bpb, n_ctx=n_ctx, final=final),
        grid=(n_steps + 1,),
        in_specs=[pl.BlockSpec((D_MODEL, tm), lambda s: (0, cur(s)[0])),
                  pl.BlockSpec((te // 2, D_MODEL), lambda s: (cur(s)[1], 0)),
                  pl.BlockSpec((D_MODEL // 2, te), lambda s: (0, prev(s)[1])),
                  tok, tok, sub, sub,
                  pl.BlockSpec((tm, D_MODEL), lambda s: (prev(s)[0], 0)), full(mod_l), full(fg)],
        out_specs=pl.BlockSpec((tm, D_MODEL), lambda s: (prev(s)[0], 0)),
        out_shape=jax.ShapeDtypeStruct((n, D_MODEL), F32),
        scratch_shapes=[pltpu.VMEM((n_tiles, D_MODEL, PEER_TOK), F32), pltpu.VMEM((te, PEER_TOK), BF16),
                        pltpu.VMEM((te, PEER_TOK), BF16), coef, coef,
                        pltpu.VMEM((n_tiles, D_MODEL, PEER_TOK), BF16), sel16, sel16,
                        pltpu.VMEM((n_tiles, 2, PEER_HEADS, te // PEER_KEYS, PEER_TOK), F32)],
        compiler_params=_cparams(("arbitrary",)),
        name="peer_dense",
    )(h2_t, u_words, vt_words, r2, g2, cnt, e1, stream, mod_l, fg)


def _reorder_w_in(w):
    offs = [0]
    for width in (SSM_WIDTH, ATT_WIDTH, KV_WIDTH, KV_WIDTH, ML_WIDTH, ML_WIDTH, ML_WIDTH, ML_WIDTH, ML_GATES,
                  N_BRANCH * D_MODEL):
        offs.append(offs[-1] + width)
    u_s, q_a, k_a, v_a, q_m, k_m, v_m, o_m, g_m, gate = (w[:, offs[n]:offs[n + 1]] for n in range(10))
    pad = jnp.zeros((w.shape[0], PROJ_WIDTH - COL_GM - ML_GATES), w.dtype)
    return jnp.concatenate([u_s, q_a, q_m, k_m, v_m, o_m, gate, k_a, v_a, g_m, pad], axis=1).astype(BF16)


def kernel(x, c, ctx, c_ctx, w_mod, b_mod, norm1_g, norm2_g, w_in, ssm_lam_re, ssm_lam_im, ssm_log_step, ssm_b_re, ssm_b_im, ssm_c_re, ssm_c_im, ssm_d, ssm_w_glu, attn_q_norm_g, attn_k_norm_g, mlstm_gate_b, mlstm_norm_g, w_branch_ssm, w_branch_attn, w_branch_mlstm, w_out, peer_w_q, peer_sub_k1, peer_sub_k2, peer_u, peer_v, final_norm_g):
    bsz, lat_len, d = x.shape
    n_ctx = ctx.shape[1]
    s_tot = n_ctx + lat_len
    depth = w_in.shape[0]
    assert d == D_MODEL and n_ctx == 256 and lat_len % n_ctx == 0 and bsz == 8

    cos, sin = _rope_tables(lat_len)
    mod = _modulation(c, c_ctx, w_mod, b_mod)
    stream = jnp.concatenate([ctx, x], axis=1)

    for layer in range(depth):
        last = layer == depth - 1
        mod_l = mod[layer]
        proj = _inproj(stream.reshape(bsz, s_tot, d), norm1_g[layer], mod_l, _reorder_w_in(w_in[layer]), n_ctx)

        u_tm = proj[:, COL_U:COL_U + SSM_WIDTH].reshape(bsz, s_tot, SSM_WIDTH).transpose(1, 0, 2)
        u_tm = u_tm.reshape(s_tot * bsz, SSM_WIDTH)
        params = _s5_params(ssm_lam_re[layer], ssm_lam_im[layer], ssm_log_step[layer], ssm_b_re[layer],
                            ssm_b_im[layer], ssm_c_re[layer], ssm_c_im[layer])
        yf, yb = _s5_scan(u_tm, params, bsz, s_tot, n_ctx)
        ys_tm = _s5_post(u_tm, yf, yb, ssm_d[layer], ssm_w_glu[layer].astype(BF16))
        ys = ys_tm.reshape(s_tot, bsz, SSM_WIDTH).transpose(1, 0, 2).reshape(bsz * s_tot, SSM_WIDTH)

        q_ctx, q_lat, kn, vn = _qkprep(proj, cos, sin, attn_q_norm_g[layer], attn_k_norm_g[layer], bsz, s_tot, n_ctx)
        ya = _attention(q_lat, kn, vn, bsz, s_tot, ATT_Q_BLOCK, s_tot)
        if not last:
            ya_ctx = _attention(q_ctx, kn, vn, bsz, s_tot, n_ctx, n_ctx)
            ya = jnp.concatenate([ya_ctx.reshape(bsz, n_ctx, ATT_WIDTH), ya.reshape(bsz, lat_len, ATT_WIDTH)],
                                 axis=1).reshape(bsz * s_tot, ATT_WIDTH)

        hf, hb = _mlstm(proj, mlstm_gate_b[layer], bsz, s_tot, n_ctx)

        stream2, h2_t = _merge(ys, ya, hf, hb, proj, stream.reshape(bsz * s_tot, d), mod_l, mlstm_norm_g[layer],
                             norm2_g[layer], w_branch_ssm[layer].astype(BF16), w_branch_attn[layer].astype(BF16),
                             w_branch_mlstm[layer].astype(BF16), w_out[layer].astype(BF16),
                             bsz, s_tot, n_ctx, last)

        s1, s2 = _peer_scores(h2_t, peer_w_q[layer].T.astype(BF16), peer_sub_k1[layer], peer_sub_k2[layer])
        sel = _peer_select(s1, s2)
        rows_per_batch = lat_len if last else s_tot
        u_words, vt_words = _pack_tables(peer_u[layer], peer_v[layer])
        stream = _peer_dense(h2_t, u_words, vt_words, sel, stream2, mod_l,
                             final_norm_g, rows_per_batch, 0 if last else n_ctx, last)

    return stream.reshape(bsz, lat_len, d)
```

```python
import functools
import math

import jax
import jax.numpy as jnp
from jax import lax
from jax.experimental import pallas as pl
from jax.experimental.pallas import tpu as pltpu

F32 = jnp.float32
BF16 = jnp.bfloat16

D_MODEL = 1024
GRID_W = 64
EPS = 1e-6
N_MOD = 6
N_BRANCH = 3
SSM_WIDTH = 512
SSM_GROUP = 16
SSM_GROUPS = 32
SSM_STATE = 64
ATT_HEADS = 8
ATT_KV_HEADS = 2
ATT_HEAD_DIM = 64
ATT_WIDTH = 512
ATT_Q_BLOCK = 512
KV_WIDTH = 128
ROPE_FREQS = 16
ROPE_BASE = 10000.0
ML_HEADS = 4
ML_HEAD_DIM = 128
ML_WIDTH = 512
ML_GATES = 16
ML_CHUNK = 256
PEER_HEADS = 8
PEER_KEYS = 128
PEER_EXPERTS = PEER_KEYS * PEER_KEYS
PEER_QDIM = 256
PEER_TOPK = 16
PEER_TOK = 256

LANES = 128
F32_SUBLANES = 8
BF16_SUBLANES = 16
VMEM_LIMIT_BYTES = 56 * 1024 * 1024

PROJ_BLOCK = 512
COL_U, COL_QA, COL_QM, COL_KM, COL_VM, COL_OM = (i * PROJ_BLOCK for i in range(6))
COL_GATE = 6 * PROJ_BLOCK
COL_KA = COL_GATE + N_BRANCH * D_MODEL
COL_VA = COL_KA + KV_WIDTH
COL_GM = COL_VA + KV_WIDTH
PROJ_WIDTH = COL_KA + PROJ_BLOCK

S5_BLOCKS = 4
S5_BLOCK_IN = SSM_WIDTH // S5_BLOCKS
S5_BLOCK_STATE = SSM_GROUPS * SSM_STATE // S5_BLOCKS
S5_CHUNK = 128
S5_CHAINS = 4

NEG_INF = float("-inf")

ROW_CHUNK = 256
MOD_COLS = 1536
S5_POST_ROWS = 1024
SCORE_TOKENS = 1024
SELECT_TOKENS = 1024
PACK_EXPERTS = 512
DENSE_TOKENS = (768, 512)
DENSE_EXPERTS = {768: 1024, 512: 2048}


def _cparams(sem, flags=None):
    return pltpu.CompilerParams(dimension_semantics=sem, vmem_limit_bytes=VMEM_LIMIT_BYTES, flags=flags)


def _split2(x):
    hi = x.astype(BF16)
    lo = (x - hi.astype(F32)).astype(BF16)
    return hi, lo


def _split3(x):
    hi = x.astype(BF16)
    r = x - hi.astype(F32)
    mid = r.astype(BF16)
    lo = (r - mid.astype(F32)).astype(BF16)
    return hi, mid, lo


def _dot(a, b):
    return jnp.dot(a, b, preferred_element_type=F32)


def _dot_nt(a, b):
    return lax.dot_general(a, b, (((1,), (1,)), ((), ())), preferred_element_type=F32)


def _sigmoid(x):
    return 1.0 / (1.0 + jnp.exp(-x))


def _gelu(x):
    return jax.nn.gelu(x, approximate=True)


def _mod_kernel(v_ref, w_ref, b_ref, o_ref):
    v = v_ref[...]
    sv = v * _sigmoid(v)
    w = w_ref[0]
    hi, mid, lo = _split3(sv)
    whi, wlo = _split2(w)
    acc = _dot(hi, whi) + _dot(mid, whi) + _dot(hi, wlo) + _dot(lo, whi) + _dot(mid, wlo)
    o_ref[0] = acc + b_ref[0]


def _modulation(c, c_ctx, w_mod, b_mod):
    depth = w_mod.shape[0]
    n_out = w_mod.shape[2]
    rows = 16
    v = jnp.zeros((rows, D_MODEL), F32).at[: c.shape[0]].set(c).at[8].set(c_ctx)
    tn = MOD_COLS
    return pl.pallas_call(
        _mod_kernel,
        grid=(depth, n_out // tn),
        in_specs=[pl.BlockSpec((rows, D_MODEL), lambda l, j: (0, 0)),
                  pl.BlockSpec((1, D_MODEL, tn), lambda l, j: (l, 0, j)),
                  pl.BlockSpec((1, 1, tn), lambda l, j: (l, 0, j))],
        out_specs=pl.BlockSpec((1, rows, tn), lambda l, j: (l, 0, j)),
        out_shape=jax.ShapeDtypeStruct((depth, rows, n_out), F32),
        compiler_params=_cparams(("arbitrary", "arbitrary")),
        name="modulation",
    )(v, w_mod, b_mod.reshape(depth, 1, n_out))


CTX_MOD_ROW = 8


def _mod_rows(mod_ref, which, b, is_ctx_col):
    lo = which * D_MODEL
    m_l = mod_ref[pl.ds(b, 1), lo:lo + D_MODEL]
    if is_ctx_col is None:
        return m_l
    m_c = mod_ref[CTX_MOD_ROW:CTX_MOD_ROW + 1, lo:lo + D_MODEL]
    return jnp.where(is_ctx_col, m_c, m_l)


def _rms(x, g):
    ms = jnp.mean(x * x, axis=-1, keepdims=True)
    return x * lax.rsqrt(ms + EPS) * g


def _inproj_kernel(x_ref, g_ref, mod_ref, w_ref, o_ref, h_scr, *, n_ctx, chunk):
    b = pl.program_id(0)
    j = pl.program_id(1)
    s_tot = x_ref.shape[1]

    @pl.when(j == 0)
    def _():
        for r0 in range(0, s_tot, chunk):
            x = x_ref[0, r0:r0 + chunk, :]
            xn = _rms(x, g_ref[...])
            row = r0 + lax.broadcasted_iota(jnp.int32, (chunk, 1), 0)
            is_ctx = row < n_ctx
            sh = _mod_rows(mod_ref, 0, b, is_ctx)
            sc = _mod_rows(mod_ref, 1, b, is_ctx)
            h_scr[r0:r0 + chunk, :] = (xn * (1.0 + sc) + sh).astype(BF16)

    o_ref[...] = _dot(h_scr[...], w_ref[...])


def _inproj(stream3, g, mod_l, w_bf16, n_ctx):
    bsz, s_tot, d = stream3.shape
    n_out = w_bf16.shape[1]
    tn = PROJ_BLOCK
    return pl.pallas_call(
        functools.partial(_inproj_kernel, n_ctx=n_ctx, chunk=ROW_CHUNK),
        grid=(bsz, n_out // tn),
        in_specs=[pl.BlockSpec((1, s_tot, d), lambda b, j: (b, 0, 0)),
                  pl.BlockSpec((1, d), lambda b, j: (0, 0)),
                  pl.BlockSpec(mod_l.shape, lambda b, j: (0, 0)),
                  pl.BlockSpec((d, tn), lambda b, j: (0, j))],
        out_specs=pl.BlockSpec((s_tot, tn), lambda b, j: (b, j)),
        out_shape=jax.ShapeDtypeStruct((bsz * s_tot, n_out), F32),
        scratch_shapes=[pltpu.VMEM((s_tot, d), BF16)],
        compiler_params=_cparams(("arbitrary", "arbitrary")),
        name="inproj",
    )(stream3, g.reshape(1, d), mod_l, w_bf16)


def _s5_param_kernel(lre_ref, lim_ref, ls_ref, bre_ref, bim_ref, are_ref, aim_ref, bbre_ref, bbim_ref):
    lre = lre_ref[...]
    lim = lim_ref[...]
    step = jnp.exp(ls_ref[...])
    mag = jnp.exp(lre * step)
    a_re = mag * jnp.cos(lim * step)
    a_im = mag * jnp.sin(lim * step)
    den = lre * lre + lim * lim
    z_re = ((a_re - 1.0) * lre + a_im * lim) / den
    z_im = (a_im * lre - (a_re - 1.0) * lim) / den
    b_re = bre_ref[...]
    b_im = bim_ref[...]
    are_ref[...] = a_re
    aim_ref[...] = a_im
    bbre_ref[...] = z_re * b_re - z_im * b_im
    bbim_ref[...] = z_re * b_im + z_im * b_re


def _s5_params(lam_re, lam_im, log_step, b_re, b_im, c_re, c_im):
    nd, g, n = lam_re.shape
    c = b_re.shape[-1]
    rows = nd * g
    wide = n * c

    def expand(z):
        return jnp.broadcast_to(z.reshape(rows, n, 1), (rows, n, c)).reshape(rows, wide)

    ls = jnp.broadcast_to(log_step.reshape(rows, 1), (rows, wide))
    spec = pl.BlockSpec((rows, wide), lambda: (0, 0))
    shp = jax.ShapeDtypeStruct((rows, wide), F32)
    a_re, a_im, bb_re, bb_im = pl.pallas_call(
        _s5_param_kernel,
        in_specs=[spec] * 5, out_specs=[spec] * 4, out_shape=[shp] * 4,
        name="s5_params",
    )(expand(lam_re), expand(lam_im), ls, b_re.reshape(rows, wide), b_im.reshape(rows, wide))

    gpb = g // S5_BLOCKS
    eye = jnp.eye(gpb, dtype=F32)

    def diag_in(bb):
        bb = bb.reshape(nd, S5_BLOCKS, gpb, n, c)
        return jnp.einsum("dkgnc,gh->dkgchn", bb, eye).reshape(nd, S5_BLOCKS, gpb * c, gpb * n).astype(BF16)

    def diag_out(cc):
        cc = cc.reshape(nd, S5_BLOCKS, gpb, c, n)
        return jnp.einsum("dkgcn,gh->dkgnhc", cc, eye).reshape(nd, S5_BLOCKS, gpb * n, gpb * c).astype(BF16)

    def decay(a):
        a = a.reshape(nd, S5_BLOCKS, gpb, n, c)[..., 0].reshape(nd, S5_BLOCKS, 1, gpb * n)
        return jnp.broadcast_to(a, (nd, S5_BLOCKS, 8, gpb * n))

    return decay(a_re), decay(a_im), diag_in(bb_re), diag_in(bb_im), diag_out(c_re), diag_out(c_im)


def _s5_kernel(uf_ref, ub_ref, are_ref, aim_ref, bre_ref, bim_ref, cre_ref, cim_ref,
               yf_ref, yb_ref, bur_scr, bui_scr, st_scr, *, steps, bsz):
    i = pl.program_id(0)

    @pl.when(i == 0)
    def _():
        st_scr[...] = jnp.zeros_like(st_scr)

    u_refs = (uf_ref, ub_ref)
    y_refs = (yf_ref, yb_ref)
    per_dir = S5_CHAINS // 2
    for k0 in range(0, S5_BLOCKS, per_dir):
        chains = [(d, k0 + j) for d in range(2) for j in range(per_dir)]
        for c, (d, k) in enumerate(chains):
            u = u_refs[d][:, k * S5_BLOCK_IN:(k + 1) * S5_BLOCK_IN].astype(BF16)
            bur_scr[c] = _dot(u, bre_ref[d, k])
            bui_scr[c] = _dot(u, bim_ref[d, k])

        def step(t, carry, chains=chains):
            out = []
            for c, (d, k) in enumerate(chains):
                sr, si = carry[2 * c], carry[2 * c + 1]
                tt = t if d == 0 else steps - 1 - t
                r0 = pl.multiple_of(tt * bsz, bsz)
                ar = are_ref[d, k]
                ai = aim_ref[d, k]
                nr = ar * sr - ai * si + bur_scr[c, pl.ds(r0, bsz), :]
                ni = ar * si + ai * sr + bui_scr[c, pl.ds(r0, bsz), :]
                bur_scr[c, pl.ds(r0, bsz), :] = nr
                bui_scr[c, pl.ds(r0, bsz), :] = ni
                out += [nr, ni]
            return tuple(out)

        init = tuple(st_scr[d, k, p] for (d, k) in chains for p in range(2))
        final = lax.fori_loop(0, steps, step, init, unroll=2)
        for c, (d, k) in enumerate(chains):
            st_scr[d, k, 0] = final[2 * c]
            st_scr[d, k, 1] = final[2 * c + 1]
            y = (_dot(bur_scr[c].astype(BF16), cre_ref[d, k]) - _dot(bui_scr[c].astype(BF16), cim_ref[d, k]))
            y_refs[d][:, k * S5_BLOCK_IN:(k + 1) * S5_BLOCK_IN] = y


def _s5_scan(u_tm, params, bsz, s_tot, n_ctx):
    a_re, a_im, bb_re, bb_im, cc_re, cc_im = params
    assert bsz == 8
    rows = S5_CHUNK * bsz
    n_chunks = s_tot // S5_CHUNK
    ctx_chunks = n_ctx // S5_CHUNK

    def bwd_chunk(i):
        return jnp.where(i < ctx_chunks, ctx_chunks - 1 - i, n_chunks - 1 + ctx_chunks - i)

    full = lambda a: pl.BlockSpec(a.shape, lambda i: (0,) * a.ndim)
    shp = jax.ShapeDtypeStruct((s_tot * bsz, SSM_WIDTH), F32)
    return pl.pallas_call(
        functools.partial(_s5_kernel, steps=S5_CHUNK, bsz=bsz),
        grid=(n_chunks,),
        in_specs=[pl.BlockSpec((rows, SSM_WIDTH), lambda i: (i, 0)),
                  pl.BlockSpec((rows, SSM_WIDTH), lambda i: (bwd_chunk(i), 0)),
                  full(a_re), full(a_im), full(bb_re), full(bb_im), full(cc_re), full(cc_im)],
        out_specs=[pl.BlockSpec((rows, SSM_WIDTH), lambda i: (i, 0)),
                   pl.BlockSpec((rows, SSM_WIDTH), lambda i: (bwd_chunk(i), 0))],
        out_shape=[shp, shp],
        scratch_shapes=[pltpu.VMEM((S5_CHAINS, rows, S5_BLOCK_STATE), F32),
                        pltpu.VMEM((S5_CHAINS, rows, S5_BLOCK_STATE), F32),
                        pltpu.VMEM((2, S5_BLOCKS, 2, bsz, S5_BLOCK_STATE), F32)],
        compiler_params=_cparams(("arbitrary",)),
        name="s5_scan",
    )(u_tm, u_tm, a_re, a_im, bb_re, bb_im, cc_re, cc_im)


def _s5_post_kernel(u_ref, yf_ref, yb_ref, d_ref, w_ref, o_ref):
    y = d_ref[...] * u_ref[...] + yf_ref[...] + yb_ref[...]
    g = _gelu(y).astype(BF16)
    gate = _sigmoid(_dot(g, w_ref[...]))
    o_ref[...] = (g.astype(F32) * gate).astype(BF16)


def _s5_post(u_tm, yf, yb, d_skip, w_glu_bf16):
    n, w = u_tm.shape
    tm = S5_POST_ROWS
    row = pl.BlockSpec((tm, w), lambda i: (i, 0))
    return pl.pallas_call(
        _s5_post_kernel,
        grid=(n // tm,),
        in_specs=[row, row, row, pl.BlockSpec((1, w), lambda i: (0, 0)), pl.BlockSpec((w, w), lambda i: (0, 0))],
        out_specs=row,
        out_shape=jax.ShapeDtypeStruct((n, w), BF16),
        compiler_params=_cparams(("arbitrary",)),
        name="s5_post",
    )(u_tm, yf, yb, d_skip.reshape(1, w), w_glu_bf16)


def _rope_tables(lat_len):
    rows = lat_len // GRID_W
    row = jnp.repeat(jnp.arange(rows, dtype=F32), GRID_W)
    col = jnp.tile(jnp.arange(GRID_W, dtype=F32), rows)
    inv = ROPE_BASE ** (-jnp.arange(ROPE_FREQS, dtype=F32) / ROPE_FREQS)
    ang_r = row[:, None] * inv
    ang_c = col[:, None] * inv
    cos = jnp.concatenate([jnp.cos(ang_r), jnp.cos(ang_r), jnp.cos(ang_c), jnp.cos(ang_c)], axis=1)
    sin = jnp.concatenate([-jnp.sin(ang_r), jnp.sin(ang_r), -jnp.sin(ang_c), jnp.sin(ang_c)], axis=1)
    return jnp.tile(cos, (1, 2)), jnp.tile(sin, (1, 2))


def _head_rms(x, ones_bd, g):
    hi, lo = _split2(x * x)
    ms = (_dot(hi, ones_bd) + _dot(lo, ones_bd)) * (1.0 / ATT_HEAD_DIM)
    return x * lax.rsqrt(ms + EPS) * g


def _rope(x, cos, sin_signed):
    lane = lax.broadcasted_iota(jnp.int32, x.shape, 1)
    first_half = (lane % (2 * ROPE_FREQS)) < ROPE_FREQS
    partner = jnp.where(first_half, pltpu.roll(x, LANES - ROPE_FREQS, 1), pltpu.roll(x, ROPE_FREQS, 1))
    return x * cos + partner * sin_signed


def _qkprep_kernel(q_ref, kv_ref, cos_ref, sin_ref, qg_ref, kg_ref, qc_ref, ql_ref, ko_ref, vo_ref, *, n_ctx, chunk):
    s_tot = q_ref.shape[0]
    r_i = lax.broadcasted_iota(jnp.int32, (LANES, LANES), 0) // ATT_HEAD_DIM
    c_i = lax.broadcasted_iota(jnp.int32, (LANES, LANES), 1) // ATT_HEAD_DIM
    ones_bd = jnp.where(r_i == c_i, 1.0, 0.0).astype(BF16)
    lane = lax.broadcasted_iota(jnp.int32, (chunk, LANES), 1)
    low = lane < ATT_HEAD_DIM
    for r0 in range(0, s_tot, chunk):
        roped = r0 >= n_ctx
        if roped:
            cos = cos_ref[r0 - n_ctx:r0 - n_ctx + chunk, :]
            sin = sin_ref[r0 - n_ctx:r0 - n_ctx + chunk, :]
        for s in range(ATT_WIDTH // LANES):
            x = _head_rms(q_ref[r0:r0 + chunk, s * LANES:(s + 1) * LANES], ones_bd, qg_ref[...])
            if roped:
                x = _rope(x, cos, sin)
            qo_ref, q0 = (ql_ref, r0 - n_ctx) if roped else (qc_ref, r0)
            qo_ref[q0:q0 + chunk, s * LANES:(s + 1) * LANES] = (x * (ATT_HEAD_DIM ** -0.5)).astype(BF16)
        k = _head_rms(kv_ref[r0:r0 + chunk, 0:LANES], ones_bd, kg_ref[...])
        if roped:
            k = _rope(k, cos, sin)
        v = kv_ref[r0:r0 + chunk, LANES:2 * LANES]
        k_sw = pltpu.roll(k, ATT_HEAD_DIM, 1)
        v_sw = pltpu.roll(v, ATT_HEAD_DIM, 1)
        zero = jnp.zeros_like(k)
        ks = (jnp.where(low, k, zero), jnp.where(low, zero, k_sw), jnp.where(low, k_sw, zero), jnp.where(low, zero, k))
        vs = (jnp.where(low, v, zero), jnp.where(low, zero, v_sw), jnp.where(low, v_sw, zero), jnp.where(low, zero, v))
        for n in range(4):
            ko_ref[0, n, r0:r0 + chunk, :] = ks[n].astype(BF16)
            vo_ref[0, n, r0:r0 + chunk, :] = vs[n].astype(BF16)


def _qkprep(proj, cos, sin, q_g, k_g, bsz, s_tot, n_ctx):
    qg = jnp.tile(q_g, 2).reshape(1, LANES)
    kg = jnp.tile(k_g, 2).reshape(1, LANES)
    kv_shape = jax.ShapeDtypeStruct((bsz, 4, s_tot, LANES), BF16)
    kv_spec = pl.BlockSpec((1, 4, s_tot, LANES), lambda b: (b, 0, 0, 0))
    return pl.pallas_call(
        functools.partial(_qkprep_kernel, n_ctx=n_ctx, chunk=ROW_CHUNK),
        grid=(bsz,),
        in_specs=[pl.BlockSpec((s_tot, PROJ_BLOCK), lambda b: (b, COL_QA // PROJ_BLOCK)),
                  pl.BlockSpec((s_tot, PROJ_BLOCK), lambda b: (b, COL_KA // PROJ_BLOCK)),
                  pl.BlockSpec(cos.shape, lambda b: (0, 0)), pl.BlockSpec(sin.shape, lambda b: (0, 0)),
                  pl.BlockSpec((1, LANES), lambda b: (0, 0)), pl.BlockSpec((1, LANES), lambda b: (0, 0))],
        out_specs=[pl.BlockSpec((n_ctx, ATT_WIDTH), lambda b: (b, 0)),
                   pl.BlockSpec((s_tot - n_ctx, ATT_WIDTH), lambda b: (b, 0)), kv_spec, kv_spec],
        out_shape=[jax.ShapeDtypeStruct((bsz * n_ctx, ATT_WIDTH), BF16),
                   jax.ShapeDtypeStruct((bsz * (s_tot - n_ctx), ATT_WIDTH), BF16), kv_shape, kv_shape],
        compiler_params=_cparams(("arbitrary",)),
        name="qk_prep",
    )(proj, proj, cos, sin, qg, kg)


def _attend(q_ref, k_ref, v_ref, o_ref, n_keys):
    tq = q_ref.shape[0]
    for hk in range(ATT_KV_HEADS):
        qs = jnp.concatenate([q_ref[:, (2 * hk) * LANES:(2 * hk + 1) * LANES],
                              q_ref[:, (2 * hk + 1) * LANES:(2 * hk + 2) * LANES]], axis=0)
        acc = jnp.zeros((2 * tq, LANES), F32)
        for p in range(2):
            s = _dot_nt(qs, k_ref[0, 2 * hk + p, 0:n_keys, :])
            m = jnp.max(s, axis=-1, keepdims=True)
            e = jnp.exp(s - m)
            l = jnp.sum(e, axis=-1, keepdims=True)
            acc = acc + _dot(e.astype(BF16), v_ref[0, 2 * hk + p, 0:n_keys, :]) / l
        o_ref[:, (2 * hk) * LANES:(2 * hk + 1) * LANES] = acc[0:tq].astype(BF16)
        o_ref[:, (2 * hk + 1) * LANES:(2 * hk + 2) * LANES] = acc[tq:2 * tq].astype(BF16)


def _attn_kernel(q_ref, k_ref, v_ref, o_ref, *, n_keys):
    _attend(q_ref, k_ref, v_ref, o_ref, n_keys)


def _attention(q, kn, vn, bsz, s_tot, tq, n_keys):
    blocks = q.shape[0] // bsz // tq
    kv_spec = pl.BlockSpec((1, 4, s_tot, LANES), lambda b, i: (b, 0, 0, 0))
    qo_spec = pl.BlockSpec((tq, ATT_WIDTH), lambda b, i: (b * blocks + i, 0))
    return pl.pallas_call(
        functools.partial(_attn_kernel, n_keys=n_keys),
        grid=(bsz, blocks),
        in_specs=[qo_spec, kv_spec, kv_spec],
        out_specs=qo_spec,
        out_shape=jax.ShapeDtypeStruct(q.shape, BF16),
        compiler_params=_cparams(("arbitrary", "arbitrary")),
        name="attention",
    )(q, kn, vn)


def _log_sigmoid(x):
    return jnp.minimum(x, 0.0) - jnp.log(1.0 + jnp.exp(-jnp.abs(x)))


def _mlstm_kernel(qf_ref, kf_ref, vf_ref, gf_ref, qb_ref, kb_ref, vb_ref, gb_ref, bias_ref,
                  hf_ref, hb_ref, c_scr, n_scr, m_scr):
    i = pl.program_id(1)
    t = ML_CHUNK

    @pl.when(i == 0)
    def _():
        c_scr[...] = jnp.zeros_like(c_scr)
        n_scr[...] = jnp.zeros_like(n_scr)
        m_scr[...] = jnp.zeros_like(m_scr)

    r_i = lax.broadcasted_iota(jnp.int32, (t, t), 0)
    c_i = lax.broadcasted_iota(jnp.int32, (t, t), 1)
    lower = r_i >= c_i
    upper = r_i <= c_i
    lower_m = jnp.where(lower, 1.0, 0.0).astype(BF16)

    for d in range(2):
        q_ref, k_ref, v_ref, g_ref, h_ref = ((qf_ref, kf_ref, vf_ref, gf_ref, hf_ref),
                                             (qb_ref, kb_ref, vb_ref, gb_ref, hb_ref))[d]
        g = g_ref[...] + bias_ref[...]
        g_t = g.T
        lf = _log_sigmoid(g)
        causal = lower if d == 0 else upper
        lf_hi, lf_lo = _split2(lf)
        prefix = _dot(lower_m, lf_hi) + _dot(lower_m, lf_lo)
        b_cols = prefix if d == 0 else prefix[t - 1:t, :] - prefix + lf
        b_rows = b_cols.T
        last = t - 1 if d == 0 else 0
        for h in range(ML_HEADS):
            ci = d * 2 * ML_HEADS + h
            cf = ci + ML_HEADS
            i_col = g[:, ci:ci + 1]
            i_row = g_t[ci:ci + 1, :]
            b_col = b_cols[:, cf:cf + 1]
            b_row = b_rows[cf:cf + 1, :]
            b_last = b_cols[last:last + 1, cf:cf + 1]
            m_prev = m_scr[d, h]
            c_prev = c_scr[d, h]
            n_prev = n_scr[d, h]

            q = q_ref[:, h * ML_HEAD_DIM:(h + 1) * ML_HEAD_DIM]
            k = k_ref[:, h * ML_HEAD_DIM:(h + 1) * ML_HEAD_DIM] * (ML_HEAD_DIM ** -0.5)
            v = v_ref[:, h * ML_HEAD_DIM:(h + 1) * ML_HEAD_DIM]
            q16, k16, v16 = q.astype(BF16), k.astype(BF16), v.astype(BF16)

            dmat = jnp.where(causal, b_col - b_row + i_row, NEG_INF)
            inter = b_col + m_prev
            m_t = jnp.maximum(inter, jnp.max(dmat, axis=-1, keepdims=True))
            w = jnp.exp(dmat - m_t)
            a_inter = jnp.exp(inter - m_t)
            qk = _dot_nt(q16, k16) * w
            num = a_inter * _dot(q16, c_prev.astype(BF16)) + _dot(qk.astype(BF16), v16)
            den = a_inter * jnp.sum(q * n_prev, axis=-1, keepdims=True) + jnp.sum(qk, axis=-1, keepdims=True)
            h_ref[:, h * ML_HEAD_DIM:(h + 1) * ML_HEAD_DIM] = num / jnp.maximum(jnp.abs(den), jnp.exp(-m_t))

            d_last_col = b_last - b_col + i_col
            m_new = jnp.maximum(b_last + m_prev, jnp.max(d_last_col, axis=0, keepdims=True))
            w_last = jnp.exp(d_last_col - m_new)
            decay = jnp.exp(b_last + m_prev - m_new)
            kw = k * w_last
            c_scr[d, h] = decay * c_prev + _dot(kw.T.astype(BF16), v16)
            n_scr[d, h] = decay * n_prev + jnp.sum(kw, axis=0, keepdims=True)
            m_scr[d, h] = m_new


def _mlstm(proj, gate_b, bsz, s_tot, n_ctx):
    t = ML_CHUNK
    n_chunks = s_tot // t
    ctx_chunks = n_ctx // t

    def fwd(b, i):
        return b * n_chunks + i

    def bwd(b, i):
        return b * n_chunks + jnp.where(i < ctx_chunks, ctx_chunks - 1 - i, n_chunks - 1 + ctx_chunks - i)

    def col(c0, width):
        return c0 // width

    def specs(rowfn):
        return [pl.BlockSpec((t, ML_WIDTH), lambda b, i: (rowfn(b, i), col(COL_QM, ML_WIDTH))),
                pl.BlockSpec((t, ML_WIDTH), lambda b, i: (rowfn(b, i), col(COL_KM, ML_WIDTH))),
                pl.BlockSpec((t, ML_WIDTH), lambda b, i: (rowfn(b, i), col(COL_VM, ML_WIDTH))),
                pl.BlockSpec((t, LANES), lambda b, i: (rowfn(b, i), col(COL_GM, LANES)))]

    bias = jnp.zeros((1, LANES), F32).at[0, :ML_GATES].set(gate_b.reshape(ML_GATES))
    shp = jax.ShapeDtypeStruct((bsz * s_tot, ML_WIDTH), F32)
    return pl.pallas_call(
        _mlstm_kernel,
        grid=(bsz, n_chunks),
        in_specs=specs(fwd) + specs(bwd) + [pl.BlockSpec((1, LANES), lambda b, i: (0, 0))],
        out_specs=[pl.BlockSpec((t, ML_WIDTH), lambda b, i: (fwd(b, i), 0)),
                   pl.BlockSpec((t, ML_WIDTH), lambda b, i: (bwd(b, i), 0))],
        out_shape=[shp, shp],
        scratch_shapes=[pltpu.VMEM((2, ML_HEADS, ML_HEAD_DIM, ML_HEAD_DIM), F32),
                        pltpu.VMEM((2, ML_HEADS, 1, ML_HEAD_DIM), F32),
                        pltpu.VMEM((2, ML_HEADS, 1, 1), F32)],
        compiler_params=_cparams(("arbitrary", "arbitrary")),
        name="mlstm",
    )(*([proj] * 8), bias)


def _merge_kernel(ys_ref, ya_ref, hf_ref, hb_ref, om_ref, gl_ref, x_ref, mod_ref, mg_ref, n2_ref,
                  wbs_ref, wba_ref, wbm_ref, wo_ref, xo_ref, h2_ref, *, blocks_per_batch, first_block):
    n = pl.program_id(0)
    b = n // (blocks_per_batch - first_block)
    if not first_block:
        b = jnp.where(n % blocks_per_batch == 0, CTX_MOD_ROW, b)
    is_ctx = None

    hs = hf_ref[...] + hb_ref[...]
    parts = []
    for h in range(ML_HEADS):
        sl = slice(h * ML_HEAD_DIM, (h + 1) * ML_HEAD_DIM)
        parts.append(_rms(hs[:, sl], mg_ref[:, sl]))
    hn = jnp.concatenate(parts, axis=1) * _sigmoid(om_ref[...])

    gl = gl_ref[...]
    merged = (_sigmoid(gl[:, 0:D_MODEL]) * _dot(ys_ref[...], wbs_ref[...])
              + _sigmoid(gl[:, D_MODEL:2 * D_MODEL]) * _dot(ya_ref[...], wba_ref[...])
              + _sigmoid(gl[:, 2 * D_MODEL:3 * D_MODEL]) * _dot(hn.astype(BF16), wbm_ref[...]))
    mix = _dot(merged.astype(BF16), wo_ref[...])
    g1 = _mod_rows(mod_ref, 2, b, is_ctx)
    x = x_ref[...] + g1 * mix
    xo_ref[...] = x
    sh2 = _mod_rows(mod_ref, 3, b, is_ctx)
    sc2 = _mod_rows(mod_ref, 4, b, is_ctx)
    h2_ref[...] = (_rms(x, n2_ref[...]) * (1.0 + sc2) + sh2).T.astype(BF16)


def _merge(ys, ya, hf, hb, proj, stream, mod_l, ml_norm_g, norm2_g, wbs, wba, wbm, wo,
           bsz, s_tot, n_ctx, latent_only):
    tm = n_ctx
    bpb = s_tot // tm
    first = 1 if latent_only else 0
    per = bpb - first

    def rows(n):
        return (n // per) * bpb + n % per + first

    n_blocks = bsz * per
    full = lambda a: pl.BlockSpec(a.shape, lambda n: (0,) * a.ndim)
    wide = lambda w: pl.BlockSpec((tm, w), lambda n: (rows(n), 0))
    mg = ml_norm_g.reshape(1, ML_WIDTH)
    n2 = norm2_g.reshape(1, D_MODEL)
    return pl.pallas_call(
        functools.partial(_merge_kernel, blocks_per_batch=bpb, first_block=first),
        grid=(n_blocks,),
        in_specs=[wide(SSM_WIDTH), pl.BlockSpec((tm, ATT_WIDTH), lambda n: (n, 0)), wide(ML_WIDTH), wide(ML_WIDTH),
                  pl.BlockSpec((tm, ML_WIDTH), lambda n: (rows(n), COL_OM // ML_WIDTH)),
                  pl.BlockSpec((tm, N_BRANCH * D_MODEL), lambda n: (rows(n), COL_GATE // (N_BRANCH * D_MODEL))),
                  wide(D_MODEL), full(mod_l), full(mg), full(n2), full(wbs), full(wba), full(wbm), full(wo)],
        out_specs=[pl.BlockSpec((tm, D_MODEL), lambda n: (n, 0)), pl.BlockSpec((D_MODEL, tm), lambda n: (0, n))],
        out_shape=[jax.ShapeDtypeStruct((n_blocks * tm, D_MODEL), F32),
                   jax.ShapeDtypeStruct((D_MODEL, n_blocks * tm), BF16)],
        compiler_params=_cparams(("arbitrary",)),
        name="merge",
    )(ys, ya, hf, hb, proj, proj, stream, mod_l, mg, n2, wbs, wba, wbm, wo)


def _peer_score_kernel(xt_ref, wq_ref, k1_ref, k2_ref, s1_ref, s2_ref):
    half = PEER_QDIM // 2
    q_t = _dot(wq_ref[...], xt_ref[...])
    for key_ref, s_ref, lo in ((k1_ref, s1_ref, 0), (k2_ref, s2_ref, half)):
        khi, klo = _split2(key_ref[...])
        for h in range(PEER_HEADS):
            qhi, qlo = _split2(q_t[h * PEER_QDIM + lo:h * PEER_QDIM + lo + half, :])
            s_ref[h] = _dot(khi, qhi) + _dot(khi, qlo) + _dot(klo, qhi)


def _peer_scores(h2_t, wq_t, k1, k2):
    n = h2_t.shape[1]
    tm = SCORE_TOKENS
    shp = jax.ShapeDtypeStruct((PEER_HEADS, PEER_KEYS, n), F32)
    out = pl.BlockSpec((PEER_HEADS, PEER_KEYS, tm), lambda i: (0, 0, i))
    key = pl.BlockSpec((PEER_KEYS, PEER_QDIM // 2), lambda i: (0, 0))
    return pl.pallas_call(
        _peer_score_kernel,
        grid=(n // tm,),
        in_specs=[pl.BlockSpec((D_MODEL, tm), lambda i: (0, i)), pl.BlockSpec(wq_t.shape, lambda i: (0, 0)), key, key],
        out_specs=[out, out],
        out_shape=[shp, shp],
        compiler_params=_cparams(("arbitrary",)),
        name="peer_scores",
    )(h2_t, wq_t, k1, k2)


def _top_values(s, k, with_rank):
    work = s
    rank = jnp.full(s.shape, float(PEER_KEYS), F32) if with_rank else None
    vals = []
    for r in range(k):
        m = jnp.max(work, axis=0, keepdims=True)
        hit = work == m
        if with_rank:
            rank = jnp.where(hit, float(r), rank)
        work = jnp.where(hit, NEG_INF, work)
        vals.append(m)
    return vals, rank


def _peer_select_kernel(s1_ref, s2_ref, r2_ref, g2_ref, cnt_ref, e1_ref, v1_scr, v2_scr):
    k = PEER_TOPK
    half = k // 2
    s1 = s1_ref[0]
    s2 = s2_ref[0]
    v1, _ = _top_values(s1, k, with_rank=False)
    v2, rank2 = _top_values(s2, k, with_rank=True)
    for a in range(k):
        v1_scr[a:a + 1, :] = v1[a]
        v2_scr[a:a + 1, :] = v2[a]
    row = lax.broadcasted_iota(jnp.int32, (half, s1.shape[1]), 0)
    pieces = [v1[0] + v2_scr[...]]
    for a in range(1, half):
        pieces.append(jnp.where(row < k // (a + 1), v1[a] + v2_scr[0:half, :], NEG_INF))
    pieces.append(v1_scr[half:k, :] + v2[0])
    top = v1[0] + v2[0]
    z = jnp.zeros_like(top)
    thr = top
    for _ in range(k):
        thr = functools.reduce(jnp.maximum, [jnp.max(p, axis=0, keepdims=True) for p in pieces])
        pieces = [jnp.where(p == thr, NEG_INF, p) for p in pieces]
        z = z + jnp.exp(thr - top)
    cnt = jnp.zeros_like(s1)
    for a in range(k):
        if a == 0:
            sums = v1[0] + v2_scr[...]
        elif a < half:
            sums = jnp.where(row < k // (a + 1), v1[a] + v2_scr[0:half, :], NEG_INF)
        else:
            sums = v1[a] + v2[0]
        n_sel = jnp.sum(jnp.where(sums >= thr, 1.0, 0.0), axis=0, keepdims=True)
        cnt = jnp.where(s1 == v1[a], n_sel, cnt)
    r2_ref[0] = rank2.astype(BF16)
    g2_ref[0] = jnp.exp(s2 - v2[0]).astype(BF16)
    cnt_ref[0] = cnt
    e1_ref[0] = jnp.exp(s1 - v1[0]) / z


def _peer_select(s1, s2):
    heads, keys, n = s1.shape
    tl = SELECT_TOKENS
    spec = pl.BlockSpec((1, keys, tl), lambda h, i: (h, 0, i))
    shp = jax.ShapeDtypeStruct((heads, keys, n), F32)
    shp16 = jax.ShapeDtypeStruct((heads, keys, n), BF16)
    return pl.pallas_call(
        _peer_select_kernel,
        grid=(heads, n // tl),
        in_specs=[spec, spec], out_specs=[spec] * 4, out_shape=[shp16, shp16, shp, shp],
        scratch_shapes=[pltpu.VMEM((PEER_TOPK, tl), F32), pltpu.VMEM((PEER_TOPK, tl), F32)],
        compiler_params=_cparams(("arbitrary", "arbitrary")),
        name="peer_select",
    )(s1, s2)


def _peer_dense_kernel(x_ref, u_ref, vt_ref, r2_ref, g2_ref, cnt_ref, e1_ref, s_ref, mod_ref, fg_ref,
                       o_ref, acc_scr, a_scr, coef_even, coef_odd, x_scr, r2_scr, g2_scr, row_scr,
                       *, te, nj, blocks_per_batch, n_ctx, final):
    n = pl.program_id(0)
    tm = x_ref.shape[1]
    n_tiles = tm // PEER_TOK
    keys_per_step = te // PEER_KEYS
    assert keys_per_step % F32_SUBLANES == 0

    @pl.when(n == 0)
    def _():
        acc_scr[...] = jnp.zeros_like(acc_scr)
        coef_odd[...] = jnp.zeros_like(coef_odd)

    @pl.when(n % nj == 0)
    def _():
        for t in range(n_tiles):
            tok = slice(t * PEER_TOK, (t + 1) * PEER_TOK)
            x_scr[t] = x_ref[:, tok]
            r2_scr[t] = r2_ref[:, :, tok]
            g2_scr[t] = g2_ref[:, :, tok]

    for t in range(n_tiles):
        tok = slice(t * PEER_TOK, (t + 1) * PEER_TOK)
        row_scr[t, 0] = cnt_ref[:, :, tok]
        row_scr[t, 1] = e1_ref[:, :, tok]

    def step(read_ref, write_ref):
        zero = jnp.zeros((PEER_KEYS, LANES), BF16)

        def tile(t, carry):
            a_scr[...] = _dot(pltpu.bitcast(u_ref[...], BF16), x_scr[t]).astype(BF16)
            acc_scr[t] += _dot(pltpu.bitcast(vt_ref[...], BF16), read_ref[t])
            for s in range(keys_per_step):
                rows = slice(s * PEER_KEYS, (s + 1) * PEER_KEYS)
                for l in range(PEER_TOK // LANES):
                    lanes = slice(l * LANES, (l + 1) * LANES)
                    w = zero
                    for h in range(PEER_HEADS):
                        cnt = row_scr[t, 0, h, s:s + 1, lanes].astype(BF16)
                        e1 = row_scr[t, 1, h, s:s + 1, lanes].astype(BF16)
                        w = w + jnp.where(r2_scr[t, h, :, lanes] < cnt, g2_scr[t, h, :, lanes], zero) * e1
                    write_ref[t, rows, lanes] = w * _gelu(a_scr[rows, lanes])
            return carry

        lax.fori_loop(0, n_tiles, tile, 0)

    @pl.when(n % 2 == 0)
    def _():
        step(coef_odd, coef_even)

    @pl.when(n % 2 == 1)
    def _():
        step(coef_even, coef_odd)

    prev = n - 1

    @pl.when(jnp.logical_and(n > 0, prev % nj == nj - 1))
    def _():
        ip = prev // nj
        b = ip // blocks_per_batch
        for t in range(n_tiles):
            tok = slice(t * PEER_TOK, (t + 1) * PEER_TOK)
            if n_ctx:
                row = (ip % blocks_per_batch) * tm + t * PEER_TOK + lax.broadcasted_iota(jnp.int32, (PEER_TOK, 1), 0)
                is_ctx = row < n_ctx
            else:
                is_ctx = None
            g2 = _mod_rows(mod_ref, 5, b, is_ctx)
            x = s_ref[tok, :] + g2 * acc_scr[t].T
            if final:
                x = _rms(x, fg_ref[...])
            o_ref[tok, :] = x
        acc_scr[...] = jnp.zeros_like(acc_scr)


def _pack_tables_kernel(u_ref, v_ref, uo_ref, vo_ref):
    uo_ref[...] = pltpu.bitcast(u_ref[...].astype(BF16), F32)
    vo_ref[...] = pltpu.bitcast(v_ref[...].T.astype(BF16), F32)


def _pack_tables(u, v):
    e, d = u.shape
    te = PACK_EXPERTS
    return pl.pallas_call(
        _pack_tables_kernel,
        grid=(e // te,),
        in_specs=[pl.BlockSpec((te, d), lambda j: (j, 0)), pl.BlockSpec((te, d), lambda j: (j, 0))],
        out_specs=[pl.BlockSpec((te // 2, d), lambda j: (j, 0)), pl.BlockSpec((d // 2, te), lambda j: (0, j))],
        out_shape=[jax.ShapeDtypeStruct((e // 2, d), F32), jax.ShapeDtypeStruct((d // 2, e), F32)],
        compiler_params=_cparams(("arbitrary",)),
        name="pack_tables",
    )(u, v)


def _peer_dense(h2_t, u_words, vt_words, sel, stream, mod_l, final_g, rows_per_batch, n_ctx, final):
    n = h2_t.shape[1]
    r2, g2, cnt, e1 = sel
    tm = next(t for t in DENSE_TOKENS if rows_per_batch % t == 0)
    te = DENSE_EXPERTS[tm]
    bpb = rows_per_batch // tm
    nj = PEER_EXPERTS // te
    n_steps = (n // tm) * nj

    def cur(s):
        c = jnp.minimum(s, n_steps - 1)
        return c // nj, c % nj

    def prev(s):
        p = jnp.maximum(s - 1, 0)
        return p // nj, p % nj

    tok = pl.BlockSpec((PEER_HEADS, PEER_KEYS, tm), lambda s: (0, 0, cur(s)[0]))
    sub = pl.BlockSpec((PEER_HEADS, te // PEER_KEYS, tm), lambda s: (0, cur(s)[1], cur(s)[0]))
    full = lambda a: pl.BlockSpec(a.shape, lambda s: (0,) * a.ndim)
    fg = final_g.reshape(1, D_MODEL)
    n_tiles = tm // PEER_TOK
    coef = pltpu.VMEM((n_tiles, te, PEER_TOK), BF16)
    sel16 = pltpu.VMEM((n_tiles, PEER_HEADS, PEER_KEYS, PEER_TOK), BF16)
    return pl.pallas_call(
        functools.partial(_peer_dense_kernel, te=te, nj=nj, blocks_per_batch=bpb, n_ctx=n_ctx, final=final),
        grid=(n_steps + 1,),
        in_specs=[pl.BlockSpec((D_MODEL, tm), lambda s: (0, cur(s)[0])),
                  pl.BlockSpec((te // 2, D_MODEL), lambda s: (cur(s)[1], 0)),
                  pl.BlockSpec((D_MODEL // 2, te), lambda s: (0, prev(s)[1])),
                  tok, tok, sub, sub,
                  pl.BlockSpec((tm, D_MODEL), lambda s: (prev(s)[0], 0)), full(mod_l), full(fg)],
        out_specs=pl.BlockSpec((tm, D_MODEL), lambda s: (prev(s)[0], 0)),
        out_shape=jax.ShapeDtypeStruct((n, D_MODEL), F32),
        scratch_shapes=[pltpu.VMEM((n_tiles, D_MODEL, PEER_TOK), F32), pltpu.VMEM((te, PEER_TOK), BF16), coef, coef,
                        pltpu.VMEM((n_tiles, D_MODEL, PEER_TOK), BF16), sel16, sel16,
                        pltpu.VMEM((n_tiles, 2, PEER_HEADS, te // PEER_KEYS, PEER_TOK), F32)],
        compiler_params=_cparams(("arbitrary",)),
        name="peer_dense",
    )(h2_t, u_words, vt_words, r2, g2, cnt, e1, stream, mod_l, fg)


def _reorder_w_in(w):
    offs = [0]
    for width in (SSM_WIDTH, ATT_WIDTH, KV_WIDTH, KV_WIDTH, ML_WIDTH, ML_WIDTH, ML_WIDTH, ML_WIDTH, ML_GATES,
                  N_BRANCH * D_MODEL):
        offs.append(offs[-1] + width)
    u_s, q_a, k_a, v_a, q_m, k_m, v_m, o_m, g_m, gate = (w[:, offs[n]:offs[n + 1]] for n in range(10))
    pad = jnp.zeros((w.shape[0], PROJ_WIDTH - COL_GM - ML_GATES), w.dtype)
    return jnp.concatenate([u_s, q_a, q_m, k_m, v_m, o_m, gate, k_a, v_a, g_m, pad], axis=1).astype(BF16)


def kernel(x, c, ctx, c_ctx, w_mod, b_mod, norm1_g, norm2_g, w_in, ssm_lam_re, ssm_lam_im, ssm_log_step, ssm_b_re, ssm_b_im, ssm_c_re, ssm_c_im, ssm_d, ssm_w_glu, attn_q_norm_g, attn_k_norm_g, mlstm_gate_b, mlstm_norm_g, w_branch_ssm, w_branch_attn, w_branch_mlstm, w_out, peer_w_q, peer_sub_k1, peer_sub_k2, peer_u, peer_v, final_norm_g):
    bsz, lat_len, d = x.shape
    n_ctx = ctx.shape[1]
    s_tot = n_ctx + lat_len
    depth = w_in.shape[0]
    assert d == D_MODEL and n_ctx == 256 and lat_len % n_ctx == 0 and bsz == 8

    cos, sin = _rope_tables(lat_len)
    mod = _modulation(c, c_ctx, w_mod, b_mod)
    stream = jnp.concatenate([ctx, x], axis=1)

    for layer in range(depth):
        last = layer == depth - 1
        mod_l = mod[layer]
        proj = _inproj(stream.reshape(bsz, s_tot, d), norm1_g[layer], mod_l, _reorder_w_in(w_in[layer]), n_ctx)

        u_tm = proj[:, COL_U:COL_U + SSM_WIDTH].reshape(bsz, s_tot, SSM_WIDTH).transpose(1, 0, 2)
        u_tm = u_tm.reshape(s_tot * bsz, SSM_WIDTH)
        params = _s5_params(ssm_lam_re[layer], ssm_lam_im[layer], ssm_log_step[layer], ssm_b_re[layer],
                            ssm_b_im[layer], ssm_c_re[layer], ssm_c_im[layer])
        yf, yb = _s5_scan(u_tm, params, bsz, s_tot, n_ctx)
        ys_tm = _s5_post(u_tm, yf, yb, ssm_d[layer], ssm_w_glu[layer].astype(BF16))
        ys = ys_tm.reshape(s_tot, bsz, SSM_WIDTH).transpose(1, 0, 2).reshape(bsz * s_tot, SSM_WIDTH)

        q_ctx, q_lat, kn, vn = _qkprep(proj, cos, sin, attn_q_norm_g[layer], attn_k_norm_g[layer], bsz, s_tot, n_ctx)
        ya = _attention(q_lat, kn, vn, bsz, s_tot, ATT_Q_BLOCK, s_tot)
        if not last:
            ya_ctx = _attention(q_ctx, kn, vn, bsz, s_tot, n_ctx, n_ctx)
            ya = jnp.concatenate([ya_ctx.reshape(bsz, n_ctx, ATT_WIDTH), ya.reshape(bsz, lat_len, ATT_WIDTH)],
                                 axis=1).reshape(bsz * s_tot, ATT_WIDTH)

        hf, hb = _mlstm(proj, mlstm_gate_b[layer], bsz, s_tot, n_ctx)

        stream2, h2_t = _merge(ys, ya, hf, hb, proj, stream.reshape(bsz * s_tot, d), mod_l, mlstm_norm_g[layer],
                             norm2_g[layer], w_branch_ssm[layer].astype(BF16), w_branch_attn[layer].astype(BF16),
                             w_branch_mlstm[layer].astype(BF16), w_out[layer].astype(BF16),
                             bsz, s_tot, n_ctx, last)

        s1, s2 = _peer_scores(h2_t, peer_w_q[layer].T.astype(BF16), peer_sub_k1[layer], peer_sub_k2[layer])
        sel = _peer_select(s1, s2)
        rows_per_batch = lat_len if last else s_tot
        u_words, vt_words = _pack_tables(peer_u[layer], peer_v[layer])
        stream = _peer_dense(h2_t, u_words, vt_words, sel, stream2, mod_l,
                             final_norm_g, rows_per_batch, 0 if last else n_ctx, last)

    return stream.reshape(bsz, lat_len, d)
```

```python
import functools
import math

import jax
import jax.numpy as jnp
from jax import lax
from jax.experimental import pallas as pl
from jax.experimental.pallas import tpu as pltpu

F32 = jnp.float32
BF16 = jnp.bfloat16

D_MODEL = 1024
GRID_W = 64
EPS = 1e-6
N_MOD = 6
N_BRANCH = 3
SSM_WIDTH = 512
SSM_GROUP = 16
SSM_GROUPS = 32
SSM_STATE = 64
ATT_HEADS = 8
ATT_KV_HEADS = 2
ATT_HEAD_DIM = 64
ATT_WIDTH = 512
ATT_Q_BLOCK = 512
KV_WIDTH = 128
ROPE_FREQS = 16
ROPE_BASE = 10000.0
ML_HEADS = 4
ML_HEAD_DIM = 128
ML_WIDTH = 512
ML_GATES = 16
ML_CHUNK = 256
PEER_HEADS = 8
PEER_KEYS = 128
PEER_EXPERTS = PEER_KEYS * PEER_KEYS
PEER_QDIM = 256
PEER_TOPK = 16
PEER_TOK = 256

LANES = 128
F32_SUBLANES = 8
BF16_SUBLANES = 16
VMEM_LIMIT_BYTES = 56 * 1024 * 1024

PROJ_BLOCK = 512
COL_U, COL_QA, COL_QM, COL_KM, COL_VM, COL_OM = (i * PROJ_BLOCK for i in range(6))
COL_GATE = 6 * PROJ_BLOCK
COL_KA = COL_GATE + N_BRANCH * D_MODEL
COL_VA = COL_KA + KV_WIDTH
COL_GM = COL_VA + KV_WIDTH
PROJ_WIDTH = COL_KA + PROJ_BLOCK

S5_BLOCKS = 4
S5_BLOCK_IN = SSM_WIDTH // S5_BLOCKS
S5_BLOCK_STATE = SSM_GROUPS * SSM_STATE // S5_BLOCKS
S5_CHUNK = 128
S5_CHAINS = 4

NEG_INF = float("-inf")

ROW_CHUNK = 256
MOD_COLS = 1536
S5_POST_ROWS = 1024
SCORE_TOKENS = 1024
SELECT_TOKENS = 1024
PACK_EXPERTS = 512
DENSE_TOKENS = (768, 512)
DENSE_EXPERTS = {768: 1024, 512: 2048}


def _cparams(sem, flags=None):
    return pltpu.CompilerParams(dimension_semantics=sem, vmem_limit_bytes=VMEM_LIMIT_BYTES, flags=flags)


def _split2(x):
    hi = x.astype(BF16)
    lo = (x - hi.astype(F32)).astype(BF16)
    return hi, lo


def _split3(x):
    hi = x.astype(BF16)
    r = x - hi.astype(F32)
    mid = r.astype(BF16)
    lo = (r - mid.astype(F32)).astype(BF16)
    return hi, mid, lo


def _dot(a, b):
    return jnp.dot(a, b, preferred_element_type=F32)


def _dot_nt(a, b):
    return lax.dot_general(a, b, (((1,), (1,)), ((), ())), preferred_element_type=F32)


def _sigmoid(x):
    return 1.0 / (1.0 + jnp.exp(-x))


def _gelu(x):
    return jax.nn.gelu(x, approximate=True)


def _mod_kernel(v_ref, w_ref, b_ref, o_ref):
    v = v_ref[...]
    sv = v * _sigmoid(v)
    w = w_ref[0]
    hi, mid, lo = _split3(sv)
    whi, wlo = _split2(w)
    acc = _dot(hi, whi) + _dot(mid, whi) + _dot(hi, wlo) + _dot(lo, whi) + _dot(mid, wlo)
    o_ref[0] = acc + b_ref[0]


def _modulation(c, c_ctx, w_mod, b_mod):
    depth = w_mod.shape[0]
    n_out = w_mod.shape[2]
    rows = 16
    v = jnp.zeros((rows, D_MODEL), F32).at[: c.shape[0]].set(c).at[8].set(c_ctx)
    tn = MOD_COLS
    return pl.pallas_call(
        _mod_kernel,
        grid=(depth, n_out // tn),
        in_specs=[pl.BlockSpec((rows, D_MODEL), lambda l, j: (0, 0)),
                  pl.BlockSpec((1, D_MODEL, tn), lambda l, j: (l, 0, j)),
                  pl.BlockSpec((1, 1, tn), lambda l, j: (l, 0, j))],
        out_specs=pl.BlockSpec((1, rows, tn), lambda l, j: (l, 0, j)),
        out_shape=jax.ShapeDtypeStruct((depth, rows, n_out), F32),
        compiler_params=_cparams(("arbitrary", "arbitrary")),
        name="modulation",
    )(v, w_mod, b_mod.reshape(depth, 1, n_out))


CTX_MOD_ROW = 8


def _mod_rows(mod_ref, which, b, is_ctx_col):
    lo = which * D_MODEL
    m_l = mod_ref[pl.ds(b, 1), lo:lo + D_MODEL]
    if is_ctx_col is None:
        return m_l
    m_c = mod_ref[CTX_MOD_ROW:CTX_MOD_ROW + 1, lo:lo + D_MODEL]
    return jnp.where(is_ctx_col, m_c, m_l)


def _rms(x, g):
    ms = jnp.mean(x * x, axis=-1, keepdims=True)
    return x * lax.rsqrt(ms + EPS) * g


def _inproj_kernel(x_ref, g_ref, mod_ref, w_ref, o_ref, h_scr, *, n_ctx, chunk):
    b = pl.program_id(0)
    j = pl.program_id(1)
    s_tot = x_ref.shape[1]

    @pl.when(j == 0)
    def _():
        for r0 in range(0, s_tot, chunk):
            x = x_ref[0, r0:r0 + chunk, :]
            xn = _rms(x, g_ref[...])
            row = r0 + lax.broadcasted_iota(jnp.int32, (chunk, 1), 0)
            is_ctx = row < n_ctx
            sh = _mod_rows(mod_ref, 0, b, is_ctx)
            sc = _mod_rows(mod_ref, 1, b, is_ctx)
            h_scr[r0:r0 + chunk, :] = (xn * (1.0 + sc) + sh).astype(BF16)

    o_ref[...] = _dot(h_scr[...], w_ref[...])


def _inproj(stream3, g, mod_l, w_bf16, n_ctx):
    bsz, s_tot, d = stream3.shape
    n_out = w_bf16.shape[1]
    tn = PROJ_BLOCK
    return pl.pallas_call(
        functools.partial(_inproj_kernel, n_ctx=n_ctx, chunk=ROW_CHUNK),
        grid=(bsz, n_out // tn),
        in_specs=[pl.BlockSpec((1, s_tot, d), lambda b, j: (b, 0, 0)),
                  pl.BlockSpec((1, d), lambda b, j: (0, 0)),
                  pl.BlockSpec(mod_l.shape, lambda b, j: (0, 0)),
                  pl.BlockSpec((d, tn), lambda b, j: (0, j))],
        out_specs=pl.BlockSpec((s_tot, tn), lambda b, j: (b, j)),
        out_shape=jax.ShapeDtypeStruct((bsz * s_tot, n_out), F32),
        scratch_shapes=[pltpu.VMEM((s_tot, d), BF16)],
        compiler_params=_cparams(("arbitrary", "arbitrary")),
        name="inproj",
    )(stream3, g.reshape(1, d), mod_l, w_bf16)


def _s5_param_kernel(lre_ref, lim_ref, ls_ref, bre_ref, bim_ref, are_ref, aim_ref, bbre_ref, bbim_ref):
    lre = lre_ref[...]
    lim = lim_ref[...]
    step = jnp.exp(ls_ref[...])
    mag = jnp.exp(lre * step)
    a_re = mag * jnp.cos(lim * step)
    a_im = mag * jnp.sin(lim * step)
    den = lre * lre + lim * lim
    z_re = ((a_re - 1.0) * lre + a_im * lim) / den
    z_im = (a_im * lre - (a_re - 1.0) * lim) / den
    b_re = bre_ref[...]
    b_im = bim_ref[...]
    are_ref[...] = a_re
    aim_ref[...] = a_im
    bbre_ref[...] = z_re * b_re - z_im * b_im
    bbim_ref[...] = z_re * b_im + z_im * b_re


def _s5_params(lam_re, lam_im, log_step, b_re, b_im, c_re, c_im):
    nd, g, n = lam_re.shape
    c = b_re.shape[-1]
    rows = nd * g
    wide = n * c

    def expand(z):
        return jnp.broadcast_to(z.reshape(rows, n, 1), (rows, n, c)).reshape(rows, wide)

    ls = jnp.broadcast_to(log_step.reshape(rows, 1), (rows, wide))
    spec = pl.BlockSpec((rows, wide), lambda: (0, 0))
    shp = jax.ShapeDtypeStruct((rows, wide), F32)
    a_re, a_im, bb_re, bb_im = pl.pallas_call(
        _s5_param_kernel,
        in_specs=[spec] * 5, out_specs=[spec] * 4, out_shape=[shp] * 4,
        name="s5_params",
    )(expand(lam_re), expand(lam_im), ls, b_re.reshape(rows, wide), b_im.reshape(rows, wide))

    gpb = g // S5_BLOCKS
    eye = jnp.eye(gpb, dtype=F32)

    def diag_in(bb):
        bb = bb.reshape(nd, S5_BLOCKS, gpb, n, c)
        return jnp.einsum("dkgnc,gh->dkgchn", bb, eye).reshape(nd, S5_BLOCKS, gpb * c, gpb * n).astype(BF16)

    def diag_out(cc):
        cc = cc.reshape(nd, S5_BLOCKS, gpb, c, n)
        return jnp.einsum("dkgcn,gh->dkgnhc", cc, eye).reshape(nd, S5_BLOCKS, gpb * n, gpb * c).astype(BF16)

    def decay(a):
        a = a.reshape(nd, S5_BLOCKS, gpb, n, c)[..., 0].reshape(nd, S5_BLOCKS, 1, gpb * n)
        return jnp.broadcast_to(a, (nd, S5_BLOCKS, 8, gpb * n))

    return decay(a_re), decay(a_im), diag_in(bb_re), diag_in(bb_im), diag_out(c_re), diag_out(c_im)


def _s5_kernel(uf_ref, ub_ref, are_ref, aim_ref, bre_ref, bim_ref, cre_ref, cim_ref,
               yf_ref, yb_ref, bur_scr, bui_scr, st_scr, *, steps, bsz):
    i = pl.program_id(0)

    @pl.when(i == 0)
    def _():
        st_scr[...] = jnp.zeros_like(st_scr)

    u_refs = (uf_ref, ub_ref)
    y_refs = (yf_ref, yb_ref)
    per_dir = S5_CHAINS // 2
    for k0 in range(0, S5_BLOCKS, per_dir):
        chains = [(d, k0 + j) for d in range(2) for j in range(per_dir)]
        for c, (d, k) in enumerate(chains):
            u = u_refs[d][:, k * S5_BLOCK_IN:(k + 1) * S5_BLOCK_IN].astype(BF16)
            bur_scr[c] = _dot(u, bre_ref[d, k])
            bui_scr[c] = _dot(u, bim_ref[d, k])

        def step(t, carry, chains=chains):
            out = []
            for c, (d, k) in enumerate(chains):
                sr, si = carry[2 * c], carry[2 * c + 1]
                tt = t if d == 0 else steps - 1 - t
                r0 = pl.multiple_of(tt * bsz, bsz)
                ar = are_ref[d, k]
                ai = aim_ref[d, k]
                nr = ar * sr - ai * si + bur_scr[c, pl.ds(r0, bsz), :]
                ni = ar * si + ai * sr + bui_scr[c, pl.ds(r0, bsz), :]
                bur_scr[c, pl.ds(r0, bsz), :] = nr
                bui_scr[c, pl.ds(r0, bsz), :] = ni
                out += [nr, ni]
            return tuple(out)

        init = tuple(st_scr[d, k, p] for (d, k) in chains for p in range(2))
        final = lax.fori_loop(0, steps, step, init, unroll=2)
        for c, (d, k) in enumerate(chains):
            st_scr[d, k, 0] = final[2 * c]
            st_scr[d, k, 1] = final[2 * c + 1]
            y = (_dot(bur_scr[c].astype(BF16), cre_ref[d, k]) - _dot(bui_scr[c].astype(BF16), cim_ref[d, k]))
            y_refs[d][:, k * S5_BLOCK_IN:(k + 1) * S5_BLOCK_IN] = y


def _s5_scan(u_tm, params, bsz, s_tot, n_ctx):
    a_re, a_im, bb_re, bb_im, cc_re, cc_im = params
    assert bsz == 8
    rows = S5_CHUNK * bsz
    n_chunks = s_tot // S5_CHUNK
    ctx_chunks = n_ctx // S5_CHUNK

    def bwd_chunk(i):
        return jnp.where(i < ctx_chunks, ctx_chunks - 1 - i, n_chunks - 1 + ctx_chunks - i)

    full = lambda a: pl.BlockSpec(a.shape, lambda i: (0,) * a.ndim)
    shp = jax.ShapeDtypeStruct((s_tot * bsz, SSM_WIDTH), F32)
    return pl.pallas_call(
        functools.partial(_s5_kernel, steps=S5_CHUNK, bsz=bsz),
        grid=(n_chunks,),
        in_specs=[pl.BlockSpec((rows, SSM_WIDTH), lambda i: (i, 0)),
                  pl.BlockSpec((rows, SSM_WIDTH), lambda i: (bwd_chunk(i), 0)),
                  full(a_re), full(a_im), full(bb_re), full(bb_im), full(cc_re), full(cc_im)],
        out_specs=[pl.BlockSpec((rows, SSM_WIDTH), lambda i: (i, 0)),
                   pl.BlockSpec((rows, SSM_WIDTH), lambda i: (bwd_chunk(i), 0))],
        out_shape=[shp, shp],
        scratch_shapes=[pltpu.VMEM((S5_CHAINS, rows, S5_BLOCK_STATE), F32),
                        pltpu.VMEM((S5_CHAINS, rows, S5_BLOCK_STATE), F32),
                        pltpu.VMEM((2, S5_BLOCKS, 2, bsz, S5_BLOCK_STATE), F32)],
        compiler_params=_cparams(("arbitrary",)),
        name="s5_scan",
    )(u_tm, u_tm, a_re, a_im, bb_re, bb_im, cc_re, cc_im)


def _s5_post_kernel(u_ref, yf_ref, yb_ref, d_ref, w_ref, o_ref):
    y = d_ref[...] * u_ref[...] + yf_ref[...] + yb_ref[...]
    g = _gelu(y).astype(BF16)
    gate = _sigmoid(_dot(g, w_ref[...]))
    o_ref[...] = (g.astype(F32) * gate).astype(BF16)


def _s5_post(u_tm, yf, yb, d_skip, w_glu_bf16):
    n, w = u_tm.shape
    tm = S5_POST_ROWS
    row = pl.BlockSpec((tm, w), lambda i: (i, 0))
    return pl.pallas_call(
        _s5_post_kernel,
        grid=(n // tm,),
        in_specs=[row, row, row, pl.BlockSpec((1, w), lambda i: (0, 0)), pl.BlockSpec((w, w), lambda i: (0, 0))],
        out_specs=row,
        out_shape=jax.ShapeDtypeStruct((n, w), BF16),
        compiler_params=_cparams(("arbitrary",)),
        name="s5_post",
    )(u_tm, yf, yb, d_skip.reshape(1, w), w_glu_bf16)


def _rope_tables(lat_len):
    rows = lat_len // GRID_W
    row = jnp.repeat(jnp.arange(rows, dtype=F32), GRID_W)
    col = jnp.tile(jnp.arange(GRID_W, dtype=F32), rows)
    inv = ROPE_BASE ** (-jnp.arange(ROPE_FREQS, dtype=F32) / ROPE_FREQS)
    ang_r = row[:, None] * inv
    ang_c = col[:, None] * inv
    cos = jnp.concatenate([jnp.cos(ang_r), jnp.cos(ang_r), jnp.cos(ang_c), jnp.cos(ang_c)], axis=1)
    sin = jnp.concatenate([-jnp.sin(ang_r), jnp.sin(ang_r), -jnp.sin(ang_c), jnp.sin(ang_c)], axis=1)
    return jnp.tile(cos, (1, 2)), jnp.tile(sin, (1, 2))


def _head_rms(x, ones_bd, g):
    hi, lo = _split2(x * x)
    ms = (_dot(hi, ones_bd) + _dot(lo, ones_bd)) * (1.0 / ATT_HEAD_DIM)
    return x * lax.rsqrt(ms + EPS) * g


def _rope(x, cos, sin_signed):
    lane = lax.broadcasted_iota(jnp.int32, x.shape, 1)
    first_half = (lane % (2 * ROPE_FREQS)) < ROPE_FREQS
    partner = jnp.where(first_half, pltpu.roll(x, LANES - ROPE_FREQS, 1), pltpu.roll(x, ROPE_FREQS, 1))
    return x * cos + partner * sin_signed


def _qkprep_kernel(q_ref, kv_ref, cos_ref, sin_ref, qg_ref, kg_ref, qc_ref, ql_ref, ko_ref, vo_ref, *, n_ctx, chunk):
    s_tot = q_ref.shape[0]
    r_i = lax.broadcasted_iota(jnp.int32, (LANES, LANES), 0) // ATT_HEAD_DIM
    c_i = lax.broadcasted_iota(jnp.int32, (LANES, LANES), 1) // ATT_HEAD_DIM
    ones_bd = jnp.where(r_i == c_i, 1.0, 0.0).astype(BF16)
    lane = lax.broadcasted_iota(jnp.int32, (chunk, LANES), 1)
    low = lane < ATT_HEAD_DIM
    for r0 in range(0, s_tot, chunk):
        roped = r0 >= n_ctx
        if roped:
            cos = cos_ref[r0 - n_ctx:r0 - n_ctx + chunk, :]
            sin = sin_ref[r0 - n_ctx:r0 - n_ctx + chunk, :]
        for s in range(ATT_WIDTH // LANES):
            x = _head_rms(q_ref[r0:r0 + chunk, s * LANES:(s + 1) * LANES], ones_bd, qg_ref[...])
            if roped:
                x = _rope(x, cos, sin)
            qo_ref, q0 = (ql_ref, r0 - n_ctx) if roped else (qc_ref, r0)
            qo_ref[q0:q0 + chunk, s * LANES:(s + 1) * LANES] = (x * (ATT_HEAD_DIM ** -0.5)).astype(BF16)
        k = _head_rms(kv_ref[r0:r0 + chunk, 0:LANES], ones_bd, kg_ref[...])
        if roped:
            k = _rope(k, cos, sin)
        v = kv_ref[r0:r0 + chunk, LANES:2 * LANES]
        k_sw = pltpu.roll(k, ATT_HEAD_DIM, 1)
        v_sw = pltpu.roll(v, ATT_HEAD_DIM, 1)
        zero = jnp.zeros_like(k)
        ks = (jnp.where(low, k, zero), jnp.where(low, zero, k_sw), jnp.where(low, k_sw, zero), jnp.where(low, zero, k))
        vs = (jnp.where(low, v, zero), jnp.where(low, zero, v_sw), jnp.where(low, v_sw, zero), jnp.where(low, zero, v))
        for n in range(4):
            ko_ref[0, n, r0:r0 + chunk, :] = ks[n].astype(BF16)
            vo_ref[0, n, r0:r0 + chunk, :] = vs[n].astype(BF16)


def _qkprep(proj, cos, sin, q_g, k_g, bsz, s_tot, n_ctx):
    qg = jnp.tile(q_g, 2).reshape(1, LANES)
    kg = jnp.tile(k_g, 2).reshape(1, LANES)
    kv_shape = jax.ShapeDtypeStruct((bsz, 4, s_tot, LANES), BF16)
    kv_spec = pl.BlockSpec((1, 4, s_tot, LANES), lambda b: (b, 0, 0, 0))
    return pl.pallas_call(
        functools.partial(_qkprep_kernel, n_ctx=n_ctx, chunk=ROW_CHUNK),
        grid=(bsz,),
        in_specs=[pl.BlockSpec((s_tot, PROJ_BLOCK), lambda b: (b, COL_QA // PROJ_BLOCK)),
                  pl.BlockSpec((s_tot, PROJ_BLOCK), lambda b: (b, COL_KA // PROJ_BLOCK)),
                  pl.BlockSpec(cos.shape, lambda b: (0, 0)), pl.BlockSpec(sin.shape, lambda b: (0, 0)),
                  pl.BlockSpec((1, LANES), lambda b: (0, 0)), pl.BlockSpec((1, LANES), lambda b: (0, 0))],
        out_specs=[pl.BlockSpec((n_ctx, ATT_WIDTH), lambda b: (b, 0)),
                   pl.BlockSpec((s_tot - n_ctx, ATT_WIDTH), lambda b: (b, 0)), kv_spec, kv_spec],
        out_shape=[jax.ShapeDtypeStruct((bsz * n_ctx, ATT_WIDTH), BF16),
                   jax.ShapeDtypeStruct((bsz * (s_tot - n_ctx), ATT_WIDTH), BF16), kv_shape, kv_shape],
        compiler_params=_cparams(("arbitrary",)),
        name="qk_prep",
    )(proj, proj, cos, sin, qg, kg)


def _attend(q_ref, k_ref, v_ref, o_ref, n_keys):
    tq = q_ref.shape[0]
    for hk in range(ATT_KV_HEADS):
        qs = jnp.concatenate([q_ref[:, (2 * hk) * LANES:(2 * hk + 1) * LANES],
                              q_ref[:, (2 * hk + 1) * LANES:(2 * hk + 2) * LANES]], axis=0)
        acc = jnp.zeros((2 * tq, LANES), F32)
        for p in range(2):
            s = _dot_nt(qs, k_ref[0, 2 * hk + p, 0:n_keys, :])
            m = jnp.max(s, axis=-1, keepdims=True)
            e = jnp.exp(s - m)
            l = jnp.sum(e, axis=-1, keepdims=True)
            acc = acc + _dot(e.astype(BF16), v_ref[0, 2 * hk + p, 0:n_keys, :]) / l
        o_ref[:, (2 * hk) * LANES:(2 * hk + 1) * LANES] = acc[0:tq].astype(BF16)
        o_ref[:, (2 * hk + 1) * LANES:(2 * hk + 2) * LANES] = acc[tq:2 * tq].astype(BF16)


def _attn_kernel(q_ref, k_ref, v_ref, o_ref, *, n_keys):
    _attend(q_ref, k_ref, v_ref, o_ref, n_keys)


def _attention(q, kn, vn, bsz, s_tot, tq, n_keys):
    blocks = q.shape[0] // bsz // tq
    kv_spec = pl.BlockSpec((1, 4, s_tot, LANES), lambda b, i: (b, 0, 0, 0))
    qo_spec = pl.BlockSpec((tq, ATT_WIDTH), lambda b, i: (b * blocks + i, 0))
    return pl.pallas_call(
        functools.partial(_attn_kernel, n_keys=n_keys),
        grid=(bsz, blocks),
        in_specs=[qo_spec, kv_spec, kv_spec],
        out_specs=qo_spec,
        out_shape=jax.ShapeDtypeStruct(q.shape, BF16),
        compiler_params=_cparams(("arbitrary", "arbitrary")),
        name="attention",
    )(q, kn, vn)


def _log_sigmoid(x):
    return jnp.minimum(x, 0.0) - jnp.log(1.0 + jnp.exp(-jnp.abs(x)))


def _mlstm_kernel(qf_ref, kf_ref, vf_ref, gf_ref, qb_ref, kb_ref, vb_ref, gb_ref, bias_ref,
                  hf_ref, hb_ref, c_scr, n_scr, m_scr):
    i = pl.program_id(1)
    t = ML_CHUNK

    @pl.when(i == 0)
    def _():
        c_scr[...] = jnp.zeros_like(c_scr)
        n_scr[...] = jnp.zeros_like(n_scr)
        m_scr[...] = jnp.zeros_like(m_scr)

    r_i = lax.broadcasted_iota(jnp.int32, (t, t), 0)
    c_i = lax.broadcasted_iota(jnp.int32, (t, t), 1)
    lower = r_i >= c_i
    upper = r_i <= c_i
    lower_m = jnp.where(lower, 1.0, 0.0).astype(BF16)

    for d in range(2):
        q_ref, k_ref, v_ref, g_ref, h_ref = ((qf_ref, kf_ref, vf_ref, gf_ref, hf_ref),
                                             (qb_ref, kb_ref, vb_ref, gb_ref, hb_ref))[d]
        g = g_ref[...] + bias_ref[...]
        g_t = g.T
        lf = _log_sigmoid(g)
        causal = lower if d == 0 else upper
        lf_hi, lf_lo = _split2(lf)
        prefix = _dot(lower_m, lf_hi) + _dot(lower_m, lf_lo)
        b_cols = prefix if d == 0 else prefix[t - 1:t, :] - prefix + lf
        b_rows = b_cols.T
        last = t - 1 if d == 0 else 0
        for h in range(ML_HEADS):
            ci = d * 2 * ML_HEADS + h
            cf = ci + ML_HEADS
            i_col = g[:, ci:ci + 1]
            i_row = g_t[ci:ci + 1, :]
            b_col = b_cols[:, cf:cf + 1]
            b_row = b_rows[cf:cf + 1, :]
            b_last = b_cols[last:last + 1, cf:cf + 1]
            m_prev = m_scr[d, h]
            c_prev = c_scr[d, h]
            n_prev = n_scr[d, h]

            q = q_ref[:, h * ML_HEAD_DIM:(h + 1) * ML_HEAD_DIM]
            k = k_ref[:, h * ML_HEAD_DIM:(h + 1) * ML_HEAD_DIM] * (ML_HEAD_DIM ** -0.5)
            v = v_ref[:, h * ML_HEAD_DIM:(h + 1) * ML_HEAD_DIM]
            q16, k16, v16 = q.astype(BF16), k.astype(BF16), v.astype(BF16)

            dmat = jnp.where(causal, b_col - b_row + i_row, NEG_INF)
            inter = b_col + m_prev
            m_t = jnp.maximum(inter, jnp.max(dmat, axis=-1, keepdims=True))
            w = jnp.exp(dmat - m_t)
            a_inter = jnp.exp(inter - m_t)
            qk = _dot_nt(q16, k16) * w
            num = a_inter * _dot(q16, c_prev.astype(BF16)) + _dot(qk.astype(BF16), v16)
            den = a_inter * jnp.sum(q * n_prev, axis=-1, keepdims=True) + jnp.sum(qk, axis=-1, keepdims=True)
            h_ref[:, h * ML_HEAD_DIM:(h + 1) * ML_HEAD_DIM] = num / jnp.maximum(jnp.abs(den), jnp.exp(-m_t))

            d_last_col = b_last - b_col + i_col
            m_new = jnp.maximum(b_last + m_prev, jnp.max(d_last_col, axis=0, keepdims=True))
            w_last = jnp.exp(d_last_col - m_new)
            decay = jnp.exp(b_last + m_prev - m_new)
            kw = k * w_last
            c_scr[d, h] = decay * c_prev + _dot(kw.T.astype(BF16), v16)
            n_scr[d, h] = decay * n_prev + jnp.sum(kw, axis=0, keepdims=True)
            m_scr[d, h] = m_new


def _mlstm(proj, gate_b, bsz, s_tot, n_ctx):
    t = ML_CHUNK
    n_chunks = s_tot // t
    ctx_chunks = n_ctx // t

    def fwd(b, i):
        return b * n_chunks + i

    def bwd(b, i):
        return b * n_chunks + jnp.where(i < ctx_chunks, ctx_chunks - 1 - i, n_chunks - 1 + ctx_chunks - i)

    def col(c0, width):
        return c0 // width

    def specs(rowfn):
        return [pl.BlockSpec((t, ML_WIDTH), lambda b, i: (rowfn(b, i), col(COL_QM, ML_WIDTH))),
                pl.BlockSpec((t, ML_WIDTH), lambda b, i: (rowfn(b, i), col(COL_KM, ML_WIDTH))),
                pl.BlockSpec((t, ML_WIDTH), lambda b, i: (rowfn(b, i), col(COL_VM, ML_WIDTH))),
                pl.BlockSpec((t, LANES), lambda b, i: (rowfn(b, i), col(COL_GM, LANES)))]

    bias = jnp.zeros((1, LANES), F32).at[0, :ML_GATES].set(gate_b.reshape(ML_GATES))
    shp = jax.ShapeDtypeStruct((bsz * s_tot, ML_WIDTH), F32)
    return pl.pallas_call(
        _mlstm_kernel,
        grid=(bsz, n_chunks),
        in_specs=specs(fwd) + specs(bwd) + [pl.BlockSpec((1, LANES), lambda b, i: (0, 0))],
        out_specs=[pl.BlockSpec((t, ML_WIDTH), lambda b, i: (fwd(b, i), 0)),
                   pl.BlockSpec((t, ML_WIDTH), lambda b, i: (bwd(b, i), 0))],
        out_shape=[shp, shp],
        scratch_shapes=[pltpu.VMEM((2, ML_HEADS, ML_HEAD_DIM, ML_HEAD_DIM), F32),
                        pltpu.VMEM((2, ML_HEADS, 1, ML_HEAD_DIM), F32),
                        pltpu.VMEM((2, ML_HEADS, 1, 1), F32)],
        compiler_params=_cparams(("arbitrary", "arbitrary")),
        name="mlstm",
    )(*([proj] * 8), bias)


def _merge_kernel(ys_ref, ya_ref, hf_ref, hb_ref, om_ref, gl_ref, x_ref, mod_ref, mg_ref, n2_ref,
                  wbs_ref, wba_ref, wbm_ref, wo_ref, xo_ref, h2_ref, *, blocks_per_batch, first_block):
    n = pl.program_id(0)
    b = n // (blocks_per_batch - first_block)
    if not first_block:
        b = jnp.where(n % blocks_per_batch == 0, CTX_MOD_ROW, b)
    is_ctx = None

    hs = hf_ref[...] + hb_ref[...]
    parts = []
    for h in range(ML_HEADS):
        sl = slice(h * ML_HEAD_DIM, (h + 1) * ML_HEAD_DIM)
        parts.append(_rms(hs[:, sl], mg_ref[:, sl]))
    hn = jnp.concatenate(parts, axis=1) * _sigmoid(om_ref[...])

    gl = gl_ref[...]
    merged = (_sigmoid(gl[:, 0:D_MODEL]) * _dot(ys_ref[...], wbs_ref[...])
              + _sigmoid(gl[:, D_MODEL:2 * D_MODEL]) * _dot(ya_ref[...], wba_ref[...])
              + _sigmoid(gl[:, 2 * D_MODEL:3 * D_MODEL]) * _dot(hn.astype(BF16), wbm_ref[...]))
    mix = _dot(merged.astype(BF16), wo_ref[...])
    g1 = _mod_rows(mod_ref, 2, b, is_ctx)
    x = x_ref[...] + g1 * mix
    xo_ref[...] = x
    sh2 = _mod_rows(mod_ref, 3, b, is_ctx)
    sc2 = _mod_rows(mod_ref, 4, b, is_ctx)
    h2_ref[...] = (_rms(x, n2_ref[...]) * (1.0 + sc2) + sh2).T.astype(BF16)


def _merge(ys, ya, hf, hb, proj, stream, mod_l, ml_norm_g, norm2_g, wbs, wba, wbm, wo,
           bsz, s_tot, n_ctx, latent_only):
    tm = n_ctx
    bpb = s_tot // tm
    first = 1 if latent_only else 0
    per = bpb - first

    def rows(n):
        return (n // per) * bpb + n % per + first

    n_blocks = bsz * per
    full = lambda a: pl.BlockSpec(a.shape, lambda n: (0,) * a.ndim)
    wide = lambda w: pl.BlockSpec((tm, w), lambda n: (rows(n), 0))
    mg = ml_norm_g.reshape(1, ML_WIDTH)
    n2 = norm2_g.reshape(1, D_MODEL)
    return pl.pallas_call(
        functools.partial(_merge_kernel, blocks_per_batch=bpb, first_block=first),
        grid=(n_blocks,),
        in_specs=[wide(SSM_WIDTH), pl.BlockSpec((tm, ATT_WIDTH), lambda n: (n, 0)), wide(ML_WIDTH), wide(ML_WIDTH),
                  pl.BlockSpec((tm, ML_WIDTH), lambda n: (rows(n), COL_OM // ML_WIDTH)),
                  pl.BlockSpec((tm, N_BRANCH * D_MODEL), lambda n: (rows(n), COL_GATE // (N_BRANCH * D_MODEL))),
                  wide(D_MODEL), full(mod_l), full(mg), full(n2), full(wbs), full(wba), full(wbm), full(wo)],
        out_specs=[pl.BlockSpec((tm, D_MODEL), lambda n: (n, 0)), pl.BlockSpec((D_MODEL, tm), lambda n: (0, n))],
        out_shape=[jax.ShapeDtypeStruct((n_blocks * tm, D_MODEL), F32),
                   jax.ShapeDtypeStruct((D_MODEL, n_blocks * tm), BF16)],
        compiler_params=_cparams(("arbitrary",)),
        name="merge",
    )(ys, ya, hf, hb, proj, proj, stream, mod_l, mg, n2, wbs, wba, wbm, wo)


def _peer_score_kernel(xt_ref, wq_ref, k1_ref, k2_ref, s1_ref, s2_ref):
    half = PEER_QDIM // 2
    q_t = _dot(wq_ref[...], xt_ref[...])
    for key_ref, s_ref, lo in ((k1_ref, s1_ref, 0), (k2_ref, s2_ref, half)):
        khi, klo = _split2(key_ref[...])
        for h in range(PEER_HEADS):
            qhi, qlo = _split2(q_t[h * PEER_QDIM + lo:h * PEER_QDIM + lo + half, :])
            s_ref[h] = _dot(khi, qhi) + _dot(khi, qlo) + _dot(klo, qhi)


def _peer_scores(h2_t, wq_t, k1, k2):
    n = h2_t.shape[1]
    tm = SCORE_TOKENS
    shp = jax.ShapeDtypeStruct((PEER_HEADS, PEER_KEYS, n), F32)
    out = pl.BlockSpec((PEER_HEADS, PEER_KEYS, tm), lambda i: (0, 0, i))
    key = pl.BlockSpec((PEER_KEYS, PEER_QDIM // 2), lambda i: (0, 0))
    return pl.pallas_call(
        _peer_score_kernel,
        grid=(n // tm,),
        in_specs=[pl.BlockSpec((D_MODEL, tm), lambda i: (0, i)), pl.BlockSpec(wq_t.shape, lambda i: (0, 0)), key, key],
        out_specs=[out, out],
        out_shape=[shp, shp],
        compiler_params=_cparams(("arbitrary",)),
        name="peer_scores",
    )(h2_t, wq_t, k1, k2)


def _top_values(s, k, with_rank):
    work = s
    rank = jnp.full(s.shape, float(PEER_KEYS), F32) if with_rank else None
    vals = []
    for r in range(k):
        m = jnp.max(work, axis=0, keepdims=True)
        hit = work == m
        if with_rank:
            rank = jnp.where(hit, float(r), rank)
        work = jnp.where(hit, NEG_INF, work)
        vals.append(m)
    return vals, rank


def _peer_select_kernel(s1_ref, s2_ref, r2_ref, g2_ref, cnt_ref, e1_ref, v1_scr, v2_scr):
    k = PEER_TOPK
    half = k // 2
    s1 = s1_ref[0]
    s2 = s2_ref[0]
    v1, _ = _top_values(s1, k, with_rank=False)
    v2, rank2 = _top_values(s2, k, with_rank=True)
    for a in range(k):
        v1_scr[a:a + 1, :] = v1[a]
        v2_scr[a:a + 1, :] = v2[a]
    row = lax.broadcasted_iota(jnp.int32, (half, s1.shape[1]), 0)
    pieces = [v1[0] + v2_scr[...]]
    for a in range(1, half):
        pieces.append(jnp.where(row < k // (a + 1), v1[a] + v2_scr[0:half, :], NEG_INF))
    pieces.append(v1_scr[half:k, :] + v2[0])
    top = v1[0] + v2[0]
    z = jnp.zeros_like(top)
    thr = top
    for _ in range(k):
        thr = functools.reduce(jnp.maximum, [jnp.max(p, axis=0, keepdims=True) for p in pieces])
        pieces = [jnp.where(p == thr, NEG_INF, p) for p in pieces]
        z = z + jnp.exp(thr - top)
    cnt = jnp.zeros_like(s1)
    for a in range(k):
        if a == 0:
            sums = v1[0] + v2_scr[...]
        elif a < half:
            sums = jnp.where(row < k // (a + 1), v1[a] + v2_scr[0:half, :], NEG_INF)
        else:
            sums = v1[a] + v2[0]
        n_sel = jnp.sum(jnp.where(sums >= thr, 1.0, 0.0), axis=0, keepdims=True)
        cnt = jnp.where(s1 == v1[a], n_sel, cnt)
    r2_ref[0] = rank2.astype(BF16)
    g2_ref[0] = jnp.exp(s2 - v2[0]).astype(BF16)
    cnt_ref[0] = cnt
    e1_ref[0] = jnp.exp(s1 - v1[0]) / z


def _peer_select(s1, s2):
    heads, keys, n = s1.shape
    tl = SELECT_TOKENS
    spec = pl.BlockSpec((1, keys, tl), lambda h, i: (h, 0, i))
    shp = jax.ShapeDtypeStruct((heads, keys, n), F32)
    shp16 = jax.ShapeDtypeStruct((heads, keys, n), BF16)
    return pl.pallas_call(
        _peer_select_kernel,
        grid=(heads, n // tl),
        in_specs=[spec, spec], out_specs=[spec] * 4, out_shape=[shp16, shp16, shp, shp],
        scratch_shapes=[pltpu.VMEM((PEER_TOPK, tl), F32), pltpu.VMEM((PEER_TOPK, tl), F32)],
        compiler_params=_cparams(("arbitrary", "arbitrary")),
        name="peer_select",
    )(s1, s2)


def _peer_dense_kernel(x_ref, u_ref, vt_ref, r2_ref, g2_ref, cnt_ref, e1_ref, s_ref, mod_ref, fg_ref,
                       o_ref, acc_scr, a_scr, w_scr, coef_even, coef_odd, x_scr, r2_scr, g2_scr, row_scr,
                       *, te, nj, blocks_per_batch, n_ctx, final):
    n = pl.program_id(0)
    tm = x_ref.shape[1]
    n_tiles = tm // PEER_TOK
    keys_per_step = te // PEER_KEYS
    assert keys_per_step % F32_SUBLANES == 0

    @pl.when(n == 0)
    def _():
        acc_scr[...] = jnp.zeros_like(acc_scr)
        coef_odd[...] = jnp.zeros_like(coef_odd)

    @pl.when(n % nj == 0)
    def _():
        for t in range(n_tiles):
            tok = slice(t * PEER_TOK, (t + 1) * PEER_TOK)
            x_scr[t] = x_ref[:, tok]
            r2_scr[t] = r2_ref[:, :, tok]
            g2_scr[t] = g2_ref[:, :, tok]

    for t in range(n_tiles):
        tok = slice(t * PEER_TOK, (t + 1) * PEER_TOK)
        row_scr[t, 0] = cnt_ref[:, :, tok]
        row_scr[t, 1] = e1_ref[:, :, tok]

    def step(read_ref, write_ref):
        zero = jnp.zeros((PEER_KEYS, LANES), BF16)

        def tile(t, carry):
            a_scr[...] = _dot(pltpu.bitcast(u_ref[...], BF16), x_scr[t]).astype(BF16)
            acc_scr[t] += _dot(pltpu.bitcast(vt_ref[...], BF16), read_ref[t])
            for s in range(keys_per_step):
                rows = slice(s * PEER_KEYS, (s + 1) * PEER_KEYS)
                for l in range(PEER_TOK // LANES):
                    lanes = slice(l * LANES, (l + 1) * LANES)
                    w = zero
                    for h in range(PEER_HEADS):
                        cnt = row_scr[t, 0, h, s:s + 1, lanes].astype(BF16)
                        e1 = row_scr[t, 1, h, s:s + 1, lanes].astype(BF16)
                        w = w + jnp.where(r2_scr[t, h, :, lanes] < cnt, g2_scr[t, h, :, lanes], zero) * e1
                    w_scr[rows, lanes] = w
            for s in range(keys_per_step):
                rows = slice(s * PEER_KEYS, (s + 1) * PEER_KEYS)
                write_ref[t, rows, :] = w_scr[rows, :] * _gelu(a_scr[rows, :])
            return carry

        lax.fori_loop(0, n_tiles, tile, 0)

    @pl.when(n % 2 == 0)
    def _():
        step(coef_odd, coef_even)

    @pl.when(n % 2 == 1)
    def _():
        step(coef_even, coef_odd)

    prev = n - 1

    @pl.when(jnp.logical_and(n > 0, prev % nj == nj - 1))
    def _():
        ip = prev // nj
        b = ip // blocks_per_batch
        for t in range(n_tiles):
            tok = slice(t * PEER_TOK, (t + 1) * PEER_TOK)
            if n_ctx:
                row = (ip % blocks_per_batch) * tm + t * PEER_TOK + lax.broadcasted_iota(jnp.int32, (PEER_TOK, 1), 0)
                is_ctx = row < n_ctx
            else:
                is_ctx = None
            g2 = _mod_rows(mod_ref, 5, b, is_ctx)
            x = s_ref[tok, :] + g2 * acc_scr[t].T
            if final:
                x = _rms(x, fg_ref[...])
            o_ref[tok, :] = x
        acc_scr[...] = jnp.zeros_like(acc_scr)


def _pack_tables_kernel(u_ref, v_ref, uo_ref, vo_ref):
    uo_ref[...] = pltpu.bitcast(u_ref[...].astype(BF16), F32)
    vo_ref[...] = pltpu.bitcast(v_ref[...].T.astype(BF16), F32)


def _pack_tables(u, v):
    e, d = u.shape
    te = PACK_EXPERTS
    return pl.pallas_call(
        _pack_tables_kernel,
        grid=(e // te,),
        in_specs=[pl.BlockSpec((te, d), lambda j: (j, 0)), pl.BlockSpec((te, d), lambda j: (j, 0))],
        out_specs=[pl.BlockSpec((te // 2, d), lambda j: (j, 0)), pl.BlockSpec((d // 2, te), lambda j: (0, j))],
        out_shape=[jax.ShapeDtypeStruct((e // 2, d), F32), jax.ShapeDtypeStruct((d // 2, e), F32)],
        compiler_params=_cparams(("arbitrary",)),
        name="pack_tables",
    )(u, v)


def _peer_dense(h2_t, u_words, vt_words, sel, stream, mod_l, final_g, rows_per_batch, n_ctx, final):
    n = h2_t.shape[1]
    r2, g2, cnt, e1 = sel
    tm = next(t for t in DENSE_TOKENS if rows_per_batch % t == 0)
    te = DENSE_EXPERTS[tm]
    bpb = rows_per_batch // tm
    nj = PEER_EXPERTS // te
    n_steps = (n // tm) * nj

    def cur(s):
        c = jnp.minimum(s, n_steps - 1)
        return c // nj, c % nj

    def prev(s):
        p = jnp.maximum(s - 1, 0)
        return p // nj, p % nj

    tok = pl.BlockSpec((PEER_HEADS, PEER_KEYS, tm), lambda s: (0, 0, cur(s)[0]))
    sub = pl.BlockSpec((PEER_HEADS, te // PEER_KEYS, tm), lambda s: (0, cur(s)[1], cur(s)[0]))
    full = lambda a: pl.BlockSpec(a.shape, lambda s: (0,) * a.ndim)
    fg = final_g.reshape(1, D_MODEL)
    n_tiles = tm // PEER_TOK
    coef = pltpu.VMEM((n_tiles, te, PEER_TOK), BF16)
    sel16 = pltpu.VMEM((n_tiles, PEER_HEADS, PEER_KEYS, PEER_TOK), BF16)
    return pl.pallas_call(
        functools.partial(_peer_dense_kernel, te=te, nj=nj, blocks_per_batch=bpb, n_ctx=n_ctx, final=final),
        grid=(n_steps + 1,),
        in_specs=[pl.BlockSpec((D_MODEL, tm), lambda s: (0, cur(s)[0])),
                  pl.BlockSpec((te // 2, D_MODEL), lambda s: (cur(s)[1], 0)),
                  pl.BlockSpec((D_MODEL // 2, te), lambda s: (0, prev(s)[1])),
                  tok, tok, sub, sub,
                  pl.BlockSpec((tm, D_MODEL), lambda s: (prev(s)[0], 0)), full(mod_l), full(fg)],
        out_specs=pl.BlockSpec((tm, D_MODEL), lambda s: (prev(s)[0], 0)),
        out_shape=jax.ShapeDtypeStruct((n, D_MODEL), F32),
        scratch_shapes=[pltpu.VMEM((n_tiles, D_MODEL, PEER_TOK), F32), pltpu.VMEM((te, PEER_TOK), BF16),
                        pltpu.VMEM((te, PEER_TOK), BF16), coef, coef,
                        pltpu.VMEM((n_tiles, D_MODEL, PEER_TOK), BF16), sel16, sel16,
                        pltpu.VMEM((n_tiles, 2, PEER_HEADS, te // PEER_KEYS, PEER_TOK), F32)],
        compiler_params=_cparams(("arbitrary",)),
        name="peer_dense",
    )(h2_t, u_words, vt_words, r2, g2, cnt, e1, stream, mod_l, fg)


def _reorder_w_in(w):
    offs = [0]
    for width in (SSM_WIDTH, ATT_WIDTH, KV_WIDTH, KV_WIDTH, ML_WIDTH, ML_WIDTH, ML_WIDTH, ML_WIDTH, ML_GATES,
                  N_BRANCH * D_MODEL):
        offs.append(offs[-1] + width)
    u_s, q_a, k_a, v_a, q_m, k_m, v_m, o_m, g_m, gate = (w[:, offs[n]:offs[n + 1]] for n in range(10))
    pad = jnp.zeros((w.shape[0], PROJ_WIDTH - COL_GM - ML_GATES), w.dtype)
    return jnp.concatenate([u_s, q_a, q_m, k_m, v_m, o_m, gate, k_a, v_a, g_m, pad], axis=1).astype(BF16)


def kernel(x, c, ctx, c_ctx, w_mod, b_mod, norm1_g, norm2_g, w_in, ssm_lam_re, ssm_lam_im, ssm_log_step, ssm_b_re, ssm_b_im, ssm_c_re, ssm_c_im, ssm_d, ssm_w_glu, attn_q_norm_g, attn_k_norm_g, mlstm_gate_b, mlstm_norm_g, w_branch_ssm, w_branch_attn, w_branch_mlstm, w_out, peer_w_q, peer_sub_k1, peer_sub_k2, peer_u, peer_v, final_norm_g):
    bsz, lat_len, d = x.shape
    n_ctx = ctx.shape[1]
    s_tot = n_ctx + lat_len
    depth = w_in.shape[0]
    assert d == D_MODEL and n_ctx == 256 and lat_len % n_ctx == 0 and bsz == 8

    cos, sin = _rope_tables(lat_len)
    mod = _modulation(c, c_ctx, w_mod, b_mod)
    stream = jnp.concatenate([ctx, x], axis=1)

    for layer in range(depth):
        last = layer == depth - 1
        mod_l = mod[layer]
        proj = _inproj(stream.reshape(bsz, s_tot, d), norm1_g[layer], mod_l, _reorder_w_in(w_in[layer]), n_ctx)

        u_tm = proj[:, COL_U:COL_U + SSM_WIDTH].reshape(bsz, s_tot, SSM_WIDTH).transpose(1, 0, 2)
        u_tm = u_tm.reshape(s_tot * bsz, SSM_WIDTH)
        params = _s5_params(ssm_lam_re[layer], ssm_lam_im[layer], ssm_log_step[layer], ssm_b_re[layer],
                            ssm_b_im[layer], ssm_c_re[layer], ssm_c_im[layer])
        yf, yb = _s5_scan(u_tm, params, bsz, s_tot, n_ctx)
        ys_tm = _s5_post(u_tm, yf, yb, ssm_d[layer], ssm_w_glu[layer].astype(BF16))
        ys = ys_tm.reshape(s_tot, bsz, SSM_WIDTH).transpose(1, 0, 2).reshape(bsz * s_tot, SSM_WIDTH)

        q_ctx, q_lat, kn, vn = _qkprep(proj, cos, sin, attn_q_norm_g[layer], attn_k_norm_g[layer], bsz, s_tot, n_ctx)
        ya = _attention(q_lat, kn, vn, bsz, s_tot, ATT_Q_BLOCK, s_tot)
        if not last:
            ya_ctx = _attention(q_ctx, kn, vn, bsz, s_tot, n_ctx, n_ctx)
            ya = jnp.concatenate([ya_ctx.reshape(bsz, n_ctx, ATT_WIDTH), ya.reshape(bsz, lat_len, ATT_WIDTH)],
                                 axis=1).reshape(bsz * s_tot, ATT_WIDTH)

        hf, hb = _mlstm(proj, mlstm_gate_b[layer], bsz, s_tot, n_ctx)

        stream2, h2_t = _merge(ys, ya, hf, hb, proj, stream.reshape(bsz * s_tot, d), mod_l, mlstm_norm_g[layer],
                             norm2_g[layer], w_branch_ssm[layer].astype(BF16), w_branch_attn[layer].astype(BF16),
                             w_branch_mlstm[layer].astype(BF16), w_out[layer].astype(BF16),
                             bsz, s_tot, n_ctx, last)

        s1, s2 = _peer_scores(h2_t, peer_w_q[layer].T.astype(BF16), peer_sub_k1[layer], peer_sub_k2[layer])
        sel = _peer_select(s1, s2)
        rows_per_batch = lat_len if last else s_tot
        u_words, vt_words = _pack_tables(peer_u[layer], peer_v[layer])
        stream = _peer_dense(h2_t, u_words, vt_words, sel, stream2, mod_l,
                             final_norm_g, rows_per_batch, 0 if last else n_ctx, last)

    return stream.reshape(bsz, lat_len, d)
```
